```python
import math
import jax, jax.numpy as jnp
from jax import lax
import numpy as np

D_MODEL = 2048
BATCH = 2
SEQ = 4096
DEPTH = 1
DEC_BATCH = 32
DEC_SEQ = 16
PAST_LEN = 2048

CHUNK = 64
Q_BLOCK = 128
N_HEADS_A = 16
Q_LORA = 768
KV_LORA = 512
QK_NOPE = 128
QK_ROPE = 64
V_DIM = 128
ROPE_THETA = 10000.0
ATTN_SCALE = (QK_NOPE + QK_ROPE) ** -0.5
SSM_EXPAND = 2
D_INNER = SSM_EXPAND * D_MODEL
HEAD_DIM_S = 64
N_HEADS_S = D_INNER // HEAD_DIM_S
N_GROUPS_S = 8
D_STATE = 128
CONV_W = 4
CONV_DIM = D_INNER + 2 * N_GROUPS_S * D_STATE
SSD_CHUNK = 64
D_FF = 5632
EPS = 1e-6
N_BRANCH = 2
SPLITS = (Q_LORA, KV_LORA + QK_ROPE, D_INNER, CONV_DIM, N_HEADS_S, N_BRANCH * D_MODEL)
D_IN_PROJ = Q_LORA + KV_LORA + QK_ROPE + D_INNER + CONV_DIM + N_HEADS_S + N_BRANCH * D_MODEL

kernel_name = 'streaming_mla_mamba2_macaron_step'


def rmsnorm(x, g):
    xf = x.astype(jnp.float32)
    y = xf * lax.rsqrt(jnp.mean(xf * xf, axis=-1, keepdims=True) + EPS)
    return (y * g.astype(jnp.float32)).astype(x.dtype)


def rope(x, pos):
    half = QK_ROPE // 2
    inv = jnp.power(ROPE_THETA, -jnp.arange(half, dtype=jnp.float32) / half)
    ang = pos.astype(jnp.float32)[:, None] * inv[None, :]
    if x.ndim == 4:
        ang = ang[:, None, :]
    cos, sin = jnp.cos(ang), jnp.sin(ang)
    x1 = x[..., :half].astype(jnp.float32)
    x2 = x[..., half:].astype(jnp.float32)
    return jnp.concatenate([x1 * cos - x2 * sin, x1 * sin + x2 * cos], axis=-1).astype(x.dtype)


def swiglu(h, w_gate, w_up, w_down):
    return (jax.nn.silu(h @ w_gate) * (h @ w_up)) @ w_down


def split_cols(t, sizes):
    out, off = [], 0
    for s in sizes:
        out.append(t[..., off:off + s])
        off += s
    return out


def causal_conv(xbc, conv_state, w, b):
    full = jnp.concatenate([conv_state.astype(xbc.dtype), xbc], axis=1)
    T = xbc.shape[1]
    acc = b + w[0] * full[:, 0:T]
    for k in range(1, CONV_W):
        acc = acc + w[k] * full[:, k:k + T]
    return jax.nn.silu(acc), full[:, -(CONV_W - 1):]


def ssd(x, a, b_in, c_in, init_state, q):
    f32 = jnp.float32
    Bt, L, H, P = x.shape
    G, N = b_in.shape[-2], b_in.shape[-1]
    R = H // G
    nc = L // q
    x = x.astype(f32).reshape(Bt, nc, q, G, R, P)
    b_in = b_in.astype(f32).reshape(Bt, nc, q, G, N)
    c_in = c_in.astype(f32).reshape(Bt, nc, q, G, N)
    a = a.astype(f32).reshape(Bt, nc, q, G, R).transpose(0, 3, 4, 1, 2)
    a_cs = jnp.cumsum(a, axis=-1)
    causal = jnp.tril(jnp.ones((q, q), dtype=bool))
    seg = a_cs[..., :, None] - a_cs[..., None, :]
    decay = jnp.exp(jnp.where(causal, seg, -jnp.inf))
    cb = jnp.einsum('bclgn,bcsgn->bgcls', c_in, b_in)
    y_diag = jnp.einsum('bgrcls,bcsgrp->bclgrp', cb[:, :, None] * decay, x)
    to_end = jnp.exp(a_cs[..., -1:] - a_cs)
    chunk_states = jnp.einsum('bcsgn,bgrcs,bcsgrp->cbgrpn', b_in, to_end, x)
    chunk_decay = jnp.exp(a_cs[..., -1]).transpose(3, 0, 1, 2)

    def step(s, inp):
        dec, st = inp
        return dec[..., None, None] * s + st, s

    s0 = init_state.astype(f32).reshape(Bt, G, R, P, N)
    final, starts = lax.scan(step, s0, (chunk_decay, chunk_states))
    y_off = jnp.einsum('bclgn,cbgrpn,bgrcl->bclgrp', c_in, starts, jnp.exp(a_cs))
    y = (y_diag + y_off).reshape(Bt, L, H, P)
    return y, final.reshape(Bt, H, P, N)


def gated_group_rmsnorm(y, z, g):
    v = y.astype(jnp.float32) * jax.nn.silu(z.astype(jnp.float32))
    v = v.reshape(*y.shape[:-1], N_GROUPS_S, D_INNER // N_GROUPS_S)
    v = v * lax.rsqrt(jnp.mean(v * v, axis=-1, keepdims=True) + EPS)
    return (v.reshape(y.shape) * g.astype(jnp.float32)).astype(y.dtype)


def mla_prompt_attention(q_nope, q_rope, c_kv, k_rope, w_uk, w_uv):
    B, S, H, _ = q_nope.shape
    k_nope = jnp.einsum('bsc,chd->bshd', c_kv, w_uk)
    v = jnp.einsum('bsc,chd->bshd', c_kv, w_uv)
    nb = S // Q_BLOCK
    chunk_id = jnp.arange(S) // CHUNK
    qn = q_nope.reshape(B, nb, Q_BLOCK, H, QK_NOPE).transpose(1, 0, 2, 3, 4)
    qr = q_rope.reshape(B, nb, Q_BLOCK, H, QK_ROPE).transpose(1, 0, 2, 3, 4)
    qc = chunk_id.reshape(nb, Q_BLOCK)

    def block(args):
        qn_b, qr_b, qc_b = args
        s = (jnp.einsum('bqhd,bkhd->bhqk', qn_b, k_nope)
             + jnp.einsum('bqhd,bkd->bhqk', qr_b, k_rope)).astype(jnp.float32) * ATTN_SCALE
        mask = chunk_id[None, :] <= qc_b[:, None]
        p = jax.nn.softmax(jnp.where(mask, s, -jnp.inf), axis=-1).astype(v.dtype)
        return jnp.einsum('bhqk,bkhd->bqhd', p, v)

    o = lax.map(block, (qn, qr, qc))
    return o.transpose(1, 0, 2, 3, 4).reshape(B, S, H, V_DIM)


def mla_sample_attention(q_nope, q_rope, lat_all, krope_all, w_uk, w_uv):
    q_lat = jnp.einsum('bthd,chd->bthc', q_nope, w_uk)
    s = (jnp.einsum('bthc,bkc->bhtk', q_lat, lat_all)
         + jnp.einsum('bthd,bkd->bhtk', q_rope, krope_all)).astype(jnp.float32) * ATTN_SCALE
    p = jax.nn.softmax(s, axis=-1).astype(lat_all.dtype)
    o_lat = jnp.einsum('bhtk,bkc->bthc', p, lat_all)
    return jnp.einsum('bthc,chd->bthd', o_lat, w_uv)


def token_mixer(h, pos, cache_lat, cache_krope, ssm_state, conv_state, w_in, b_gate, n_qa, w_q_b,
                n_kva, w_kv_b, conv_w, conv_b, dt_bias, a_log, d_skip, n_ssm, w_o_attn, w_o_ssm, w_out):
    B, T, _ = h.shape
    q_a, kv_a, z, xbc, dt, gates = split_cols(h @ w_in, SPLITS)
    c_q = rmsnorm(q_a, n_qa)
    q = (c_q @ w_q_b).reshape(B, T, N_HEADS_A, QK_NOPE + QK_ROPE)
    q_nope = q[..., :QK_NOPE]
    q_rope = rope(q[..., QK_NOPE:], pos)
    c_kv = rmsnorm(kv_a[..., :KV_LORA], n_kva)
    k_rope = rope(kv_a[..., KV_LORA:], pos)
    w_kv = w_kv_b.reshape(KV_LORA, N_HEADS_A, QK_NOPE + V_DIM)
    w_uk, w_uv = w_kv[..., :QK_NOPE], w_kv[..., QK_NOPE:]
    if cache_lat is None:
        o_attn = mla_prompt_attention(q_nope, q_rope, c_kv, k_rope, w_uk, w_uv)
    else:
        lat_all = jnp.concatenate([cache_lat.astype(c_kv.dtype), c_kv], axis=1)
        krope_all = jnp.concatenate([cache_krope.astype(k_rope.dtype), k_rope], axis=1)
        o_attn = mla_sample_attention(q_nope, q_rope, lat_all, krope_all, w_uk, w_uv)
    attn_branch = o_attn.reshape(B, T, N_HEADS_A * V_DIM) @ w_o_attn
    xbc, new_conv = causal_conv(xbc, conv_state, conv_w, conv_b)
    xs, bs, cs = split_cols(xbc, (D_INNER, N_GROUPS_S * D_STATE, N_GROUPS_S * D_STATE))
    dt = jax.nn.softplus(dt.astype(jnp.float32) + dt_bias.astype(jnp.float32))
    A = -jnp.exp(a_log.astype(jnp.float32))
    xs_h = xs.reshape(B, T, N_HEADS_S, HEAD_DIM_S)
    q_len = SSD_CHUNK if cache_lat is None else T
    y, new_ssm = ssd(xs_h * dt[..., None], dt * A, bs.reshape(B, T, N_GROUPS_S, D_STATE),
                     cs.reshape(B, T, N_GROUPS_S, D_STATE), ssm_state, q_len)
    y = (y + d_skip.astype(jnp.float32)[:, None] * xs_h.astype(jnp.float32)).astype(h.dtype)
    y = gated_group_rmsnorm(y.reshape(B, T, D_INNER), z, n_ssm)
    ssm_branch = y @ w_o_ssm
    g = jax.nn.sigmoid(gates.reshape(B, T, N_BRANCH, D_MODEL) + b_gate)
    merged = g[..., 0, :] * attn_branch + g[..., 1, :] * ssm_branch
    return merged @ w_out, c_kv, k_rope, new_ssm, new_conv


def trunk_layer(x, pos, cache_lat, cache_krope, ssm_state, conv_state, lw):
    (n_f1, wg1, wu1, wd1, n_mix, w_in, b_gate, n_qa, w_q_b, n_kva, w_kv_b, conv_w, conv_b,
     dt_bias, a_log, d_skip, n_ssm, w_o_attn, w_o_ssm, w_out, n_f2, wg2, wu2, wd2) = lw
    x = x + 0.5 * swiglu(rmsnorm(x, n_f1), wg1, wu1, wd1)
    m, c_kv, k_rope, new_ssm, new_conv = token_mixer(
        rmsnorm(x, n_mix), pos, cache_lat, cache_krope, ssm_state, conv_state, w_in, b_gate, n_qa,
        w_q_b, n_kva, w_kv_b, conv_w, conv_b, dt_bias, a_log, d_skip, n_ssm, w_o_attn, w_o_ssm, w_out)
    x = x + m
    x = x + 0.5 * swiglu(rmsnorm(x, n_f2), wg2, wu2, wd2)
    return x, c_kv, k_rope, new_ssm, new_conv


def setup_inputs(seed: int = 0) -> dict:
    key = jax.random.key(seed)
    ks = jax.random.split(key, 40)
    f32 = jnp.float32
    L = DEPTH

    def nrm(k, shape, scale):
        return jax.random.normal(k, shape, f32) * scale

    def gain(k, n):
        return 1.0 + nrm(k, (L, n), 0.1)

    dt0 = jnp.exp(jax.random.uniform(ks[30], (L, N_HEADS_S), f32, math.log(1e-3), math.log(1e-1)))
    return {
        'x_prompt': nrm(ks[0], (BATCH, SEQ, D_MODEL), 1.0),
        'x_sample': nrm(ks[1], (DEC_BATCH, DEC_SEQ, D_MODEL), 1.0),
        'cache_kv_latent': nrm(ks[2], (L, DEC_BATCH, PAST_LEN, KV_LORA), 1.0),
        'cache_k_rope': nrm(ks[3], (L, DEC_BATCH, PAST_LEN, QK_ROPE), 1.0),
        'state_ssm': nrm(ks[4], (L, DEC_BATCH, N_HEADS_S, HEAD_DIM_S, D_STATE), 0.1),
        'state_conv': nrm(ks[5], (L, DEC_BATCH, CONV_W - 1, CONV_DIM), 1.0),
        'norm_ffn1': gain(ks[6], D_MODEL),
        'w_ffn1_gate': nrm(ks[7], (L, D_MODEL, D_FF), D_MODEL ** -0.5),
        'w_ffn1_up': nrm(ks[8], (L, D_MODEL, D_FF), D_MODEL ** -0.5),
        'w_ffn1_down': nrm(ks[9], (L, D_FF, D_MODEL), D_FF ** -0.5),
        'norm_mix': gain(ks[10], D_MODEL),
        'w_in': nrm(ks[11], (L, D_MODEL, D_IN_PROJ), D_MODEL ** -0.5),
        'b_gate': nrm(ks[12], (L, N_BRANCH, D_MODEL), 0.1),
        'norm_q_a': gain(ks[13], Q_LORA),
        'w_q_b': nrm(ks[14], (L, Q_LORA, N_HEADS_A * (QK_NOPE + QK_ROPE)), Q_LORA ** -0.5),
        'norm_kv_a': gain(ks[15], KV_LORA),
        'w_kv_b': nrm(ks[16], (L, KV_LORA, N_HEADS_A * (QK_NOPE + V_DIM)), KV_LORA ** -0.5),
        'conv_w': nrm(ks[17], (L, CONV_W, CONV_DIM), CONV_W ** -0.5),
        'conv_b': nrm(ks[18], (L, CONV_DIM), 0.1),
        'dt_bias': dt0 + jnp.log(-jnp.expm1(-dt0)),
        'a_log': jnp.log(jax.random.uniform(ks[19], (L, N_HEADS_S), f32, 1.0, 16.0)),
        'd_skip': 1.0 + nrm(ks[20], (L, N_HEADS_S), 0.1),
        'norm_ssm': gain(ks[21], D_INNER),
        'w_o_attn': nrm(ks[22], (L, N_HEADS_A * V_DIM, D_MODEL), (N_HEADS_A * V_DIM) ** -0.5),
        'w_o_ssm': nrm(ks[23], (L, D_INNER, D_MODEL), D_INNER ** -0.5),
        'w_out': nrm(ks[24], (L, D_MODEL, D_MODEL), D_MODEL ** -0.5),
        'norm_ffn2': gain(ks[25], D_MODEL),
        'w_ffn2_gate': nrm(ks[26], (L, D_MODEL, D_FF), D_MODEL ** -0.5),
        'w_ffn2_up': nrm(ks[27], (L, D_MODEL, D_FF), D_MODEL ** -0.5),
        'w_ffn2_down': nrm(ks[28], (L, D_FF, D_MODEL), D_FF ** -0.5),
        'norm_final': 1.0 + nrm(ks[29], (D_MODEL,), 0.1),
    }


def reference(x_prompt, x_sample, cache_kv_latent, cache_k_rope, state_ssm, state_conv,
              norm_ffn1, w_ffn1_gate, w_ffn1_up, w_ffn1_down, norm_mix, w_in, b_gate,
              norm_q_a, w_q_b, norm_kv_a, w_kv_b, conv_w, conv_b, dt_bias, a_log, d_skip,
              norm_ssm, w_o_attn, w_o_ssm, w_out, norm_ffn2, w_ffn2_gate, w_ffn2_up,
              w_ffn2_down, norm_final):
    yp, ys = x_prompt, x_sample
    Bp, S = x_prompt.shape[0], x_prompt.shape[1]
    Bs, T = x_sample.shape[0], x_sample.shape[1]
    past = cache_kv_latent.shape[2]
    pos_p = jnp.arange(S)
    pos_s = past + jnp.arange(T)
    p_lat, p_kr, p_ssm, p_conv = [], [], [], []
    s_lat, s_kr, s_ssm, s_conv = [], [], [], []
    for l in range(DEPTH):
        lw = (norm_ffn1[l], w_ffn1_gate[l], w_ffn1_up[l], w_ffn1_down[l], norm_mix[l], w_in[l],
              b_gate[l], norm_q_a[l], w_q_b[l], norm_kv_a[l], w_kv_b[l], conv_w[l], conv_b[l],
              dt_bias[l], a_log[l], d_skip[l], norm_ssm[l], w_o_attn[l], w_o_ssm[l], w_out[l],
              norm_ffn2[l], w_ffn2_gate[l], w_ffn2_up[l], w_ffn2_down[l])
        yp, c_kv, k_r, st, cv = trunk_layer(
            yp, pos_p, None, None,
            jnp.zeros((Bp, N_HEADS_S, HEAD_DIM_S, D_STATE), jnp.float32),
            jnp.zeros((Bp, CONV_W - 1, CONV_DIM), yp.dtype), lw)
        p_lat.append(c_kv)
        p_kr.append(k_r)
        p_ssm.append(st)
        p_conv.append(cv)
        ys, c_kv, k_r, st, cv = trunk_layer(
            ys, pos_s, cache_kv_latent[l], cache_k_rope[l], state_ssm[l], state_conv[l], lw)
        s_lat.append(c_kv)
        s_kr.append(k_r)
        s_ssm.append(st)
        s_conv.append(cv)
    y_prompt = rmsnorm(yp, norm_final)
    y_sample = rmsnorm(ys, norm_final)
    return (y_prompt, y_sample,
            jnp.stack(p_lat), jnp.stack(p_kr), jnp.stack(p_ssm), jnp.stack(p_conv),
            jnp.stack(s_lat), jnp.stack(s_kr), jnp.stack(s_ssm), jnp.stack(s_conv))
```

```python
import functools
import math

import jax
import jax.numpy as jnp
from jax import lax
from jax.experimental import pallas as pl
from jax.experimental.pallas import tpu as pltpu

F32 = jnp.float32
BF16 = jnp.bfloat16

D_MODEL = 2048
CHUNK = 64
N_HEADS_A = 16
Q_LORA = 768
KV_LORA = 512
QK_NOPE = 128
QK_ROPE = 64
V_DIM = 128
ROPE_THETA = 10000.0
ATTN_SCALE = (QK_NOPE + QK_ROPE) ** -0.5
D_INNER = 2 * D_MODEL
HEAD_DIM_S = 64
N_HEADS_S = D_INNER // HEAD_DIM_S
N_GROUPS_S = 8
HEADS_PER_GROUP = N_HEADS_S // N_GROUPS_S
GROUP_W = D_INNER // N_GROUPS_S
D_STATE = 128
CONV_W = 4
CONV_DIM = D_INNER + 2 * N_GROUPS_S * D_STATE
D_FF = 5632
EPS = 1e-6

QK_HEAD = 256
COL_QA = 0
COL_LAT = Q_LORA
COL_KR = COL_LAT + KV_LORA
COL_DT = COL_KR + 2 * QK_ROPE
SMALL_W = COL_DT + 128
COL_Z = SMALL_W
COL_X = COL_Z + D_INNER
COL_B = COL_X + D_INNER
COL_C = COL_B + N_GROUPS_S * D_STATE
COL_GATE = COL_C + N_GROUPS_S * D_STATE
PROJ_W = COL_GATE + 2 * D_MODEL

VMEM_LIMIT = 56 * 1024 * 1024


def _cparams(*sem):
    return pltpu.CompilerParams(dimension_semantics=sem, vmem_limit_bytes=VMEM_LIMIT)


def _rms(x, g):
    return x * lax.rsqrt(jnp.mean(x * x, axis=-1, keepdims=True) + EPS) * g


def _silu(x):
    return x * jax.nn.sigmoid(x)


def _dot(a, b):
    return jnp.dot(a, b, preferred_element_type=F32)


def _dot_nt(a, b):
    return lax.dot_general(a, b, (((1,), (1,)), ((), ())), preferred_element_type=F32)


def _dot_tn(a, b):
    return lax.dot_general(a, b, (((0,), (0,)), ((), ())), preferred_element_type=F32)


def _ffn_kernel(*refs, final_norm):
    if final_norm:
        x_ref, g_ref, wg_ref, wu_ref, wd_ref, gf_ref, o_ref, h_scr, acc_scr = refs
    else:
        x_ref, g_ref, wg_ref, wu_ref, wd_ref, o_ref, h_scr, acc_scr = refs
    j = pl.program_id(1)

    @pl.when(j == 0)
    def _():
        h_scr[...] = _rms(x_ref[...], g_ref[...]).astype(BF16)
        acc_scr[...] = jnp.zeros_like(acc_scr)

    h = h_scr[...]
    a = _silu(_dot(h, wg_ref[...])) * _dot(h, wu_ref[...])
    acc_scr[...] += _dot(a.astype(BF16), wd_ref[...])

    @pl.when(j == pl.num_programs(1) - 1)
    def _():
        y = x_ref[...] + 0.5 * acc_scr[...]
        if final_norm:
            y = _rms(y, gf_ref[...])
        o_ref[...] = y


def _ffn(x, g, wg, wu, wd, gf=None, *, tm=512, tf=512):
    M, D = x.shape
    F = wg.shape[1]
    tm = min(tm, M)
    in_specs = [
        pl.BlockSpec((tm, D), lambda i, j: (i, 0)),
        pl.BlockSpec((1, D), lambda i, j: (0, 0)),
        pl.BlockSpec((D, tf), lambda i, j: (0, j)),
        pl.BlockSpec((D, tf), lambda i, j: (0, j)),
        pl.BlockSpec((tf, D), lambda i, j: (j, 0)),
    ]
    args = [x, g.reshape(1, D), wg, wu, wd]
    if gf is not None:
        in_specs.append(pl.BlockSpec((1, D), lambda i, j: (0, 0)))
        args.append(gf.reshape(1, D))
    return pl.pallas_call(
        functools.partial(_ffn_kernel, final_norm=gf is not None),
        grid=(M // tm, F // tf),
        in_specs=in_specs,
        out_specs=pl.BlockSpec((tm, D), lambda i, j: (i, 0)),
        out_shape=jax.ShapeDtypeStruct((M, D), F32),
        scratch_shapes=[pltpu.VMEM((tm, D), BF16), pltpu.VMEM((tm, D), F32)],
        compiler_params=_cparams("parallel", "arbitrary"),
        name="ffn",
    )(*args)


def _inproj_kernel(x_ref, g_ref, w_ref, o_ref, h_scr):
    @pl.when(pl.program_id(1) == 0)
    def _():
        h_scr[...] = _rms(x_ref[...], g_ref[...]).astype(BF16)

    o_ref[...] = _dot(h_scr[...], w_ref[...])


def _inproj(x, g, w, *, tm=512, tn=512):
    M, D = x.shape
    N = w.shape[1]
    tm = min(tm, M)
    return pl.pallas_call(
        _inproj_kernel,
        grid=(M // tm, N // tn),
        in_specs=[
            pl.BlockSpec((tm, D), lambda i, j: (i, 0)),
            pl.BlockSpec((1, D), lambda i, j: (0, 0)),
            pl.BlockSpec((D, tn), lambda i, j: (0, j)),
        ],
        out_specs=pl.BlockSpec((tm, tn), lambda i, j: (i, j)),
        out_shape=jax.ShapeDtypeStruct((M, N), F32),
        scratch_shapes=[pltpu.VMEM((tm, D), BF16)],
        compiler_params=_cparams("parallel", "arbitrary"),
        name="inproj",
    )(x, g.reshape(1, D), w)


def _rope_pair(blk, cs, keep):
    prod = blk * cs
    return jnp.where(keep, prod + pltpu.roll(prod, 64, axis=1), 0.0)


def _prep_common(small_ref, cs_ref, nq_ref, nkv_ref, wq_ref, ckv_ref, kr_ref):
    sm = small_ref[...]
    cs = cs_ref[...]
    keep = lax.broadcasted_iota(jnp.int32, cs.shape, 1) < QK_ROPE
    cq = _rms(sm[:, COL_QA:COL_QA + Q_LORA], nq_ref[...]).astype(BF16)
    q = _dot(cq, wq_ref[...]) * ATTN_SCALE
    ckv = _rms(sm[:, COL_LAT:COL_LAT + KV_LORA], nkv_ref[...])
    ckv_ref[...] = ckv
    krot = _rope_pair(sm[:, COL_KR:COL_KR + 128], cs, keep)
    kr_ref[...] = krot[:, :QK_ROPE]
    return q, ckv, krot, cs, keep


def _prep_prompt_kernel(small_ref, cs_ref, nq_ref, nkv_ref, wq_ref, wuk_ref, wuv_ref,
                        ckv_ref, kr_ref, q_ref, k_ref, v_ref):
    q, ckv, krot, cs, keep = _prep_common(small_ref, cs_ref, nq_ref, nkv_ref, wq_ref, ckv_ref, kr_ref)
    ckv_b = ckv.astype(BF16)
    kn = _dot(ckv_b, wuk_ref[...])
    krot_b = krot.astype(BF16)
    for h in range(N_HEADS_A):
        c0 = h * QK_HEAD
        q_ref[:, c0:c0 + 128] = q[:, c0:c0 + 128].astype(BF16)
        q_ref[:, c0 + 128:c0 + 256] = _rope_pair(q[:, c0 + 128:c0 + 256], cs, keep).astype(BF16)
        k_ref[:, c0:c0 + 128] = kn[:, h * 128:(h + 1) * 128].astype(BF16)
        k_ref[:, c0 + 128:c0 + 256] = krot_b
    v_ref[...] = _dot(ckv_b, wuv_ref[...]).astype(BF16)


def _prep_sample_kernel(small_ref, cs_ref, nq_ref, nkv_ref, wq_ref, wukt_ref,
                        ckv_ref, kr_ref, qlat_ref, qrot_ref):
    q, _, _, cs, keep = _prep_common(small_ref, cs_ref, nq_ref, nkv_ref, wq_ref, ckv_ref, kr_ref)
    for h in range(N_HEADS_A):
        c0 = h * QK_HEAD
        qn = q[:, c0:c0 + 128].astype(BF16)
        qlat_ref[:, h * KV_LORA:(h + 1) * KV_LORA] = _dot(qn, wukt_ref[h]).astype(BF16)
        qrot_ref[:, h * 128:(h + 1) * 128] = _rope_pair(q[:, c0 + 128:c0 + 256], cs, keep).astype(BF16)


def _prep(proj, cs_table, nq, nkv, wq, extra_w, *, prompt, tm=512):
    M = proj.shape[0]
    tm = min(tm, M)
    n_cs = cs_table.shape[0] // tm
    H = N_HEADS_A
    row = lambda i: (i, 0)
    const2 = lambda i: (0, 0)
    in_specs = [
        pl.BlockSpec((tm, SMALL_W), row),
        pl.BlockSpec((tm, 128), lambda i: (i % n_cs, 0)),
        pl.BlockSpec((1, Q_LORA), const2),
        pl.BlockSpec((1, KV_LORA), const2),
        pl.BlockSpec((Q_LORA, H * QK_HEAD), const2),
    ]
    out_specs = [pl.BlockSpec((tm, KV_LORA), row), pl.BlockSpec((tm, QK_ROPE), row)]
    out_shape = [jax.ShapeDtypeStruct((M, KV_LORA), F32), jax.ShapeDtypeStruct((M, QK_ROPE), F32)]
    if prompt:
        w_uk, w_uv = extra_w
        in_specs += [pl.BlockSpec((KV_LORA, H * QK_NOPE), const2), pl.BlockSpec((KV_LORA, H * V_DIM), const2)]
        out_specs += [pl.BlockSpec((tm, H * QK_HEAD), row), pl.BlockSpec((tm, H * QK_HEAD), row),
                      pl.BlockSpec((tm, H * V_DIM), row)]
        out_shape += [jax.ShapeDtypeStruct((M, H * QK_HEAD), BF16), jax.ShapeDtypeStruct((M, H * QK_HEAD), BF16),
                      jax.ShapeDtypeStruct((M, H * V_DIM), BF16)]
        body, extra = _prep_prompt_kernel, [w_uk, w_uv]
    else:
        (w_ukt,) = extra_w
        in_specs += [pl.BlockSpec((H, QK_NOPE, KV_LORA), lambda i: (0, 0, 0))]
        out_specs += [pl.BlockSpec((tm, H * KV_LORA), row), pl.BlockSpec((tm, H * 128), row)]
        out_shape += [jax.ShapeDtypeStruct((M, H * KV_LORA), BF16), jax.ShapeDtypeStruct((M, H * 128), BF16)]
        body, extra = _prep_sample_kernel, [w_ukt]
    return pl.pallas_call(
        body,
        grid=(M // tm,),
        in_specs=in_specs,
        out_specs=out_specs,
        out_shape=out_shape,
        compiler_params=_cparams("parallel"),
        name="prep_prompt" if prompt else "prep_sample",
    )(proj, cs_table, nq.reshape(1, -1), nkv.reshape(1, -1), wq, *extra)


def _attn_prompt_kernel(q_ref, k_ref, v_ref, o_ref, *, seq, tq):
    r_chunk = lax.broadcasted_iota(jnp.int32, (tq, tq), 0) // CHUNK
    c_chunk = lax.broadcasted_iota(jnp.int32, (tq, tq), 1) // CHUNK
    diag_mask = c_chunk <= r_chunk

    def q_body(qi, _):
        qs = pl.multiple_of(qi * tq, tq)
        q = q_ref[pl.ds(qs, tq), :]

        def step(kj, carry, masked):
            m, l, acc = carry
            ks = pl.multiple_of(kj * tq, tq)
            s = _dot_nt(q, k_ref[pl.ds(ks, tq), :])
            if masked:
                s = jnp.where(diag_mask, s, -jnp.inf)
            m_new = jnp.maximum(m, jnp.max(s, axis=-1, keepdims=True))
            p = jnp.exp(s - m_new)
            alpha = jnp.exp(m - m_new)
            l = alpha * l + jnp.sum(p, axis=-1, keepdims=True)
            acc = alpha * acc + _dot(p.astype(BF16), v_ref[pl.ds(ks, tq), :])
            return m_new, l, acc

        init = (jnp.full((tq, 1), -jnp.inf, F32), jnp.zeros((tq, 1), F32), jnp.zeros((tq, V_DIM), F32))
        carry = lax.fori_loop(0, qi, lambda kj, c: step(kj, c, False), init)
        _, l, acc = step(qi, carry, True)
        o_ref[pl.ds(qs, tq), :] = (acc / l).astype(o_ref.dtype)
        return 0

    lax.fori_loop(0, seq // tq, q_body, 0)


def _attn_prompt(q, k, v, *, tq=128):
    B, S, _ = q.shape
    H = N_HEADS_A
    return pl.pallas_call(
        functools.partial(_attn_prompt_kernel, seq=S, tq=tq),
        grid=(B, H),
        in_specs=[
            pl.BlockSpec((None, S, QK_HEAD), lambda b, h: (b, 0, h)),
            pl.BlockSpec((None, S, QK_HEAD), lambda b, h: (b, 0, h)),
            pl.BlockSpec((None, S, V_DIM), lambda b, h: (b, 0, h)),
        ],
        out_specs=pl.BlockSpec((None, S, V_DIM), lambda b, h: (b, 0, h)),
        out_shape=jax.ShapeDtypeStruct((B, S, H * V_DIM), BF16),
        compiler_params=_cparams("parallel", "parallel"),
        name="attn_prompt",
    )(q, k, v)


def _attn_sample_kernel(qlat_ref, qrot_ref, clat_ref, ckr_ref, nlat_ref, nkr_ref, o_ref):
    ql = qlat_ref[...]
    qr = qrot_ref[...][:, :QK_ROPE]
    lat = clat_ref[...].astype(BF16)
    nlat = nlat_ref[...].astype(BF16)
    s1 = _dot_nt(ql, lat) + _dot_nt(qr, ckr_ref[...].astype(BF16))
    s2 = _dot_nt(ql, nlat) + _dot_nt(qr, nkr_ref[...].astype(BF16))
    m = jnp.maximum(jnp.max(s1, axis=-1, keepdims=True), jnp.max(s2, axis=-1, keepdims=True))
    p1 = jnp.exp(s1 - m)
    p2 = jnp.exp(s2 - m)
    l = jnp.sum(p1, axis=-1, keepdims=True) + jnp.sum(p2, axis=-1, keepdims=True)
    o = _dot(p1.astype(BF16), lat) + _dot(p2.astype(BF16), nlat)
    o_ref[...] = (o / l).astype(o_ref.dtype)


def _attn_sample(qlat, qrot, cache_lat, cache_kr, new_lat, new_kr):
    Bs, R, _ = qlat.shape
    P = cache_lat.shape[1]
    T = new_lat.shape[1]
    b3 = lambda b: (b, 0, 0)
    return pl.pallas_call(
        _attn_sample_kernel,
        grid=(Bs,),
        in_specs=[
            pl.BlockSpec((None, R, KV_LORA), b3),
            pl.BlockSpec((None, R, 128), b3),
            pl.BlockSpec((None, P, KV_LORA), b3),
            pl.BlockSpec((None, P, QK_ROPE), b3),
            pl.BlockSpec((None, T, KV_LORA), b3),
            pl.BlockSpec((None, T, QK_ROPE), b3),
        ],
        out_specs=pl.BlockSpec((None, R, KV_LORA), b3),
        out_shape=jax.ShapeDtypeStruct((Bs, R, KV_LORA), BF16),
        compiler_params=_cparams("parallel"),
        name="attn_sample",
    )(qlat, qrot, cache_lat, cache_kr, new_lat, new_kr)


def _uv_kernel(x_ref, w_ref, o_ref):
    o_ref[...] = _dot(x_ref[...], w_ref[...]).astype(o_ref.dtype)


def _uv_proj(olat, w_uv_heads):
    M = olat.shape[0]
    H = N_HEADS_A
    return pl.pallas_call(
        _uv_kernel,
        grid=(H,),
        in_specs=[pl.BlockSpec((M, KV_LORA), lambda h: (0, h)),
                  pl.BlockSpec((None, KV_LORA, V_DIM), lambda h: (h, 0, 0))],
        out_specs=pl.BlockSpec((M, V_DIM), lambda h: (0, h)),
        out_shape=jax.ShapeDtypeStruct((M, H * V_DIM), BF16),
        compiler_params=_cparams("parallel"),
        name="uv_proj",
    )(olat, w_uv_heads)


HIST = 8


def _ssd_kernel(x_ref, b_ref, c_ref, z_ref, dtc_ref, dtr_ref, cx_ref, cb_ref, cc_ref,
                wx_ref, wb_ref, wc_ref, bx_ref, bb_ref, bc_ref,
                dtbc_ref, dtbr_ref, alc_ref, alr_ref, dsk_ref, nssm_ref, s0_ref,
                y_ref, sout_ref, xbuf, bbuf, cbuf, s_scr, *, q_len, n_chunks):
    c = pl.program_id(2)
    Q = q_len
    P = HEAD_DIM_S
    h0 = HIST - (CONV_W - 1)

    @pl.when(c == 0)
    def _():
        xbuf[h0:HIST, :] = cx_ref[...]
        bbuf[h0:HIST, :] = cb_ref[...]
        cbuf[h0:HIST, :] = cc_ref[...]
        s_scr[...] = s0_ref[...]

    def conv(buf, cur_ref, w_ref, bias_ref):
        buf[HIST:HIST + Q, :] = cur_ref[...]
        w = w_ref[...]
        acc = bias_ref[...] + w[0:1, :] * buf[h0:h0 + Q, :]
        for k in range(1, CONV_W):
            acc = acc + w[k:k + 1, :] * buf[h0 + k:h0 + k + Q, :]
        buf[h0:HIST, :] = buf[Q + h0:Q + HIST, :]
        return _silu(acc)

    xs = conv(xbuf, x_ref, wx_ref, bx_ref)
    bm = conv(bbuf, b_ref, wb_ref, bb_ref).astype(BF16)
    cm = conv(cbuf, c_ref, wc_ref, bc_ref).astype(BF16)

    if n_chunks == 1:
        dt_row_raw = dtr_ref[...]
    else:
        dt_row_raw = dtr_ref[:, pl.ds(pl.multiple_of(c * Q, Q), Q)]
    dt_col = jax.nn.softplus(dtc_ref[...] + dtbc_ref[...])
    dt_row = jax.nn.softplus(dt_row_raw + dtbr_ref[...])
    a_col = dt_col * (-jnp.exp(alc_ref[...]))
    a_row = dt_row * (-jnp.exp(alr_ref[...]))
    ri = lax.broadcasted_iota(jnp.int32, (Q, Q), 0)
    ci = lax.broadcasted_iota(jnp.int32, (Q, Q), 1)
    causal = ci <= ri
    tril = causal.astype(F32)
    triu = (ri <= ci).astype(F32)
    acs_col = jnp.dot(tril, a_col, precision=lax.Precision.HIGHEST, preferred_element_type=F32)
    acs_row = jnp.dot(a_row, triu, precision=lax.Precision.HIGHEST, preferred_element_type=F32)

    cb = _dot_nt(cm, bm)
    y_off = _dot_nt(cm, s_scr[...].astype(BF16))
    dsk = dsk_ref[...]
    ys = []
    for r in range(HEADS_PER_GROUP):
        ac = acs_col[:, r:r + 1]
        ar = acs_row[r:r + 1, :]
        decay = jnp.exp(jnp.where(causal, ac - ar, -jnp.inf))
        xr = xs[:, r * P:(r + 1) * P]
        xdt = xr * dt_col[:, r:r + 1]
        y = _dot((cb * decay).astype(BF16), xdt.astype(BF16))
        y = y + y_off[:, r * P:(r + 1) * P] * jnp.exp(ac)
        ys.append(y + dsk[:, r:r + 1] * xr)
        a_end = ar[:, Q - 1:Q]
        xw = xdt * jnp.exp(a_end - ac)
        st = _dot_tn(xw.astype(BF16), bm)
        s_scr[r * P:(r + 1) * P, :] = jnp.exp(a_end) * s_scr[r * P:(r + 1) * P, :] + st
    y = jnp.concatenate(ys, axis=1)

    v = y * _silu(z_ref[...])
    y_ref[...] = _rms(v, nssm_ref[...]).astype(y_ref.dtype)

    @pl.when(c == n_chunks - 1)
    def _():
        sout_ref[...] = s_scr[...]


def _ssd(proj3, dt_col, dt_row, conv_state, conv_w, conv_b, dt_bias, a_log, d_skip, n_ssm, state0, *, q_len):
    B, L, _ = proj3.shape
    G, R, N = N_GROUPS_S, HEADS_PER_GROUP, D_STATE
    Q = q_len
    nc = L // Q
    xblk = COL_X // GROUP_W
    zblk = COL_Z // GROUP_W
    bblk = COL_B // N
    cblk = COL_C // N
    xw0, bw0, cw0 = 0, D_INNER // N, D_INNER // N + G
    grp = lambda b, g, c: (g, 0, 0)
    in_specs = [
        pl.BlockSpec((None, Q, GROUP_W), lambda b, g, c: (b, c, xblk + g)),
        pl.BlockSpec((None, Q, N), lambda b, g, c: (b, c, bblk + g)),
        pl.BlockSpec((None, Q, N), lambda b, g, c: (b, c, cblk + g)),
        pl.BlockSpec((None, Q, GROUP_W), lambda b, g, c: (b, c, zblk + g)),
        pl.BlockSpec((None, None, Q, R), lambda b, g, c: (b, g, c, 0)),
        pl.BlockSpec((None, None, R, L), lambda b, g, c: (b, g, 0, 0)),
        pl.BlockSpec((None, CONV_W - 1, GROUP_W), lambda b, g, c: (b, 0, g)),
        pl.BlockSpec((None, CONV_W - 1, N), lambda b, g, c: (b, 0, bw0 + g)),
        pl.BlockSpec((None, CONV_W - 1, N), lambda b, g, c: (b, 0, cw0 + g)),
        pl.BlockSpec((CONV_W, GROUP_W), lambda b, g, c: (0, g)),
        pl.BlockSpec((CONV_W, N), lambda b, g, c: (0, bw0 + g)),
        pl.BlockSpec((CONV_W, N), lambda b, g, c: (0, cw0 + g)),
        pl.BlockSpec((1, GROUP_W), lambda b, g, c: (0, g)),
        pl.BlockSpec((1, N), lambda b, g, c: (0, bw0 + g)),
        pl.BlockSpec((1, N), lambda b, g, c: (0, cw0 + g)),
        pl.BlockSpec((None, 1, R), grp),
        pl.BlockSpec((None, R, 1), grp),
        pl.BlockSpec((None, 1, R), grp),
        pl.BlockSpec((None, R, 1), grp),
        pl.BlockSpec((None, 1, R), grp),
        pl.BlockSpec((1, GROUP_W), lambda b, g, c: (0, g)),
        pl.BlockSpec((None, None, GROUP_W, N), lambda b, g, c: (b, g, 0, 0)),
    ]
    cb2 = conv_b.reshape(1, CONV_DIM)
    args = [proj3, proj3, proj3, proj3, dt_col, dt_row, conv_state, conv_state, conv_state,
            conv_w, conv_w, conv_w, cb2, cb2, cb2,
            dt_bias.reshape(G, 1, R), dt_bias.reshape(G, R, 1), a_log.reshape(G, 1, R), a_log.reshape(G, R, 1),
            d_skip.reshape(G, 1, R), n_ssm.reshape(1, D_INNER), state0.reshape(B, G, GROUP_W, N)]
    y, s_out = pl.pallas_call(
        functools.partial(_ssd_kernel, q_len=Q, n_chunks=nc),
        grid=(B, G, nc),
        in_specs=in_specs,
        out_specs=[pl.BlockSpec((None, Q, GROUP_W), lambda b, g, c: (b, c, g)),
                   pl.BlockSpec((None, None, GROUP_W, N), lambda b, g, c: (b, g, 0, 0))],
        out_shape=[jax.ShapeDtypeStruct((B, L, D_INNER), BF16),
                   jax.ShapeDtypeStruct((B, G, GROUP_W, N), F32)],
        scratch_shapes=[pltpu.VMEM((Q + HIST, GROUP_W), F32), pltpu.VMEM((Q + HIST, N), F32),
                        pltpu.VMEM((Q + HIST, N), F32), pltpu.VMEM((GROUP_W, N), F32)],
        compiler_params=_cparams("parallel", "parallel", "arbitrary"),
        name="ssd",
    )(*args)
    return y, s_out.reshape(B, N_HEADS_S, HEAD_DIM_S, N)


def _merge_kernel(oa_ref, ys_ref, wa_ref, ws_ref, g0_ref, g1_ref, bg_ref, o_ref):
    attn = _dot(oa_ref[...], wa_ref[...])
    ssm = _dot(ys_ref[...], ws_ref[...])
    bg = bg_ref[...]
    g0 = jax.nn.sigmoid(g0_ref[...] + bg[0:1, :])
    g1 = jax.nn.sigmoid(g1_ref[...] + bg[1:2, :])
    o_ref[...] = (g0 * attn + g1 * ssm).astype(o_ref.dtype)


def _merge(o_attn, y_ssm, w_o_attn, w_o_ssm, proj, b_gate, *, tm=512, tn=512):
    M = o_attn.shape[0]
    tm = min(tm, M)
    g0blk = COL_GATE // tn
    g1blk = (COL_GATE + D_MODEL) // tn
    return pl.pallas_call(
        _merge_kernel,
        grid=(M // tm, D_MODEL // tn),
        in_specs=[
            pl.BlockSpec((tm, o_attn.shape[1]), lambda i, j: (i, 0)),
            pl.BlockSpec((tm, D_INNER), lambda i, j: (i, 0)),
            pl.BlockSpec((w_o_attn.shape[0], tn), lambda i, j: (0, j)),
            pl.BlockSpec((D_INNER, tn), lambda i, j: (0, j)),
            pl.BlockSpec((tm, tn), lambda i, j: (i, g0blk + j)),
            pl.BlockSpec((tm, tn), lambda i, j: (i, g1blk + j)),
            pl.BlockSpec((2, tn), lambda i, j: (0, j)),
        ],
        out_specs=pl.BlockSpec((tm, tn), lambda i, j: (i, j)),
        out_shape=jax.ShapeDtypeStruct((M, D_MODEL), BF16),
        compiler_params=_cparams("parallel", "arbitrary"),
        name="merge",
    )(o_attn, y_ssm, w_o_attn, w_o_ssm, proj, proj, b_gate)


def _outproj_kernel(m_ref, w_ref, x_ref, o_ref):
    o_ref[...] = x_ref[...] + _dot(m_ref[...], w_ref[...])


def _outproj(merged, w_out, x, *, tm=512, tn=512):
    M = x.shape[0]
    tm = min(tm, M)
    return pl.pallas_call(
        _outproj_kernel,
        grid=(M // tm, D_MODEL // tn),
        in_specs=[
            pl.BlockSpec((tm, D_MODEL), lambda i, j: (i, 0)),
            pl.BlockSpec((D_MODEL, tn), lambda i, j: (0, j)),
            pl.BlockSpec((tm, tn), lambda i, j: (i, j)),
        ],
        out_specs=pl.BlockSpec((tm, tn), lambda i, j: (i, j)),
        out_shape=jax.ShapeDtypeStruct((M, D_MODEL), F32),
        compiler_params=_cparams("parallel", "arbitrary"),
        name="outproj",
    )(merged, w_out, x)


def _swap_halves(w):
    half = QK_ROPE // 2
    return jnp.concatenate([w[..., half:], w[..., :half]], axis=-1)


def _rope_table(pos):
    half = QK_ROPE // 2
    inv = jnp.power(ROPE_THETA, -jnp.arange(half, dtype=F32) / half)
    ang = pos.astype(F32)[:, None] * inv[None, :]
    cos, sin = jnp.cos(ang), jnp.sin(ang)
    return jnp.concatenate([cos, cos, -sin, sin], axis=-1)


def _dt_layouts(proj3):
    B, L, _ = proj3.shape
    dt = proj3[:, :, COL_DT:COL_DT + N_HEADS_S].reshape(B, L, N_GROUPS_S, HEADS_PER_GROUP)
    return dt.transpose(0, 2, 1, 3), dt.transpose(0, 2, 3, 1)


def kernel(x_prompt, x_sample, cache_kv_latent, cache_k_rope, state_ssm, state_conv, norm_ffn1, w_ffn1_gate, w_ffn1_up, w_ffn1_down, norm_mix, w_in, b_gate, norm_q_a, w_q_b, norm_kv_a, w_kv_b, conv_w, conv_b, dt_bias, a_log, d_skip, norm_ssm, w_o_attn, w_o_ssm, w_out, norm_ffn2, w_ffn2_gate, w_ffn2_up, w_ffn2_down, norm_final):
    Bp, S, D = x_prompt.shape
    Bs, T, _ = x_sample.shape
    past = cache_kv_latent.shape[2]
    H = N_HEADS_A
    l = 0

    wi = w_in[l]
    o_kva = Q_LORA
    o_z = o_kva + KV_LORA + QK_ROPE
    o_xbc = o_z + D_INNER
    o_dt = o_xbc + CONV_DIM
    o_gate = o_dt + N_HEADS_S
    w_kr = wi[:, o_kva + KV_LORA:o_z]
    w_proj = jnp.concatenate(
        [wi[:, :o_kva + KV_LORA], w_kr, _swap_halves(w_kr), wi[:, o_dt:o_gate],
         jnp.zeros((D, 128 - N_HEADS_S), wi.dtype), wi[:, o_z:o_xbc], wi[:, o_xbc:o_dt], wi[:, o_gate:]],
        axis=1).astype(BF16)
    wq3 = w_q_b[l].reshape(Q_LORA, H, QK_NOPE + QK_ROPE)
    wq = jnp.concatenate([wq3, _swap_halves(wq3[..., QK_NOPE:])], axis=-1).reshape(Q_LORA, H * QK_HEAD).astype(BF16)
    wkv3 = w_kv_b[l].reshape(KV_LORA, H, QK_NOPE + V_DIM)
    w_uk = wkv3[..., :QK_NOPE]
    w_uv = wkv3[..., QK_NOPE:]
    w_uk_flat = w_uk.reshape(KV_LORA, H * QK_NOPE).astype(BF16)
    w_uv_flat = w_uv.reshape(KV_LORA, H * V_DIM).astype(BF16)
    w_uk_t = w_uk.transpose(1, 2, 0).astype(BF16)
    w_uv_h = w_uv.transpose(1, 0, 2).astype(BF16)
    wg1, wu1, wd1 = w_ffn1_gate[l].astype(BF16), w_ffn1_up[l].astype(BF16), w_ffn1_down[l].astype(BF16)
    wg2, wu2, wd2 = w_ffn2_gate[l].astype(BF16), w_ffn2_up[l].astype(BF16), w_ffn2_down[l].astype(BF16)
    woa, wos, wo = w_o_attn[l].astype(BF16), w_o_ssm[l].astype(BF16), w_out[l].astype(BF16)

    def mixer_tail(x1, proj, o_attn, y_ssm):
        merged = _merge(o_attn, y_ssm, woa, wos, proj, b_gate[l])
        x2 = _outproj(merged, wo, x1)
        return _ffn(x2, norm_ffn2[l], wg2, wu2, wd2, norm_final)

    xp = x_prompt.reshape(Bp * S, D)
    x1 = _ffn(xp, norm_ffn1[l], wg1, wu1, wd1)
    proj = _inproj(x1, norm_mix[l], w_proj)
    cs_p = _rope_table(jnp.arange(S))
    p_lat, p_kr, q, k, v = _prep(proj, cs_p, norm_q_a[l], norm_kv_a[l], wq, (w_uk_flat, w_uv_flat), prompt=True)
    o_attn = _attn_prompt(q.reshape(Bp, S, -1), k.reshape(Bp, S, -1), v.reshape(Bp, S, -1)).reshape(Bp * S, -1)
    proj3 = proj.reshape(Bp, S, PROJ_W)
    dt_c, dt_r = _dt_layouts(proj3)
    y_ssm, p_ssm = _ssd(proj3, dt_c, dt_r, jnp.zeros((Bp, CONV_W - 1, CONV_DIM), F32), conv_w[l], conv_b[l],
                        dt_bias[l], a_log[l], d_skip[l], norm_ssm[l],
                        jnp.zeros((Bp, N_HEADS_S, HEAD_DIM_S, D_STATE), F32), q_len=128)
    y_prompt = mixer_tail(x1, proj, o_attn, y_ssm.reshape(Bp * S, D_INNER)).reshape(Bp, S, D)
    p_conv = proj3[:, S - (CONV_W - 1):, COL_X:COL_X + CONV_DIM]

    xs_ = x_sample.reshape(Bs * T, D)
    x1s = _ffn(xs_, norm_ffn1[l], wg1, wu1, wd1)
    projs = _inproj(x1s, norm_mix[l], w_proj)
    cs_s = jnp.tile(_rope_table(past + jnp.arange(T)), (Bs, 1))
    s_lat, s_kr, qlat, qrot = _prep(projs, cs_s, norm_q_a[l], norm_kv_a[l], wq, (w_uk_t,), prompt=False)
    o_lat = _attn_sample(qlat.reshape(Bs, T * H, KV_LORA), qrot.reshape(Bs, T * H, 128),
                         cache_kv_latent[l], cache_k_rope[l],
                         s_lat.reshape(Bs, T, KV_LORA), s_kr.reshape(Bs, T, QK_ROPE))
    o_attn_s = _uv_proj(o_lat.reshape(Bs * T, H * KV_LORA), w_uv_h)
    projs3 = projs.reshape(Bs, T, PROJ_W)
    dt_cs, dt_rs = _dt_layouts(projs3)
    y_ssm_s, s_ssm = _ssd(projs3, dt_cs, dt_rs, state_conv[l], conv_w[l], conv_b[l],
                          dt_bias[l], a_log[l], d_skip[l], norm_ssm[l], state_ssm[l], q_len=T)
    y_sample = mixer_tail(x1s, projs, o_attn_s, y_ssm_s.reshape(Bs * T, D_INNER)).reshape(Bs, T, D)
    s_conv = projs3[:, T - (CONV_W - 1):, COL_X:COL_X + CONV_DIM]

    return (y_prompt, y_sample,
            p_lat.reshape(1, Bp, S, KV_LORA), p_kr.reshape(1, Bp, S, QK_ROPE), p_ssm[None], p_conv[None],
            s_lat.reshape(1, Bs, T, KV_LORA), s_kr.reshape(1, Bs, T, QK_ROPE), s_ssm[None], s_conv[None])
```

```python
import functools
import math

import jax
import jax.numpy as jnp
from jax import lax
from jax.experimental import pallas as pl
from jax.experimental.pallas import tpu as pltpu

F32 = jnp.float32
BF16 = jnp.bfloat16

D_MODEL = 2048
CHUNK = 64
N_HEADS_A = 16
Q_LORA = 768
KV_LORA = 512
QK_NOPE = 128
QK_ROPE = 64
V_DIM = 128
ROPE_THETA = 10000.0
ATTN_SCALE = (QK_NOPE + QK_ROPE) ** -0.5
D_INNER = 2 * D_MODEL
HEAD_DIM_S = 64
N_HEADS_S = D_INNER // HEAD_DIM_S
N_GROUPS_S = 8
HEADS_PER_GROUP = N_HEADS_S // N_GROUPS_S
GROUP_W = D_INNER // N_GROUPS_S
D_STATE = 128
CONV_W = 4
CONV_DIM = D_INNER + 2 * N_GROUPS_S * D_STATE
D_FF = 5632
EPS = 1e-6

QK_HEAD = 256
COL_QA = 0
COL_LAT = Q_LORA
COL_KR = COL_LAT + KV_LORA
COL_DT = COL_KR + 2 * QK_ROPE
SMALL_W = COL_DT + 128
BC_W = N_GROUPS_S * D_STATE
COL_X = 0
COL_Z = D_INNER
COL_GATE = 2 * D_INNER
COL_B = COL_GATE + 2 * D_MODEL
COL_C = COL_B + BC_W
BIG_W = COL_C + BC_W
LOG2E = math.log2(math.e)

VMEM_LIMIT = 56 * 1024 * 1024


def _cparams(*sem):
    return pltpu.CompilerParams(dimension_semantics=sem, vmem_limit_bytes=VMEM_LIMIT)


def _rms(x, g):
    return x * lax.rsqrt(jnp.mean(x * x, axis=-1, keepdims=True) + EPS) * g


def _silu(x):
    return x * jax.nn.sigmoid(x)


def _dot(a, b):
    return jnp.dot(a, b, preferred_element_type=F32)


def _dot_nt(a, b):
    return lax.dot_general(a, b, (((1,), (1,)), ((), ())), preferred_element_type=F32)


def _dot_tn(a, b):
    return lax.dot_general(a, b, (((0,), (0,)), ((), ())), preferred_element_type=F32)


def _ffn_kernel(*refs, final_norm):
    if final_norm:
        x_ref, g_ref, wg_ref, wu_ref, wd_ref, gf_ref, o_ref, h_scr, acc_scr = refs
    else:
        x_ref, g_ref, wg_ref, wu_ref, wd_ref, o_ref, h_scr, acc_scr = refs
    j = pl.program_id(1)

    @pl.when(j == 0)
    def _():
        h_scr[...] = _rms(x_ref[...], g_ref[...]).astype(BF16)
        acc_scr[...] = jnp.zeros_like(acc_scr)

    h = h_scr[...]
    a = _silu(_dot(h, wg_ref[...])) * _dot(h, wu_ref[...])
    acc_scr[...] += _dot(a.astype(BF16), wd_ref[...])

    @pl.when(j == pl.num_programs(1) - 1)
    def _():
        y = x_ref[...] + 0.5 * acc_scr[...]
        if final_norm:
            y = _rms(y, gf_ref[...])
        o_ref[...] = y


def _ffn(x, g, wg, wu, wd, gf=None, *, tm=512, tf=512):
    M, D = x.shape
    F = wg.shape[1]
    tm = min(tm, M)
    in_specs = [
        pl.BlockSpec((tm, D), lambda i, j: (i, 0)),
        pl.BlockSpec((1, D), lambda i, j: (0, 0)),
        pl.BlockSpec((D, tf), lambda i, j: (0, j)),
        pl.BlockSpec((D, tf), lambda i, j: (0, j)),
        pl.BlockSpec((tf, D), lambda i, j: (j, 0)),
    ]
    args = [x, g.reshape(1, D), wg, wu, wd]
    if gf is not None:
        in_specs.append(pl.BlockSpec((1, D), lambda i, j: (0, 0)))
        args.append(gf.reshape(1, D))
    return pl.pallas_call(
        functools.partial(_ffn_kernel, final_norm=gf is not None),
        grid=(M // tm, F // tf),
        in_specs=in_specs,
        out_specs=pl.BlockSpec((tm, D), lambda i, j: (i, 0)),
        out_shape=jax.ShapeDtypeStruct((M, D), F32),
        scratch_shapes=[pltpu.VMEM((tm, D), BF16), pltpu.VMEM((tm, D), F32)],
        compiler_params=_cparams("parallel", "arbitrary"),
        name="ffn",
    )(*args)


def _inproj_kernel(x_ref, g_ref, w_ref, o_ref, h_scr):
    @pl.when(pl.program_id(1) == 0)
    def _():
        h_scr[...] = _rms(x_ref[...], g_ref[...]).astype(BF16)

    o_ref[...] = _dot(h_scr[...], w_ref[...])


def _inproj(x, g, w, *, tm, tn):
    M, D = x.shape
    N = w.shape[1]
    tm = min(tm, M)
    return pl.pallas_call(
        _inproj_kernel,
        grid=(M // tm, N // tn),
        in_specs=[
            pl.BlockSpec((tm, D), lambda i, j: (i, 0)),
            pl.BlockSpec((1, D), lambda i, j: (0, 0)),
            pl.BlockSpec((D, tn), lambda i, j: (0, j)),
        ],
        out_specs=pl.BlockSpec((tm, tn), lambda i, j: (i, j)),
        out_shape=jax.ShapeDtypeStruct((M, N), F32),
        scratch_shapes=[pltpu.VMEM((tm, D), BF16)],
        compiler_params=_cparams("parallel", "arbitrary"),
        name="inproj",
    )(x, g.reshape(1, D), w)


def _rope_pair(blk, cs, keep):
    prod = blk * cs
    return jnp.where(keep, prod + pltpu.roll(prod, 64, axis=1), 0.0)


def _prep_common(small_ref, cs_ref, nq_ref, nkv_ref, wq_ref, ckv_ref, kr_ref):
    sm = small_ref[...]
    cs = cs_ref[...]
    keep = lax.broadcasted_iota(jnp.int32, cs.shape, 1) < QK_ROPE
    cq = _rms(sm[:, COL_QA:COL_QA + Q_LORA], nq_ref[...]).astype(BF16)
    q = _dot(cq, wq_ref[...]) * (ATTN_SCALE * LOG2E)
    ckv = _rms(sm[:, COL_LAT:COL_LAT + KV_LORA], nkv_ref[...])
    ckv_ref[...] = ckv
    krot = _rope_pair(sm[:, COL_KR:COL_KR + 128], cs, keep)
    kr_ref[...] = krot[:, :QK_ROPE]
    return q, ckv, krot, cs, keep


def _prep_prompt_kernel(small_ref, cs_ref, nq_ref, nkv_ref, wq_ref, wuk_ref, wuv_ref,
                        ckv_ref, kr_ref, q_ref, k_ref, v_ref):
    q, ckv, krot, cs, keep = _prep_common(small_ref, cs_ref, nq_ref, nkv_ref, wq_ref, ckv_ref, kr_ref)
    ckv_b = ckv.astype(BF16)
    kn = _dot(ckv_b, wuk_ref[...])
    krot_b = krot.astype(BF16)
    for h in range(N_HEADS_A):
        c0 = h * QK_HEAD
        q_ref[:, c0:c0 + 128] = q[:, c0:c0 + 128].astype(BF16)
        q_ref[:, c0 + 128:c0 + 256] = _rope_pair(q[:, c0 + 128:c0 + 256], cs, keep).astype(BF16)
        k_ref[:, c0:c0 + 128] = kn[:, h * 128:(h + 1) * 128].astype(BF16)
        k_ref[:, c0 + 128:c0 + 256] = krot_b
    v_ref[...] = _dot(ckv_b, wuv_ref[...]).astype(BF16)


def _prep_sample_kernel(small_ref, cs_ref, nq_ref, nkv_ref, wq_ref, wukt_ref,
                        ckv_ref, kr_ref, qlat_ref, qrot_ref):
    q, _, _, cs, keep = _prep_common(small_ref, cs_ref, nq_ref, nkv_ref, wq_ref, ckv_ref, kr_ref)
    for h in range(N_HEADS_A):
        c0 = h * QK_HEAD
        qn = q[:, c0:c0 + 128].astype(BF16)
        qlat_ref[:, h * KV_LORA:(h + 1) * KV_LORA] = _dot(qn, wukt_ref[h]).astype(BF16)
        qrot_ref[:, h * 128:(h + 1) * 128] = _rope_pair(q[:, c0 + 128:c0 + 256], cs, keep).astype(BF16)


def _prep(proj, cs_table, nq, nkv, wq, extra_w, *, prompt, tm=512):
    M = proj.shape[0]
    tm = min(tm, M)
    n_cs = cs_table.shape[0] // tm
    H = N_HEADS_A
    row = lambda i: (i, 0)
    const2 = lambda i: (0, 0)
    in_specs = [
        pl.BlockSpec((tm, SMALL_W), row),
        pl.BlockSpec((tm, 128), lambda i: (i % n_cs, 0)),
        pl.BlockSpec((1, Q_LORA), const2),
        pl.BlockSpec((1, KV_LORA), const2),
        pl.BlockSpec((Q_LORA, H * QK_HEAD), const2),
    ]
    out_specs = [pl.BlockSpec((tm, KV_LORA), row), pl.BlockSpec((tm, QK_ROPE), row)]
    out_shape = [jax.ShapeDtypeStruct((M, KV_LORA), F32), jax.ShapeDtypeStruct((M, QK_ROPE), F32)]
    if prompt:
        w_uk, w_uv = extra_w
        in_specs += [pl.BlockSpec((KV_LORA, H * QK_NOPE), const2), pl.BlockSpec((KV_LORA, H * V_DIM), const2)]
        out_specs += [pl.BlockSpec((tm, H * QK_HEAD), row), pl.BlockSpec((tm, H * QK_HEAD), row),
                      pl.BlockSpec((tm, H * V_DIM), row)]
        out_shape += [jax.ShapeDtypeStruct((M, H * QK_HEAD), BF16), jax.ShapeDtypeStruct((M, H * QK_HEAD), BF16),
                      jax.ShapeDtypeStruct((M, H * V_DIM), BF16)]
        body, extra = _prep_prompt_kernel, [w_uk, w_uv]
    else:
        (w_ukt,) = extra_w
        in_specs += [pl.BlockSpec((H, QK_NOPE, KV_LORA), lambda i: (0, 0, 0))]
        out_specs += [pl.BlockSpec((tm, H * KV_LORA), row), pl.BlockSpec((tm, H * 128), row)]
        out_shape += [jax.ShapeDtypeStruct((M, H * KV_LORA), BF16), jax.ShapeDtypeStruct((M, H * 128), BF16)]
        body, extra = _prep_sample_kernel, [w_ukt]
    return pl.pallas_call(
        body,
        grid=(M // tm,),
        in_specs=in_specs,
        out_specs=out_specs,
        out_shape=out_shape,
        compiler_params=_cparams("parallel"),
        name="prep_prompt" if prompt else "prep_sample",
    )(proj, cs_table, nq.reshape(1, -1), nkv.reshape(1, -1), wq, *extra)


def _attn_prompt_kernel(q_ref, k_ref, v_ref, bias_ref, o_ref, s_scr, p_scr, *, seq, tq):
    for qi in range(seq // tq):
        kv = (qi + 1) * tq
        s_buf = s_scr.at[qi % 2]
        p_buf = p_scr.at[qi % 2]
        q = q_ref[qi * tq:kv, :]
        s_buf[:, :kv] = _dot_nt(q, k_ref[:kv, :])
        s_buf[:, kv - tq:kv] = s_buf[:, kv - tq:kv] + bias_ref[...]
        m = jnp.max(s_buf[:, :kv], axis=-1, keepdims=True)
        p = jnp.exp2(s_buf[:, :kv] - m)
        l = jnp.sum(p, axis=-1, keepdims=True)
        p_buf[:, :kv] = p.astype(BF16)
        acc = _dot(p_buf[:, :kv], v_ref[:kv, :])
        o_ref[qi * tq:kv, :] = (acc / l).astype(o_ref.dtype)


def _attn_prompt(q, k, v, *, tq=512):
    B, S, _ = q.shape
    H = N_HEADS_A
    r_chunk = lax.broadcasted_iota(jnp.int32, (tq, tq), 0) // CHUNK
    c_chunk = lax.broadcasted_iota(jnp.int32, (tq, tq), 1) // CHUNK
    bias = jnp.where(c_chunk <= r_chunk, 0.0, -jnp.inf).astype(F32)
    return pl.pallas_call(
        functools.partial(_attn_prompt_kernel, seq=S, tq=tq),
        grid=(B, H),
        in_specs=[
            pl.BlockSpec((None, S, QK_HEAD), lambda b, h: (b, 0, h)),
            pl.BlockSpec((None, S, QK_HEAD), lambda b, h: (b, 0, h)),
            pl.BlockSpec((None, S, V_DIM), lambda b, h: (b, 0, h)),
            pl.BlockSpec((tq, tq), lambda b, h: (0, 0)),
        ],
        out_specs=pl.BlockSpec((None, S, V_DIM), lambda b, h: (b, 0, h)),
        out_shape=jax.ShapeDtypeStruct((B, S, H * V_DIM), BF16),
        scratch_shapes=[pltpu.VMEM((2, tq, S), F32), pltpu.VMEM((2, tq, S), BF16)],
        compiler_params=_cparams("parallel", "parallel"),
        name="attn_prompt",
    )(q, k, v, bias)


def _attn_sample_kernel(qlat_ref, qrot_ref, clat_ref, ckr_ref, nlat_ref, nkr_ref, o_ref):
    ql = qlat_ref[...]
    qr = qrot_ref[...][:, :QK_ROPE]
    lat = clat_ref[...].astype(BF16)
    nlat = nlat_ref[...].astype(BF16)
    s1 = _dot_nt(ql, lat) + _dot_nt(qr, ckr_ref[...].astype(BF16))
    s2 = _dot_nt(ql, nlat) + _dot_nt(qr, nkr_ref[...].astype(BF16))
    m = jnp.maximum(jnp.max(s1, axis=-1, keepdims=True), jnp.max(s2, axis=-1, keepdims=True))
    p1 = jnp.exp2(s1 - m)
    p2 = jnp.exp2(s2 - m)
    l = jnp.sum(p1, axis=-1, keepdims=True) + jnp.sum(p2, axis=-1, keepdims=True)
    o = _dot(p1.astype(BF16), lat) + _dot(p2.astype(BF16), nlat)
    o_ref[...] = (o / l).astype(o_ref.dtype)


def _attn_sample(qlat, qrot, cache_lat, cache_kr, new_lat, new_kr):
    Bs, R, _ = qlat.shape
    P = cache_lat.shape[1]
    T = new_lat.shape[1]
    b3 = lambda b: (b, 0, 0)
    return pl.pallas_call(
        _attn_sample_kernel,
        grid=(Bs,),
        in_specs=[
            pl.BlockSpec((None, R, KV_LORA), b3),
            pl.BlockSpec((None, R, 128), b3),
            pl.BlockSpec((None, P, KV_LORA), b3),
            pl.BlockSpec((None, P, QK_ROPE), b3),
            pl.BlockSpec((None, T, KV_LORA), b3),
            pl.BlockSpec((None, T, QK_ROPE), b3),
        ],
        out_specs=pl.BlockSpec((None, R, KV_LORA), b3),
        out_shape=jax.ShapeDtypeStruct((Bs, R, KV_LORA), BF16),
        compiler_params=_cparams("parallel"),
        name="attn_sample",
    )(qlat, qrot, cache_lat, cache_kr, new_lat, new_kr)


def _uv_kernel(x_ref, w_ref, o_ref):
    o_ref[...] = _dot(x_ref[...], w_ref[...]).astype(o_ref.dtype)


def _uv_proj(olat, w_uv_heads):
    M = olat.shape[0]
    H = N_HEADS_A
    return pl.pallas_call(
        _uv_kernel,
        grid=(H,),
        in_specs=[pl.BlockSpec((M, KV_LORA), lambda h: (0, h)),
                  pl.BlockSpec((None, KV_LORA, V_DIM), lambda h: (h, 0, 0))],
        out_specs=pl.BlockSpec((M, V_DIM), lambda h: (0, h)),
        out_shape=jax.ShapeDtypeStruct((M, H * V_DIM), BF16),
        compiler_params=_cparams("parallel"),
        name="uv_proj",
    )(olat, w_uv_heads)


HIST = 8


def _split3(v):
    hi = v.astype(BF16)
    r1 = v - hi.astype(F32)
    mid = r1.astype(BF16)
    lo = (r1 - mid.astype(F32)).astype(BF16)
    return jnp.concatenate([hi, mid, lo], axis=1)


def _ssd_kernel(x_ref, z_ref, b_ref, c_ref, dt_ref, cst_ref, cw_ref, cbias_ref, dtb_ref, alog_ref,
                dskx_ref, nssm_ref, e64_ref, e128_ref, s0_ref,
                y_ref, sout_ref, xbuf, bbuf, cbuf, xs_scr, bm_scr, cm_scr, s_scr, *, q_len, n_chunks):
    c = pl.program_id(1)
    Q = q_len
    P = HEAD_DIM_S
    G, R, N, W = N_GROUPS_S, HEADS_PER_GROUP, D_STATE, GROUP_W
    h0 = HIST - (CONV_W - 1)

    @pl.when(c == 0)
    def _():
        xbuf[h0:HIST, :] = cst_ref[:, :D_INNER]
        bbuf[h0:HIST, :] = cst_ref[:, D_INNER:D_INNER + BC_W]
        cbuf[h0:HIST, :] = cst_ref[:, D_INNER + BC_W:]
        s_scr[...] = s0_ref[...]

    def conv(buf, cur_ref, lo, hi):
        buf[HIST:HIST + Q, :] = cur_ref[...]
        acc = cbias_ref[:, lo:hi] + cw_ref[0:1, lo:hi] * buf[h0:h0 + Q, :]
        for k in range(1, CONV_W):
            acc = acc + cw_ref[k:k + 1, lo:hi] * buf[h0 + k:h0 + k + Q, :]
        buf[h0:HIST, :] = buf[Q + h0:Q + HIST, :]
        return _silu(acc)

    xs_scr[...] = conv(xbuf, x_ref, 0, D_INNER)
    bm_scr[...] = conv(bbuf, b_ref, D_INNER, D_INNER + BC_W).astype(BF16)
    cm_scr[...] = conv(cbuf, c_ref, D_INNER + BC_W, CONV_DIM).astype(BF16)

    dt_all = jax.nn.softplus(dt_ref[...] + dtb_ref[...])
    a_all = dt_all * (-jnp.exp(alog_ref[...]))
    ri = lax.broadcasted_iota(jnp.int32, (Q, Q), 0)
    ci = lax.broadcasted_iota(jnp.int32, (Q, Q), 1)
    causal = ci <= ri
    hp = lax.Precision.HIGHEST
    acs_all = jnp.dot(causal.astype(F32), a_all, precision=hp, preferred_element_type=F32)
    acs_row = lax.dot_general(a_all, (ri <= ci).astype(F32), (((0,), (0,)), ((), ())),
                              precision=hp, preferred_element_type=F32)
    left = lax.broadcasted_iota(jnp.int32, (Q, 128), 1) < P
    e64 = e64_ref[...]
    e128 = e128_ref[...]

    for g in range(G):
        h_lo = g * R
        dtx = _dot(_split3(dt_all[:, h_lo:h_lo + R]), e64)
        acs3 = _split3(acs_all[:, h_lo:h_lo + R])
        acsx = _dot(acs3, e64)
        acsb = _dot(acs3, e128)
        xs = xs_scr[:, g * W:(g + 1) * W]
        bm = bm_scr[:, g * N:(g + 1) * N]
        cm = cm_scr[:, g * N:(g + 1) * N]
        xdt = xs * dtx
        cb = _dot_nt(cm, bm)
        y_off = _dot_nt(cm, s_scr[g].astype(BF16))
        ys = []
        for j in range(R // 2):
            ms = []
            for r in (2 * j, 2 * j + 1):
                seg = acsb[:, r * 128:r * 128 + Q] - acs_row[h_lo + r:h_lo + r + 1, :]
                ms.append((cb * jnp.exp(jnp.where(causal, seg, -jnp.inf))).astype(BF16))
            xp = xdt[:, j * 128:(j + 1) * 128]
            rhs = jnp.concatenate([jnp.where(left, xp, 0.0), jnp.where(left, 0.0, xp)], axis=0).astype(BF16)
            ys.append(_dot(jnp.concatenate(ms, axis=1), rhs))
        y = jnp.concatenate(ys, axis=1) + y_off * jnp.exp(acsx) + dskx_ref[:, g * W:(g + 1) * W] * xs

        a_end = acsx[Q - 1:Q, :]
        xw = (xdt * jnp.exp(a_end - acsx)).astype(BF16)
        st = _dot_tn(xw, bm)
        for r in range(R):
            dec = jnp.exp(acs_row[h_lo + r:h_lo + r + 1, Q - 1:Q])
            s_scr[g, r * P:(r + 1) * P, :] = dec * s_scr[g, r * P:(r + 1) * P, :] + st[r * P:(r + 1) * P, :]

        v = y * _silu(z_ref[:, g * W:(g + 1) * W])
        y_ref[:, g * W:(g + 1) * W] = _rms(v, nssm_ref[:, g * W:(g + 1) * W]).astype(y_ref.dtype)

    @pl.when(c == n_chunks - 1)
    def _():
        sout_ref[...] = s_scr[...]


def _head_expander(width):
    R = HEADS_PER_GROUP
    row_head = jnp.arange(3 * R) % R
    col_head = jnp.arange(R * width) // width
    return (row_head[:, None] == col_head[None, :]).astype(BF16)


def _ssd(big3, small3, conv_state, conv_w, conv_b, dt_bias, a_log, d_skip, n_ssm, state0, *, q_len):
    B, L, _ = big3.shape
    G, N = N_GROUPS_S, D_STATE
    Q = q_len
    nc = L // Q
    pad = 128 - N_HEADS_S
    const = lambda b, c: (0, 0)
    in_specs = [
        pl.BlockSpec((None, Q, D_INNER), lambda b, c: (b, c, COL_X // D_INNER)),
        pl.BlockSpec((None, Q, D_INNER), lambda b, c: (b, c, COL_Z // D_INNER)),
        pl.BlockSpec((None, Q, BC_W), lambda b, c: (b, c, COL_B // BC_W)),
        pl.BlockSpec((None, Q, BC_W), lambda b, c: (b, c, COL_C // BC_W)),
        pl.BlockSpec((None, Q, 128), lambda b, c: (b, c, COL_DT // 128)),
        pl.BlockSpec((None, CONV_W - 1, CONV_DIM), lambda b, c: (b, 0, 0)),
        pl.BlockSpec((CONV_W, CONV_DIM), const),
        pl.BlockSpec((1, CONV_DIM), const),
        pl.BlockSpec((1, 128), const),
        pl.BlockSpec((1, 128), const),
        pl.BlockSpec((1, D_INNER), const),
        pl.BlockSpec((1, D_INNER), const),
        pl.BlockSpec((3 * HEADS_PER_GROUP, GROUP_W), const),
        pl.BlockSpec((3 * HEADS_PER_GROUP, HEADS_PER_GROUP * 128), const),
        pl.BlockSpec((None, G, GROUP_W, N), lambda b, c: (b, 0, 0, 0)),
    ]
    args = [big3, big3, big3, big3, small3, conv_state, conv_w, conv_b.reshape(1, CONV_DIM),
            jnp.pad(dt_bias, (0, pad)).reshape(1, 128), jnp.pad(a_log, (0, pad)).reshape(1, 128),
            jnp.repeat(d_skip, HEAD_DIM_S).reshape(1, D_INNER), n_ssm.reshape(1, D_INNER),
            _head_expander(HEAD_DIM_S), _head_expander(128), state0.reshape(B, G, GROUP_W, N)]
    y, s_out = pl.pallas_call(
        functools.partial(_ssd_kernel, q_len=Q, n_chunks=nc),
        grid=(B, nc),
        in_specs=in_specs,
        out_specs=[pl.BlockSpec((None, Q, D_INNER), lambda b, c: (b, c, 0)),
                   pl.BlockSpec((None, G, GROUP_W, N), lambda b, c: (b, 0, 0, 0))],
        out_shape=[jax.ShapeDtypeStruct((B, L, D_INNER), BF16),
                   jax.ShapeDtypeStruct((B, G, GROUP_W, N), F32)],
        scratch_shapes=[pltpu.VMEM((Q + HIST, D_INNER), F32), pltpu.VMEM((Q + HIST, BC_W), F32),
                        pltpu.VMEM((Q + HIST, BC_W), F32), pltpu.VMEM((Q, D_INNER), F32),
                        pltpu.VMEM((Q, BC_W), BF16), pltpu.VMEM((Q, BC_W), BF16),
                        pltpu.VMEM((G, GROUP_W, N), F32)],
        compiler_params=_cparams("parallel", "arbitrary"),
        name="ssd",
    )(*args)
    return y, s_out.reshape(B, N_HEADS_S, HEAD_DIM_S, N)


def _merge_kernel(oa_ref, ys_ref, wa_ref, ws_ref, g0_ref, g1_ref, bg_ref, o_ref):
    attn = _dot(oa_ref[...], wa_ref[...])
    ssm = _dot(ys_ref[...], ws_ref[...])
    bg = bg_ref[...]
    g0 = jax.nn.sigmoid(g0_ref[...] + bg[0:1, :])
    g1 = jax.nn.sigmoid(g1_ref[...] + bg[1:2, :])
    o_ref[...] = (g0 * attn + g1 * ssm).astype(o_ref.dtype)


def _merge(o_attn, y_ssm, w_o_attn, w_o_ssm, proj, b_gate, *, tm=512, tn=512):
    M = o_attn.shape[0]
    tm = min(tm, M)
    g0blk = COL_GATE // tn
    g1blk = (COL_GATE + D_MODEL) // tn
    return pl.pallas_call(
        _merge_kernel,
        grid=(M // tm, D_MODEL // tn),
        in_specs=[
            pl.BlockSpec((tm, o_attn.shape[1]), lambda i, j: (i, 0)),
            pl.BlockSpec((tm, D_INNER), lambda i, j: (i, 0)),
            pl.BlockSpec((w_o_attn.shape[0], tn), lambda i, j: (0, j)),
            pl.BlockSpec((D_INNER, tn), lambda i, j: (0, j)),
            pl.BlockSpec((tm, tn), lambda i, j: (i, g0blk + j)),
            pl.BlockSpec((tm, tn), lambda i, j: (i, g1blk + j)),
            pl.BlockSpec((2, tn), lambda i, j: (0, j)),
        ],
        out_specs=pl.BlockSpec((tm, tn), lambda i, j: (i, j)),
        out_shape=jax.ShapeDtypeStruct((M, D_MODEL), BF16),
        compiler_params=_cparams("parallel", "arbitrary"),
        name="merge",
    )(o_attn, y_ssm, w_o_attn, w_o_ssm, proj, proj, b_gate)


def _outproj_kernel(m_ref, w_ref, x_ref, o_ref):
    o_ref[...] = x_ref[...] + _dot(m_ref[...], w_ref[...])


def _outproj(merged, w_out, x, *, tm=512, tn=512):
    M = x.shape[0]
    tm = min(tm, M)
    return pl.pallas_call(
        _outproj_kernel,
        grid=(M // tm, D_MODEL // tn),
        in_specs=[
            pl.BlockSpec((tm, D_MODEL), lambda i, j: (i, 0)),
            pl.BlockSpec((D_MODEL, tn), lambda i, j: (0, j)),
            pl.BlockSpec((tm, tn), lambda i, j: (i, j)),
        ],
        out_specs=pl.BlockSpec((tm, tn), lambda i, j: (i, j)),
        out_shape=jax.ShapeDtypeStruct((M, D_MODEL), F32),
        compiler_params=_cparams("parallel", "arbitrary"),
        name="outproj",
    )(merged, w_out, x)


def _swap_halves(w):
    half = QK_ROPE // 2
    return jnp.concatenate([w[..., half:], w[..., :half]], axis=-1)


def _rope_table(pos):
    half = QK_ROPE // 2
    inv = jnp.power(ROPE_THETA, -jnp.arange(half, dtype=F32) / half)
    ang = pos.astype(F32)[:, None] * inv[None, :]
    cos, sin = jnp.cos(ang), jnp.sin(ang)
    return jnp.concatenate([cos, cos, -sin, sin], axis=-1)


def _last_conv_rows(big3):
    tail = big3[:, big3.shape[1] - (CONV_W - 1):, :]
    return jnp.concatenate([tail[..., COL_X:COL_X + D_INNER], tail[..., COL_B:COL_B + 2 * BC_W]], axis=-1)


def kernel(x_prompt, x_sample, cache_kv_latent, cache_k_rope, state_ssm, state_conv, norm_ffn1, w_ffn1_gate, w_ffn1_up, w_ffn1_down, norm_mix, w_in, b_gate, norm_q_a, w_q_b, norm_kv_a, w_kv_b, conv_w, conv_b, dt_bias, a_log, d_skip, norm_ssm, w_o_attn, w_o_ssm, w_out, norm_ffn2, w_ffn2_gate, w_ffn2_up, w_ffn2_down, norm_final):
    Bp, S, D = x_prompt.shape
    Bs, T, _ = x_sample.shape
    past = cache_kv_latent.shape[2]
    H = N_HEADS_A
    l = 0

    wi = w_in[l]
    o_kva = Q_LORA
    o_z = o_kva + KV_LORA + QK_ROPE
    o_xbc = o_z + D_INNER
    o_dt = o_xbc + CONV_DIM
    o_gate = o_dt + N_HEADS_S
    w_kr = wi[:, o_kva + KV_LORA:o_z]
    w_small = jnp.concatenate(
        [wi[:, :o_kva + KV_LORA], w_kr, _swap_halves(w_kr), wi[:, o_dt:o_gate],
         jnp.zeros((D, 128 - N_HEADS_S), wi.dtype)], axis=1).astype(BF16)
    w_big = jnp.concatenate(
        [wi[:, o_xbc:o_xbc + D_INNER], wi[:, o_z:o_xbc], wi[:, o_gate:], wi[:, o_xbc + D_INNER:o_dt]],
        axis=1).astype(BF16)
    wq3 = w_q_b[l].reshape(Q_LORA, H, QK_NOPE + QK_ROPE)
    wq = jnp.concatenate([wq3, _swap_halves(wq3[..., QK_NOPE:])], axis=-1).reshape(Q_LORA, H * QK_HEAD).astype(BF16)
    wkv3 = w_kv_b[l].reshape(KV_LORA, H, QK_NOPE + V_DIM)
    w_uk = wkv3[..., :QK_NOPE]
    w_uv = wkv3[..., QK_NOPE:]
    w_uk_flat = w_uk.reshape(KV_LORA, H * QK_NOPE).astype(BF16)
    w_uv_flat = w_uv.reshape(KV_LORA, H * V_DIM).astype(BF16)
    w_uk_t = w_uk.transpose(1, 2, 0).astype(BF16)
    w_uv_h = w_uv.transpose(1, 0, 2).astype(BF16)
    wg1, wu1, wd1 = w_ffn1_gate[l].astype(BF16), w_ffn1_up[l].astype(BF16), w_ffn1_down[l].astype(BF16)
    wg2, wu2, wd2 = w_ffn2_gate[l].astype(BF16), w_ffn2_up[l].astype(BF16), w_ffn2_down[l].astype(BF16)
    woa, wos, wo = w_o_attn[l].astype(BF16), w_o_ssm[l].astype(BF16), w_out[l].astype(BF16)

    def mixer_head(x):
        x1 = _ffn(x, norm_ffn1[l], wg1, wu1, wd1)
        small = _inproj(x1, norm_mix[l], w_small, tm=1024, tn=512)
        big = _inproj(x1, norm_mix[l], w_big, tm=1024, tn=1024)
        return x1, small, big

    def mixer_tail(x1, big, o_attn, y_ssm):
        merged = _merge(o_attn, y_ssm, woa, wos, big, b_gate[l])
        x2 = _outproj(merged, wo, x1)
        return _ffn(x2, norm_ffn2[l], wg2, wu2, wd2, norm_final)

    def ssd(big, small, batch, conv_state, state, q_len):
        big3 = big.reshape(batch, -1, BIG_W)
        y, new_state = _ssd(big3, small.reshape(batch, -1, SMALL_W), conv_state, conv_w[l], conv_b[l],
                            dt_bias[l], a_log[l], d_skip[l], norm_ssm[l], state, q_len=q_len)
        return y.reshape(-1, D_INNER), new_state, _last_conv_rows(big3)

    x1, small, big = mixer_head(x_prompt.reshape(Bp * S, D))
    cs_p = _rope_table(jnp.arange(S))
    p_lat, p_kr, q, k, v = _prep(small, cs_p, norm_q_a[l], norm_kv_a[l], wq, (w_uk_flat, w_uv_flat), prompt=True)
    o_attn = _attn_prompt(q.reshape(Bp, S, -1), k.reshape(Bp, S, -1), v.reshape(Bp, S, -1)).reshape(Bp * S, -1)
    y_ssm, p_ssm, p_conv = ssd(big, small, Bp, jnp.zeros((Bp, CONV_W - 1, CONV_DIM), F32),
                               jnp.zeros((Bp, N_HEADS_S, HEAD_DIM_S, D_STATE), F32), 128)
    y_prompt = mixer_tail(x1, big, o_attn, y_ssm).reshape(Bp, S, D)

    x1s, smalls, bigs = mixer_head(x_sample.reshape(Bs * T, D))
    cs_s = jnp.tile(_rope_table(past + jnp.arange(T)), (Bs, 1))
    s_lat, s_kr, qlat, qrot = _prep(smalls, cs_s, norm_q_a[l], norm_kv_a[l], wq, (w_uk_t,), prompt=False)
    o_lat = _attn_sample(qlat.reshape(Bs, T * H, KV_LORA), qrot.reshape(Bs, T * H, 128),
                         cache_kv_latent[l], cache_k_rope[l],
                         s_lat.reshape(Bs, T, KV_LORA), s_kr.reshape(Bs, T, QK_ROPE))
    o_attn_s = _uv_proj(o_lat.reshape(Bs * T, H * KV_LORA), w_uv_h)
    y_ssm_s, s_ssm, s_conv = ssd(bigs, smalls, Bs, state_conv[l], state_ssm[l], T)
    y_sample = mixer_tail(x1s, bigs, o_attn_s, y_ssm_s).reshape(Bs, T, D)

    return (y_prompt, y_sample,
            p_lat.reshape(1, Bp, S, KV_LORA), p_kr.reshape(1, Bp, S, QK_ROPE), p_ssm[None], p_conv[None],
            s_lat.reshape(1, Bs, T, KV_LORA), s_kr.reshape(1, Bs, T, QK_ROPE), s_ssm[None], s_conv[None])
```

```python
import functools
import math

import jax
import jax.numpy as jnp
from jax import lax
from jax.experimental import pallas as pl
from jax.experimental.pallas import tpu as pltpu

F32 = jnp.float32
BF16 = jnp.bfloat16

D_MODEL = 2048
CHUNK = 64
N_HEADS_A = 16
Q_LORA = 768
KV_LORA = 512
QK_NOPE = 128
QK_ROPE = 64
V_DIM = 128
ROPE_THETA = 10000.0
ATTN_SCALE = (QK_NOPE + QK_ROPE) ** -0.5
D_INNER = 2 * D_MODEL
HEAD_DIM_S = 64
N_HEADS_S = D_INNER // HEAD_DIM_S
N_GROUPS_S = 8
HEADS_PER_GROUP = N_HEADS_S // N_GROUPS_S
GROUP_W = D_INNER // N_GROUPS_S
D_STATE = 128
CONV_W = 4
CONV_DIM = D_INNER + 2 * N_GROUPS_S * D_STATE
D_FF = 5632
EPS = 1e-6

QK_HEAD = 256
COL_QA = 0
COL_LAT = Q_LORA
COL_KR = COL_LAT + KV_LORA
COL_DT = COL_KR + 2 * QK_ROPE
SMALL_W = COL_DT + 128
BC_W = N_GROUPS_S * D_STATE
COL_X = 0
COL_B = D_INNER
COL_C = COL_B + BC_W
COL_Z = 0
COL_GATE = D_INNER
ZG_W = COL_GATE + 2 * D_MODEL
LOG2E = math.log2(math.e)

VMEM_LIMIT = 56 * 1024 * 1024


def _cparams(*sem):
    return pltpu.CompilerParams(dimension_semantics=sem, vmem_limit_bytes=VMEM_LIMIT)


def _rms(x, g):
    return x * lax.rsqrt(jnp.mean(x * x, axis=-1, keepdims=True) + EPS) * g


def _silu(x):
    h = 0.5 * x
    return h + h * jnp.tanh(h)


def _dot(a, b):
    return jnp.dot(a, b, preferred_element_type=F32)


def _dot_nt(a, b):
    return lax.dot_general(a, b, (((1,), (1,)), ((), ())), preferred_element_type=F32)


def _dot_tn(a, b):
    return lax.dot_general(a, b, (((0,), (0,)), ((), ())), preferred_element_type=F32)


def _ffn_kernel(*refs, final_norm):
    if final_norm:
        x_ref, g_ref, wg_ref, wu_ref, wd_ref, gf_ref, o_ref, h_scr, acc_scr = refs
    else:
        x_ref, g_ref, wg_ref, wu_ref, wd_ref, o_ref, h_scr, acc_scr = refs
    j = pl.program_id(1)

    @pl.when(j == 0)
    def _():
        h_scr[...] = _rms(x_ref[...], g_ref[...]).astype(BF16)
        acc_scr[...] = jnp.zeros_like(acc_scr)

    h = h_scr[...]
    a = _silu(_dot(h, wg_ref[...])) * _dot(h, wu_ref[...])
    acc_scr[...] += _dot(a.astype(BF16), wd_ref[...])

    @pl.when(j == pl.num_programs(1) - 1)
    def _():
        y = x_ref[...] + 0.5 * acc_scr[...]
        if final_norm:
            y = _rms(y, gf_ref[...])
        o_ref[...] = y


def _ffn(x, g, wg, wu, wd, gf=None, *, tm=512, tf=512):
    M, D = x.shape
    F = wg.shape[1]
    tm = min(tm, M)
    in_specs = [
        pl.BlockSpec((tm, D), lambda i, j: (i, 0)),
        pl.BlockSpec((1, D), lambda i, j: (0, 0)),
        pl.BlockSpec((D, tf), lambda i, j: (0, j)),
        pl.BlockSpec((D, tf), lambda i, j: (0, j)),
        pl.BlockSpec((tf, D), lambda i, j: (j, 0)),
    ]
    args = [x, g.reshape(1, D), wg, wu, wd]
    if gf is not None:
        in_specs.append(pl.BlockSpec((1, D), lambda i, j: (0, 0)))
        args.append(gf.reshape(1, D))
    return pl.pallas_call(
        functools.partial(_ffn_kernel, final_norm=gf is not None),
        grid=(M // tm, F // tf),
        in_specs=in_specs,
        out_specs=pl.BlockSpec((tm, D), lambda i, j: (i, 0)),
        out_shape=jax.ShapeDtypeStruct((M, D), F32),
        scratch_shapes=[pltpu.VMEM((tm, D), BF16), pltpu.VMEM((tm, D), F32)],
        compiler_params=_cparams("parallel", "arbitrary"),
        name="ffn",
    )(*args)


def _relayout_kernel(w_ref, small_ref, xbc_ref, zg_ref):
    w = w_ref[...]
    o_lat = Q_LORA
    o_kr = o_lat + KV_LORA
    o_z = o_kr + QK_ROPE
    o_x = o_z + D_INNER
    o_bc = o_x + D_INNER
    o_dt = o_bc + 2 * BC_W
    o_gate = o_dt + N_HEADS_S
    half = QK_ROPE // 2
    small = jnp.concatenate(
        [w[:, :o_z], w[:, o_kr + half:o_z], w[:, o_kr:o_kr + half], w[:, o_dt:o_gate],
         jnp.zeros((w.shape[0], 128 - N_HEADS_S), F32)], axis=1)
    small_ref[...] = small.astype(BF16)
    xbc_ref[...] = w[:, o_x:o_dt].astype(BF16)
    zg_ref[:, COL_Z:COL_Z + D_INNER] = w[:, o_z:o_x].astype(BF16)
    zg_ref[:, COL_GATE:] = w[:, o_gate:].astype(BF16)


def _relayout_w_in(w, *, tk=128):
    K, N = w.shape
    widths = (SMALL_W, CONV_DIM, ZG_W)
    return pl.pallas_call(
        _relayout_kernel,
        grid=(K // tk,),
        in_specs=[pl.BlockSpec((tk, N), lambda i: (i, 0))],
        out_specs=[pl.BlockSpec((tk, n), lambda i: (i, 0)) for n in widths],
        out_shape=[jax.ShapeDtypeStruct((K, n), BF16) for n in widths],
        compiler_params=_cparams("parallel"),
        name="relayout_w_in",
    )(w)


HIST = 8


def _causal_conv_silu(hist, cur, w, bias):
    ext = jnp.concatenate([hist, cur], axis=0)
    acc = bias
    for k in range(CONV_W - 1):
        acc = acc + w[k:k + 1, :] * pltpu.roll(ext, CONV_W - 1 - k, axis=0)[HIST:, :]
    return _silu(acc + w[CONV_W - 1:CONV_W, :] * cur)


def _inproj_kernel(x_ref, g_ref, w_ref, o_ref, h_scr):
    @pl.when(pl.program_id(1) == 0)
    def _():
        h_scr[...] = _rms(x_ref[...], g_ref[...]).astype(BF16)

    o_ref[...] = _dot(h_scr[...], w_ref[...])


def _inproj(x, g, w, *, tm, tn):
    M, D = x.shape
    N = w.shape[1]
    tm = min(tm, M)
    return pl.pallas_call(
        _inproj_kernel,
        grid=(M // tm, N // tn),
        in_specs=[
            pl.BlockSpec((tm, D), lambda i, j: (i, 0)),
            pl.BlockSpec((1, D), lambda i, j: (0, 0)),
            pl.BlockSpec((D, tn), lambda i, j: (0, j)),
        ],
        out_specs=pl.BlockSpec((tm, tn), lambda i, j: (i, j)),
        out_shape=jax.ShapeDtypeStruct((M, N), F32),
        scratch_shapes=[pltpu.VMEM((tm, D), BF16)],
        compiler_params=_cparams("parallel", "arbitrary"),
        name="inproj",
    )(x, g.reshape(1, D), w)


def _rope_pair(blk, cs, keep):
    prod = blk * cs
    return jnp.where(keep, prod + pltpu.roll(prod, 64, axis=1), 0.0)


def _prep_common(small_ref, cs_ref, nq_ref, nkv_ref, wq_ref, ckv_ref, kr_ref):
    sm = small_ref[...]
    cs = cs_ref[...]
    keep = lax.broadcasted_iota(jnp.int32, cs.shape, 1) < QK_ROPE
    cq = _rms(sm[:, COL_QA:COL_QA + Q_LORA], nq_ref[...]).astype(BF16)
    q = _dot(cq, wq_ref[...]) * (ATTN_SCALE * LOG2E)
    ckv = _rms(sm[:, COL_LAT:COL_LAT + KV_LORA], nkv_ref[...])
    ckv_ref[...] = ckv
    krot = _rope_pair(sm[:, COL_KR:COL_KR + 128], cs, keep)
    kr_ref[...] = krot[:, :QK_ROPE]
    return q, ckv, krot, cs, keep


def _prep_prompt_kernel(small_ref, cs_ref, nq_ref, nkv_ref, wq_ref, wuk_ref, wuv_ref,
                        ckv_ref, kr_ref, q_ref, k_ref, v_ref):
    q, ckv, krot, cs, keep = _prep_common(small_ref, cs_ref, nq_ref, nkv_ref, wq_ref, ckv_ref, kr_ref)
    ckv_b = ckv.astype(BF16)
    kn = _dot(ckv_b, wuk_ref[...])
    krot_b = krot.astype(BF16)
    for h in range(N_HEADS_A):
        c0 = h * QK_HEAD
        q_ref[:, c0:c0 + 128] = q[:, c0:c0 + 128].astype(BF16)
        q_ref[:, c0 + 128:c0 + 256] = _rope_pair(q[:, c0 + 128:c0 + 256], cs, keep).astype(BF16)
        k_ref[:, c0:c0 + 128] = kn[:, h * 128:(h + 1) * 128].astype(BF16)
        k_ref[:, c0 + 128:c0 + 256] = krot_b
    v_ref[...] = _dot(ckv_b, wuv_ref[...]).astype(BF16)


def _prep_sample_kernel(small_ref, cs_ref, nq_ref, nkv_ref, wq_ref, wukt_ref,
                        ckv_ref, kr_ref, qlat_ref, qrot_ref):
    q, _, _, cs, keep = _prep_common(small_ref, cs_ref, nq_ref, nkv_ref, wq_ref, ckv_ref, kr_ref)
    for h in range(N_HEADS_A):
        c0 = h * QK_HEAD
        qn = q[:, c0:c0 + 128].astype(BF16)
        qlat_ref[:, h * KV_LORA:(h + 1) * KV_LORA] = _dot(qn, wukt_ref[h]).astype(BF16)
        qrot_ref[:, h * 128:(h + 1) * 128] = _rope_pair(q[:, c0 + 128:c0 + 256], cs, keep).astype(BF16)


def _prep(proj, cs_table, nq, nkv, wq, extra_w, *, prompt, tm=512):
    M = proj.shape[0]
    tm = min(tm, M)
    n_cs = cs_table.shape[0] // tm
    H = N_HEADS_A
    row = lambda i: (i, 0)
    const2 = lambda i: (0, 0)
    in_specs = [
        pl.BlockSpec((tm, SMALL_W), row),
        pl.BlockSpec((tm, 128), lambda i: (i % n_cs, 0)),
        pl.BlockSpec((1, Q_LORA), const2),
        pl.BlockSpec((1, KV_LORA), const2),
        pl.BlockSpec((Q_LORA, H * QK_HEAD), const2),
    ]
    out_specs = [pl.BlockSpec((tm, KV_LORA), row), pl.BlockSpec((tm, QK_ROPE), row)]
    out_shape = [jax.ShapeDtypeStruct((M, KV_LORA), F32), jax.ShapeDtypeStruct((M, QK_ROPE), F32)]
    if prompt:
        w_uk, w_uv = extra_w
        in_specs += [pl.BlockSpec((KV_LORA, H * QK_NOPE), const2), pl.BlockSpec((KV_LORA, H * V_DIM), const2)]
        out_specs += [pl.BlockSpec((tm, H * QK_HEAD), row), pl.BlockSpec((tm, H * QK_HEAD), row),
                      pl.BlockSpec((tm, H * V_DIM), row)]
        out_shape += [jax.ShapeDtypeStruct((M, H * QK_HEAD), BF16), jax.ShapeDtypeStruct((M, H * QK_HEAD), BF16),
                      jax.ShapeDtypeStruct((M, H * V_DIM), BF16)]
        body, extra = _prep_prompt_kernel, [w_uk, w_uv]
    else:
        (w_ukt,) = extra_w
        in_specs += [pl.BlockSpec((H, QK_NOPE, KV_LORA), lambda i: (0, 0, 0))]
        out_specs += [pl.BlockSpec((tm, H * KV_LORA), row), pl.BlockSpec((tm, H * 128), row)]
        out_shape += [jax.ShapeDtypeStruct((M, H * KV_LORA), BF16), jax.ShapeDtypeStruct((M, H * 128), BF16)]
        body, extra = _prep_sample_kernel, [w_ukt]
    return pl.pallas_call(
        body,
        grid=(M // tm,),
        in_specs=in_specs,
        out_specs=out_specs,
        out_shape=out_shape,
        compiler_params=_cparams("parallel"),
        name="prep_prompt" if prompt else "prep_sample",
    )(proj, cs_table, nq.reshape(1, -1), nkv.reshape(1, -1), wq, *extra)


def _attn_prompt_kernel(q_ref, k_ref, v_ref, bias_ref, o_ref, s_scr, p_scr, *, seq, tq):
    for qi in reversed(range(seq // tq)):
        kv = (qi + 1) * tq
        s_buf = s_scr.at[qi % 2]
        p_buf = p_scr.at[qi % 2]
        q = q_ref[qi * tq:kv, :]
        s_buf[:, :kv] = _dot_nt(q, k_ref[:kv, :])
        s_buf[:, kv - tq:kv] = s_buf[:, kv - tq:kv] + bias_ref[...]
        m = jnp.max(s_buf[:, :kv], axis=-1, keepdims=True)
        p = jnp.exp2(s_buf[:, :kv] - m)
        l = jnp.sum(p, axis=-1, keepdims=True)
        p_buf[:, :kv] = p.astype(BF16)
        acc = _dot(p_buf[:, :kv], v_ref[:kv, :])
        o_ref[qi * tq:kv, :] = (acc / l).astype(o_ref.dtype)


def _attn_prompt(q, k, v, *, tq=512):
    B, S, _ = q.shape
    H = N_HEADS_A
    r_chunk = lax.broadcasted_iota(jnp.int32, (tq, tq), 0) // CHUNK
    c_chunk = lax.broadcasted_iota(jnp.int32, (tq, tq), 1) // CHUNK
    bias = jnp.where(c_chunk <= r_chunk, 0.0, -jnp.inf).astype(F32)
    return pl.pallas_call(
        functools.partial(_attn_prompt_kernel, seq=S, tq=tq),
        grid=(B, H),
        in_specs=[
            pl.BlockSpec((None, S, QK_HEAD), lambda b, h: (b, 0, h)),
            pl.BlockSpec((None, S, QK_HEAD), lambda b, h: (b, 0, h)),
            pl.BlockSpec((None, S, V_DIM), lambda b, h: (b, 0, h)),
            pl.BlockSpec((tq, tq), lambda b, h: (0, 0)),
        ],
        out_specs=pl.BlockSpec((None, S, V_DIM), lambda b, h: (b, 0, h)),
        out_shape=jax.ShapeDtypeStruct((B, S, H * V_DIM), BF16),
        scratch_shapes=[pltpu.VMEM((2, tq, S), F32), pltpu.VMEM((2, tq, S), BF16)],
        compiler_params=_cparams("parallel", "parallel"),
        name="attn_prompt",
    )(q, k, v, bias)


def _attn_sample_kernel(qlat_ref, qrot_ref, clat_ref, ckr_ref, nlat_ref, nkr_ref, o_ref):
    H = N_HEADS_A
    T = qlat_ref.shape[0]
    ql = jnp.concatenate([qlat_ref[:, h * KV_LORA:(h + 1) * KV_LORA] for h in range(H)], axis=0)
    qr = jnp.concatenate([qrot_ref[:, h * 128:h * 128 + QK_ROPE] for h in range(H)], axis=0)
    lat = clat_ref[...].astype(BF16)
    nlat = nlat_ref[...].astype(BF16)
    s1 = _dot_nt(ql, lat) + _dot_nt(qr, ckr_ref[...].astype(BF16))
    s2 = _dot_nt(ql, nlat) + _dot_nt(qr, nkr_ref[...].astype(BF16))
    m = jnp.maximum(jnp.max(s1, axis=-1, keepdims=True), jnp.max(s2, axis=-1, keepdims=True))
    p1 = jnp.exp2(s1 - m)
    p2 = jnp.exp2(s2 - m)
    l = jnp.sum(p1, axis=-1, keepdims=True) + jnp.sum(p2, axis=-1, keepdims=True)
    o = _dot(p1.astype(BF16), lat) + _dot(p2.astype(BF16), nlat)
    o = (o / l).astype(o_ref.dtype)
    for h in range(H):
        o_ref[:, h * KV_LORA:(h + 1) * KV_LORA] = o[h * T:(h + 1) * T, :]


def _attn_sample(qlat, qrot, cache_lat, cache_kr, new_lat, new_kr):
    Bs, P, _ = cache_lat.shape
    T = new_lat.shape[1]
    H = N_HEADS_A
    b2 = lambda b: (b, 0)
    b3 = lambda b: (b, 0, 0)
    return pl.pallas_call(
        _attn_sample_kernel,
        grid=(Bs,),
        in_specs=[
            pl.BlockSpec((T, H * KV_LORA), b2),
            pl.BlockSpec((T, H * 128), b2),
            pl.BlockSpec((None, P, KV_LORA), b3),
            pl.BlockSpec((None, P, QK_ROPE), b3),
            pl.BlockSpec((None, T, KV_LORA), b3),
            pl.BlockSpec((None, T, QK_ROPE), b3),
        ],
        out_specs=pl.BlockSpec((T, H * KV_LORA), b2),
        out_shape=jax.ShapeDtypeStruct((Bs * T, H * KV_LORA), BF16),
        compiler_params=_cparams("parallel"),
        name="attn_sample",
    )(qlat, qrot, cache_lat, cache_kr, new_lat, new_kr)


def _uv_kernel(x_ref, w_ref, o_ref):
    o_ref[...] = _dot(x_ref[...], w_ref[...]).astype(o_ref.dtype)


def _uv_proj(olat, w_uv_heads):
    M = olat.shape[0]
    H = N_HEADS_A
    return pl.pallas_call(
        _uv_kernel,
        grid=(H,),
        in_specs=[pl.BlockSpec((M, KV_LORA), lambda h: (0, h)),
                  pl.BlockSpec((None, KV_LORA, V_DIM), lambda h: (h, 0, 0))],
        out_specs=pl.BlockSpec((M, V_DIM), lambda h: (0, h)),
        out_shape=jax.ShapeDtypeStruct((M, H * V_DIM), BF16),
        compiler_params=_cparams("parallel"),
        name="uv_proj",
    )(olat, w_uv_heads)


def _split3(v):
    hi = v.astype(BF16)
    r1 = v - hi.astype(F32)
    mid = r1.astype(BF16)
    lo = (r1 - mid.astype(F32)).astype(BF16)
    return jnp.concatenate([hi, mid, lo], axis=1)


def _ssd_kernel(x_ref, z_ref, b_ref, c_ref, dt_ref, cst_ref, cw_ref, cbias_ref, dtb_ref, alog_ref,
                dskx_ref, nssm_ref, e64_ref, e128_ref, s0_ref,
                y_ref, sout_ref, xbuf, bbuf, cbuf, xs_scr, bm_scr, cm_scr, s_scr, *, q_len, n_chunks):
    c = pl.program_id(1)
    Q = q_len
    P = HEAD_DIM_S
    G, R, N, W = N_GROUPS_S, HEADS_PER_GROUP, D_STATE, GROUP_W
    h0 = HIST - (CONV_W - 1)

    @pl.when(c == 0)
    def _():
        for buf, lo, hi in ((xbuf, 0, D_INNER), (bbuf, D_INNER, D_INNER + BC_W), (cbuf, D_INNER + BC_W, CONV_DIM)):
            buf[0:h0, :] = jnp.zeros((h0, hi - lo), F32)
            buf[h0:HIST, :] = cst_ref[:, lo:hi]
        s_scr[...] = s0_ref[...]

    def conv(hist, cur_ref, lo, hi):
        cur = cur_ref[...]
        out = _causal_conv_silu(hist[...], cur, cw_ref[:, lo:hi], cbias_ref[:, lo:hi])
        hist[...] = cur[Q - HIST:, :]
        return out

    xs_scr[...] = conv(xbuf, x_ref, 0, D_INNER)
    bm_scr[...] = conv(bbuf, b_ref, D_INNER, D_INNER + BC_W).astype(BF16)
    cm_scr[...] = conv(cbuf, c_ref, D_INNER + BC_W, CONV_DIM).astype(BF16)

    dt_all = jax.nn.softplus(dt_ref[...] + dtb_ref[...])
    a_all = dt_all * (-LOG2E * jnp.exp(alog_ref[...]))
    ri = lax.broadcasted_iota(jnp.int32, (Q, Q), 0)
    ci = lax.broadcasted_iota(jnp.int32, (Q, Q), 1)
    causal = ci <= ri
    hp = lax.Precision.HIGHEST
    acs_all = jnp.dot(causal.astype(F32), a_all, precision=hp, preferred_element_type=F32)
    acs_row = lax.dot_general(a_all, (ri <= ci).astype(F32), (((0,), (0,)), ((), ())),
                              precision=hp, preferred_element_type=F32)
    a_end = acs_all[Q - 1:Q, :]
    eacs_all = jnp.exp2(acs_all)
    w_all = dt_all * jnp.exp2(a_end - acs_all)
    dec_all = jnp.exp2(a_end)
    left = lax.broadcasted_iota(jnp.int32, (Q, 128), 1) < P
    e64 = e64_ref[...]
    e128 = e128_ref[...]

    for g in range(G):
        h_lo = g * R
        dtx = _dot(_split3(dt_all[:, h_lo:h_lo + R]), e64)
        eacsx = _dot(_split3(eacs_all[:, h_lo:h_lo + R]), e64)
        wx = _dot(_split3(w_all[:, h_lo:h_lo + R]), e64)
        acsb = _dot(_split3(acs_all[:, h_lo:h_lo + R]), e128)
        xs = xs_scr[:, g * W:(g + 1) * W]
        bm = bm_scr[:, g * N:(g + 1) * N]
        cm = cm_scr[:, g * N:(g + 1) * N]
        xdt = xs * dtx
        cb = _dot_nt(cm, bm)
        y_off = _dot_nt(cm, s_scr[g].astype(BF16))
        ys = []
        for j in range(R // 2):
            ms = []
            for r in (2 * j, 2 * j + 1):
                seg = acsb[:, r * 128:r * 128 + Q] - acs_row[h_lo + r:h_lo + r + 1, :]
                ms.append((cb * jnp.exp2(jnp.where(causal, seg, -jnp.inf))).astype(BF16))
            xp = xdt[:, j * 128:(j + 1) * 128]
            rhs = jnp.concatenate([jnp.where(left, xp, 0.0), jnp.where(left, 0.0, xp)], axis=0).astype(BF16)
            ys.append(_dot(jnp.concatenate(ms, axis=1), rhs))
        y = jnp.concatenate(ys, axis=1) + y_off * eacsx + dskx_ref[:, g * W:(g + 1) * W] * xs

        st = _dot_tn((xs * wx).astype(BF16), bm)
        for r in range(R):
            dec = dec_all[:, h_lo + r:h_lo + r + 1]
            s_scr[g, r * P:(r + 1) * P, :] = dec * s_scr[g, r * P:(r + 1) * P, :] + st[r * P:(r + 1) * P, :]

        v = y * _silu(z_ref[:, g * W:(g + 1) * W])
        y_ref[:, g * W:(g + 1) * W] = _rms(v, nssm_ref[:, g * W:(g + 1) * W]).astype(y_ref.dtype)

    @pl.when(c == n_chunks - 1)
    def _():
        sout_ref[...] = s_scr[...]


def _head_expander(width):
    R = HEADS_PER_GROUP
    row_head = jnp.arange(3 * R) % R
    col_head = jnp.arange(R * width) // width
    return (row_head[:, None] == col_head[None, :]).astype(BF16)


def _ssd(xbc3, zg3, small3, conv_state, conv_w, conv_b, dt_bias, a_log, d_skip, n_ssm, state0, *, q_len):
    B, L, _ = xbc3.shape
    G, N = N_GROUPS_S, D_STATE
    Q = q_len
    nc = L // Q
    pad = 128 - N_HEADS_S
    const = lambda b, c: (0, 0)
    in_specs = [
        pl.BlockSpec((None, Q, D_INNER), lambda b, c: (b, c, COL_X // D_INNER)),
        pl.BlockSpec((None, Q, D_INNER), lambda b, c: (b, c, COL_Z // D_INNER)),
        pl.BlockSpec((None, Q, BC_W), lambda b, c: (b, c, COL_B // BC_W)),
        pl.BlockSpec((None, Q, BC_W), lambda b, c: (b, c, COL_C // BC_W)),
        pl.BlockSpec((None, Q, 128), lambda b, c: (b, c, COL_DT // 128)),
        pl.BlockSpec((None, CONV_W - 1, CONV_DIM), lambda b, c: (b, 0, 0)),
        pl.BlockSpec((CONV_W, CONV_DIM), const),
        pl.BlockSpec((1, CONV_DIM), const),
        pl.BlockSpec((1, 128), const),
        pl.BlockSpec((1, 128), const),
        pl.BlockSpec((1, D_INNER), const),
        pl.BlockSpec((1, D_INNER), const),
        pl.BlockSpec((3 * HEADS_PER_GROUP, GROUP_W), const),
        pl.BlockSpec((3 * HEADS_PER_GROUP, HEADS_PER_GROUP * 128), const),
        pl.BlockSpec((None, G, GROUP_W, N), lambda b, c: (b, 0, 0, 0)),
    ]
    args = [xbc3, zg3, xbc3, xbc3, small3, conv_state, conv_w, conv_b.reshape(1, CONV_DIM),
            jnp.pad(dt_bias, (0, pad)).reshape(1, 128), jnp.pad(a_log, (0, pad)).reshape(1, 128),
            jnp.repeat(d_skip, HEAD_DIM_S).reshape(1, D_INNER), n_ssm.reshape(1, D_INNER),
            _head_expander(HEAD_DIM_S), _head_expander(128), state0.reshape(B, G, GROUP_W, N)]
    y, s_out = pl.pallas_call(
        functools.partial(_ssd_kernel, q_len=Q, n_chunks=nc),
        grid=(B, nc),
        in_specs=in_specs,
        out_specs=[pl.BlockSpec((None, Q, D_INNER), lambda b, c: (b, c, 0)),
                   pl.BlockSpec((None, G, GROUP_W, N), lambda b, c: (b, 0, 0, 0))],
        out_shape=[jax.ShapeDtypeStruct((B, L, D_INNER), BF16),
                   jax.ShapeDtypeStruct((B, G, GROUP_W, N), F32)],
        scratch_shapes=[pltpu.VMEM((HIST, D_INNER), F32), pltpu.VMEM((HIST, BC_W), F32),
                        pltpu.VMEM((HIST, BC_W), F32), pltpu.VMEM((Q, D_INNER), F32),
                        pltpu.VMEM((Q, BC_W), BF16), pltpu.VMEM((Q, BC_W), BF16),
                        pltpu.VMEM((G, GROUP_W, N), F32)],
        compiler_params=_cparams("parallel", "arbitrary"),
        name="ssd",
    )(*args)
    return y, s_out.reshape(B, N_HEADS_S, HEAD_DIM_S, N)


def _merge_kernel(oa_ref, ys_ref, wa_ref, ws_ref, g0_ref, g1_ref, bg_ref, o_ref):
    attn = _dot(oa_ref[...], wa_ref[...])
    ssm = _dot(ys_ref[...], ws_ref[...])
    bg = bg_ref[...]
    g0 = jax.nn.sigmoid(g0_ref[...] + bg[0:1, :])
    g1 = jax.nn.sigmoid(g1_ref[...] + bg[1:2, :])
    o_ref[...] = (g0 * attn + g1 * ssm).astype(o_ref.dtype)


def _merge(o_attn, y_ssm, w_o_attn, w_o_ssm, proj, b_gate, *, tm=512, tn=512):
    M = o_attn.shape[0]
    tm = min(tm, M)
    g0blk = COL_GATE // tn
    g1blk = (COL_GATE + D_MODEL) // tn
    return pl.pallas_call(
        _merge_kernel,
        grid=(M // tm, D_MODEL // tn),
        in_specs=[
            pl.BlockSpec((tm, o_attn.shape[1]), lambda i, j: (i, 0)),
            pl.BlockSpec((tm, D_INNER), lambda i, j: (i, 0)),
            pl.BlockSpec((w_o_attn.shape[0], tn), lambda i, j: (0, j)),
            pl.BlockSpec((D_INNER, tn), lambda i, j: (0, j)),
            pl.BlockSpec((tm, tn), lambda i, j: (i, g0blk + j)),
            pl.BlockSpec((tm, tn), lambda i, j: (i, g1blk + j)),
            pl.BlockSpec((2, tn), lambda i, j: (0, j)),
        ],
        out_specs=pl.BlockSpec((tm, tn), lambda i, j: (i, j)),
        out_shape=jax.ShapeDtypeStruct((M, D_MODEL), BF16),
        compiler_params=_cparams("parallel", "arbitrary"),
        name="merge",
    )(o_attn, y_ssm, w_o_attn, w_o_ssm, proj, proj, b_gate)


def _outproj_kernel(m_ref, w_ref, x_ref, o_ref):
    o_ref[...] = x_ref[...] + _dot(m_ref[...], w_ref[...])


def _outproj(merged, w_out, x, *, tm=512, tn=512):
    M = x.shape[0]
    tm = min(tm, M)
    return pl.pallas_call(
        _outproj_kernel,
        grid=(M // tm, D_MODEL // tn),
        in_specs=[
            pl.BlockSpec((tm, D_MODEL), lambda i, j: (i, 0)),
            pl.BlockSpec((D_MODEL, tn), lambda i, j: (0, j)),
            pl.BlockSpec((tm, tn), lambda i, j: (i, j)),
        ],
        out_specs=pl.BlockSpec((tm, tn), lambda i, j: (i, j)),
        out_shape=jax.ShapeDtypeStruct((M, D_MODEL), F32),
        compiler_params=_cparams("parallel", "arbitrary"),
        name="outproj",
    )(merged, w_out, x)


def _swap_halves(w):
    half = QK_ROPE // 2
    return jnp.concatenate([w[..., half:], w[..., :half]], axis=-1)


def _rope_table(pos):
    half = QK_ROPE // 2
    inv = jnp.power(ROPE_THETA, -jnp.arange(half, dtype=F32) / half)
    ang = pos.astype(F32)[:, None] * inv[None, :]
    cos, sin = jnp.cos(ang), jnp.sin(ang)
    return jnp.concatenate([cos, cos, -sin, sin], axis=-1)


def kernel(x_prompt, x_sample, cache_kv_latent, cache_k_rope, state_ssm, state_conv, norm_ffn1, w_ffn1_gate, w_ffn1_up, w_ffn1_down, norm_mix, w_in, b_gate, norm_q_a, w_q_b, norm_kv_a, w_kv_b, conv_w, conv_b, dt_bias, a_log, d_skip, norm_ssm, w_o_attn, w_o_ssm, w_out, norm_ffn2, w_ffn2_gate, w_ffn2_up, w_ffn2_down, norm_final):
    Bp, S, D = x_prompt.shape
    Bs, T, _ = x_sample.shape
    past = cache_kv_latent.shape[2]
    H = N_HEADS_A
    l = 0

    w_small, w_xbc, w_zg = _relayout_w_in(w_in[l])
    wq3 = w_q_b[l].reshape(Q_LORA, H, QK_NOPE + QK_ROPE)
    wq = jnp.concatenate([wq3, _swap_halves(wq3[..., QK_NOPE:])], axis=-1).reshape(Q_LORA, H * QK_HEAD).astype(BF16)
    wkv3 = w_kv_b[l].reshape(KV_LORA, H, QK_NOPE + V_DIM)
    w_uk = wkv3[..., :QK_NOPE]
    w_uv = wkv3[..., QK_NOPE:]
    w_uk_flat = w_uk.reshape(KV_LORA, H * QK_NOPE).astype(BF16)
    w_uv_flat = w_uv.reshape(KV_LORA, H * V_DIM).astype(BF16)
    w_uk_t = w_uk.transpose(1, 2, 0).astype(BF16)
    w_uv_h = w_uv.transpose(1, 0, 2).astype(BF16)
    wg1, wu1, wd1 = w_ffn1_gate[l].astype(BF16), w_ffn1_up[l].astype(BF16), w_ffn1_down[l].astype(BF16)
    wg2, wu2, wd2 = w_ffn2_gate[l].astype(BF16), w_ffn2_up[l].astype(BF16), w_ffn2_down[l].astype(BF16)
    woa, wos, wo = w_o_attn[l].astype(BF16), w_o_ssm[l].astype(BF16), w_out[l].astype(BF16)

    def mixer_tail(x1, zg, o_attn, y_ssm):
        merged = _merge(o_attn, y_ssm, woa, wos, zg, b_gate[l], tm=1024, tn=512)
        x2 = _outproj(merged, wo, x1, tm=1024, tn=1024)
        return _ffn(x2, norm_ffn2[l], wg2, wu2, wd2, norm_final)

    def mixer_head(x, batch):
        x1 = _ffn(x, norm_ffn1[l], wg1, wu1, wd1)
        small = _inproj(x1, norm_mix[l], w_small, tm=1024, tn=512)
        zg = _inproj(x1, norm_mix[l], w_zg, tm=1024, tn=1024)
        xbc = _inproj(x1, norm_mix[l], w_xbc, tm=1024, tn=1024)
        xbc3 = xbc.reshape(batch, -1, CONV_DIM)
        return x1, small, zg, xbc, xbc3[:, xbc3.shape[1] - (CONV_W - 1):, :]

    def ssd(xbc, zg, small, batch, conv_state, state, q_len):
        y, new_state = _ssd(xbc.reshape(batch, -1, CONV_DIM), zg.reshape(batch, -1, ZG_W),
                            small.reshape(batch, -1, SMALL_W), conv_state, conv_w[l], conv_b[l],
                            dt_bias[l], a_log[l], d_skip[l], norm_ssm[l], state, q_len=q_len)
        return y.reshape(-1, D_INNER), new_state

    x1, small, zg, xbc, p_conv = mixer_head(x_prompt.reshape(Bp * S, D), Bp)
    cs_p = _rope_table(jnp.arange(S))
    p_lat, p_kr, q, k, v = _prep(small, cs_p, norm_q_a[l], norm_kv_a[l], wq, (w_uk_flat, w_uv_flat), prompt=True)
    o_attn = _attn_prompt(q.reshape(Bp, S, -1), k.reshape(Bp, S, -1), v.reshape(Bp, S, -1)).reshape(Bp * S, -1)
    y_ssm, p_ssm = ssd(xbc, zg, small, Bp, jnp.zeros((Bp, CONV_W - 1, CONV_DIM), F32),
                       jnp.zeros((Bp, N_HEADS_S, HEAD_DIM_S, D_STATE), F32), 128)
    y_prompt = mixer_tail(x1, zg, o_attn, y_ssm).reshape(Bp, S, D)

    x1s, smalls, zgs, xbcs, s_conv = mixer_head(x_sample.reshape(Bs * T, D), Bs)
    cs_s = jnp.tile(_rope_table(past + jnp.arange(T)), (Bs, 1))
    s_lat, s_kr, qlat, qrot = _prep(smalls, cs_s, norm_q_a[l], norm_kv_a[l], wq, (w_uk_t,), prompt=False)
    o_lat = _attn_sample(qlat, qrot, cache_kv_latent[l], cache_k_rope[l],
                         s_lat.reshape(Bs, T, KV_LORA), s_kr.reshape(Bs, T, QK_ROPE))
    o_attn_s = _uv_proj(o_lat, w_uv_h)
    y_ssm_s, s_ssm = ssd(xbcs, zgs, smalls, Bs, state_conv[l], state_ssm[l], T)
    y_sample = mixer_tail(x1s, zgs, o_attn_s, y_ssm_s).reshape(Bs, T, D)

    return (y_prompt, y_sample,
            p_lat.reshape(1, Bp, S, KV_LORA), p_kr.reshape(1, Bp, S, QK_ROPE), p_ssm[None], p_conv[None],
            s_lat.reshape(1, Bs, T, KV_LORA), s_kr.reshape(1, Bs, T, QK_ROPE), s_ssm[None], s_conv[None])
```

```python
import functools
import math

import jax
import jax.numpy as jnp
from jax import lax
from jax.experimental import pallas as pl
from jax.experimental.pallas import tpu as pltpu

F32 = jnp.float32
BF16 = jnp.bfloat16

D_MODEL = 2048
CHUNK = 64
N_HEADS_A = 16
Q_LORA = 768
KV_LORA = 512
QK_NOPE = 128
QK_ROPE = 64
V_DIM = 128
ROPE_THETA = 10000.0
ATTN_SCALE = (QK_NOPE + QK_ROPE) ** -0.5
D_INNER = 2 * D_MODEL
HEAD_DIM_S = 64
N_HEADS_S = D_INNER // HEAD_DIM_S
N_GROUPS_S = 8
HEADS_PER_GROUP = N_HEADS_S // N_GROUPS_S
GROUP_W = D_INNER // N_GROUPS_S
D_STATE = 128
CONV_W = 4
CONV_DIM = D_INNER + 2 * N_GROUPS_S * D_STATE
D_FF = 5632
EPS = 1e-6

QK_HEAD = 256
COL_QA = 0
COL_LAT = Q_LORA
COL_KR = COL_LAT + KV_LORA
COL_DT = COL_KR + 2 * QK_ROPE
SMALL_W = COL_DT + 128
XS_W = CONV_DIM + SMALL_W
BC_W = N_GROUPS_S * D_STATE
COL_X = 0
COL_B = D_INNER
COL_C = COL_B + BC_W
COL_Z = 0
COL_GATE = D_INNER
ZG_W = COL_GATE + 2 * D_MODEL
LOG2E = math.log2(math.e)

VMEM_LIMIT = 60 * 1024 * 1024


def _cparams(*sem):
    return pltpu.CompilerParams(dimension_semantics=sem, vmem_limit_bytes=VMEM_LIMIT)


def _rms(x, g):
    return x * lax.rsqrt(jnp.mean(x * x, axis=-1, keepdims=True) + EPS) * g


def _silu(x):
    h = 0.5 * x
    return h + h * jnp.tanh(h)


def _dot(a, b):
    return jnp.dot(a, b, preferred_element_type=F32)


def _dot_nt(a, b):
    return lax.dot_general(a, b, (((1,), (1,)), ((), ())), preferred_element_type=F32)


def _dot_tn(a, b):
    return lax.dot_general(a, b, (((0,), (0,)), ((), ())), preferred_element_type=F32)


def _ffn_kernel(*refs, final_norm):
    if final_norm:
        x_ref, g_ref, wg_ref, wu_ref, wd_ref, gf_ref, o_ref, h_scr = refs
    else:
        x_ref, g_ref, wg_ref, wu_ref, wd_ref, o_ref, h_scr = refs
    j = pl.program_id(1)

    @pl.when(j == 0)
    def _():
        x = x_ref[...]
        h_scr[...] = _rms(x, g_ref[...]).astype(BF16)
        o_ref[...] = x

    h = h_scr[...]
    a = (0.5 * _silu(_dot(h, wg_ref[...]))) * _dot(h, wu_ref[...])
    o_ref[...] += _dot(a.astype(BF16), wd_ref[...])

    if final_norm:
        @pl.when(j == pl.num_programs(1) - 1)
        def _():
            o_ref[...] = _rms(o_ref[...], gf_ref[...])


def _ffn(x, g, wg, wu, wd, gf=None, *, tm=1024, tf=512):
    M, D = x.shape
    F = wg.shape[1]
    tm = min(tm, M)
    in_specs = [
        pl.BlockSpec((tm, D), lambda i, j: (i, 0)),
        pl.BlockSpec((1, D), lambda i, j: (0, 0)),
        pl.BlockSpec((D, tf), lambda i, j: (0, j)),
        pl.BlockSpec((D, tf), lambda i, j: (0, j)),
        pl.BlockSpec((tf, D), lambda i, j: (j, 0)),
    ]
    args = [x, g.reshape(1, D), wg, wu, wd]
    if gf is not None:
        in_specs.append(pl.BlockSpec((1, D), lambda i, j: (0, 0)))
        args.append(gf.reshape(1, D))
    return pl.pallas_call(
        functools.partial(_ffn_kernel, final_norm=gf is not None),
        grid=(M // tm, F // tf),
        in_specs=in_specs,
        out_specs=pl.BlockSpec((tm, D), lambda i, j: (i, 0)),
        out_shape=jax.ShapeDtypeStruct((M, D), F32),
        scratch_shapes=[pltpu.VMEM((tm, D), BF16)],
        compiler_params=_cparams("parallel", "arbitrary"),
        name="ffn",
    )(*args)


def _relayout_kernel(w_ref, xs_ref, zg_ref):
    w = w_ref[...]
    o_lat = Q_LORA
    o_kr = o_lat + KV_LORA
    o_z = o_kr + QK_ROPE
    o_x = o_z + D_INNER
    o_bc = o_x + D_INNER
    o_dt = o_bc + 2 * BC_W
    o_gate = o_dt + N_HEADS_S
    half = QK_ROPE // 2
    small = jnp.concatenate(
        [w[:, :o_z], w[:, o_kr + half:o_z], w[:, o_kr:o_kr + half], w[:, o_dt:o_gate],
         jnp.zeros((w.shape[0], 128 - N_HEADS_S), F32)], axis=1)
    xs_ref[:, :CONV_DIM] = w[:, o_x:o_dt].astype(BF16)
    xs_ref[:, CONV_DIM:] = small.astype(BF16)
    zg_ref[:, COL_Z:COL_Z + D_INNER] = w[:, o_z:o_x].astype(BF16)
    zg_ref[:, COL_GATE:] = w[:, o_gate:].astype(BF16)


def _relayout_w_in(w, *, tk=128):
    K, N = w.shape
    widths = (XS_W, ZG_W)
    return pl.pallas_call(
        _relayout_kernel,
        grid=(K // tk,),
        in_specs=[pl.BlockSpec((tk, N), lambda i: (i, 0))],
        out_specs=[pl.BlockSpec((tk, n), lambda i: (i, 0)) for n in widths],
        out_shape=[jax.ShapeDtypeStruct((K, n), BF16) for n in widths],
        compiler_params=_cparams("parallel"),
        name="relayout_w_in",
    )(w)


HIST = 8


def _causal_conv_silu(hist, cur, w, bias):
    ext = jnp.concatenate([hist, cur], axis=0)
    acc = bias
    for k in range(CONV_W - 1):
        acc = acc + w[k:k + 1, :] * pltpu.roll(ext, CONV_W - 1 - k, axis=0)[HIST:, :]
    return _silu(acc + w[CONV_W - 1:CONV_W, :] * cur)


def _inproj_kernel(x_ref, g_ref, w_ref, o_ref, h_scr):
    @pl.when(pl.program_id(1) == 0)
    def _():
        h_scr[...] = _rms(x_ref[...], g_ref[...]).astype(BF16)

    o_ref[...] = _dot(h_scr[...], w_ref[...])


def _inproj(x, g, w, *, tm, tn):
    M, D = x.shape
    N = w.shape[1]
    tm = min(tm, M)
    return pl.pallas_call(
        _inproj_kernel,
        grid=(M // tm, N // tn),
        in_specs=[
            pl.BlockSpec((tm, D), lambda i, j: (i, 0)),
            pl.BlockSpec((1, D), lambda i, j: (0, 0)),
            pl.BlockSpec((D, tn), lambda i, j: (0, j)),
        ],
        out_specs=pl.BlockSpec((tm, tn), lambda i, j: (i, j)),
        out_shape=jax.ShapeDtypeStruct((M, N), F32),
        scratch_shapes=[pltpu.VMEM((tm, D), BF16)],
        compiler_params=_cparams("parallel", "arbitrary"),
        name="inproj",
    )(x, g.reshape(1, D), w)


def _rope_pair(blk, cs, keep):
    prod = blk * cs
    return jnp.where(keep, prod + pltpu.roll(prod, 64, axis=1), 0.0)


def _prep_common(small_ref, cs_ref, nq_ref, nkv_ref, wq_ref, ckv_ref, kr_ref):
    sm = small_ref[...]
    cs = cs_ref[...]
    keep = lax.broadcasted_iota(jnp.int32, cs.shape, 1) < QK_ROPE
    cq = _rms(sm[:, COL_QA:COL_QA + Q_LORA], nq_ref[...]).astype(BF16)
    q = _dot(cq, wq_ref[...]) * (ATTN_SCALE * LOG2E)
    ckv = _rms(sm[:, COL_LAT:COL_LAT + KV_LORA], nkv_ref[...])
    ckv_ref[...] = ckv
    krot = _rope_pair(sm[:, COL_KR:COL_KR + 128], cs, keep)
    kr_ref[...] = krot[:, :QK_ROPE]
    return q, ckv, krot, cs, keep


def _prep_prompt_kernel(small_ref, cs_ref, nq_ref, nkv_ref, wq_ref, wuk_ref, wuv_ref,
                        ckv_ref, kr_ref, q_ref, k_ref, v_ref):
    q, ckv, krot, cs, keep = _prep_common(small_ref, cs_ref, nq_ref, nkv_ref, wq_ref, ckv_ref, kr_ref)
    ckv_b = ckv.astype(BF16)
    kn = _dot(ckv_b, wuk_ref[...])
    krot_b = krot.astype(BF16)
    for h in range(N_HEADS_A):
        c0 = h * QK_HEAD
        q_ref[:, c0:c0 + 128] = q[:, c0:c0 + 128].astype(BF16)
        q_ref[:, c0 + 128:c0 + 256] = _rope_pair(q[:, c0 + 128:c0 + 256], cs, keep).astype(BF16)
        k_ref[:, c0:c0 + 128] = kn[:, h * 128:(h + 1) * 128].astype(BF16)
        k_ref[:, c0 + 128:c0 + 256] = krot_b
    v_ref[...] = _dot(ckv_b, wuv_ref[...]).astype(BF16)


def _prep_sample_kernel(small_ref, cs_ref, nq_ref, nkv_ref, wq_ref, wukt_ref,
                        ckv_ref, kr_ref, qlat_ref, qrot_ref):
    q, _, _, cs, keep = _prep_common(small_ref, cs_ref, nq_ref, nkv_ref, wq_ref, ckv_ref, kr_ref)
    for h in range(N_HEADS_A):
        c0 = h * QK_HEAD
        qn = q[:, c0:c0 + 128].astype(BF16)
        qlat_ref[:, h * KV_LORA:(h + 1) * KV_LORA] = _dot(qn, wukt_ref[h]).astype(BF16)
        qrot_ref[:, h * 128:(h + 1) * 128] = _rope_pair(q[:, c0 + 128:c0 + 256], cs, keep).astype(BF16)


def _prep(proj, cs_table, nq, nkv, wq, extra_w, *, prompt, tm=512):
    M = proj.shape[0]
    tm = min(tm, M)
    n_cs = cs_table.shape[0] // tm
    H = N_HEADS_A
    row = lambda i: (i, 0)
    const2 = lambda i: (0, 0)
    in_specs = [
        pl.BlockSpec((tm, SMALL_W), lambda i: (i, CONV_DIM // SMALL_W)),
        pl.BlockSpec((tm, 128), lambda i: (i % n_cs, 0)),
        pl.BlockSpec((1, Q_LORA), const2),
        pl.BlockSpec((1, KV_LORA), const2),
        pl.BlockSpec((Q_LORA, H * QK_HEAD), const2),
    ]
    out_specs = [pl.BlockSpec((tm, KV_LORA), row), pl.BlockSpec((tm, QK_ROPE), row)]
    out_shape = [jax.ShapeDtypeStruct((M, KV_LORA), F32), jax.ShapeDtypeStruct((M, QK_ROPE), F32)]
    if prompt:
        w_uk, w_uv = extra_w
        in_specs += [pl.BlockSpec((KV_LORA, H * QK_NOPE), const2), pl.BlockSpec((KV_LORA, H * V_DIM), const2)]
        out_specs += [pl.BlockSpec((tm, H * QK_HEAD), row), pl.BlockSpec((tm, H * QK_HEAD), row),
                      pl.BlockSpec((tm, H * V_DIM), row)]
        out_shape += [jax.ShapeDtypeStruct((M, H * QK_HEAD), BF16), jax.ShapeDtypeStruct((M, H * QK_HEAD), BF16),
                      jax.ShapeDtypeStruct((M, H * V_DIM), BF16)]
        body, extra = _prep_prompt_kernel, [w_uk, w_uv]
    else:
        (w_ukt,) = extra_w
        in_specs += [pl.BlockSpec((H, QK_NOPE, KV_LORA), lambda i: (0, 0, 0))]
        out_specs += [pl.BlockSpec((tm, H * KV_LORA), row), pl.BlockSpec((tm, H * 128), row)]
        out_shape += [jax.ShapeDtypeStruct((M, H * KV_LORA), BF16), jax.ShapeDtypeStruct((M, H * 128), BF16)]
        body, extra = _prep_sample_kernel, [w_ukt]
    return pl.pallas_call(
        body,
        grid=(M // tm,),
        in_specs=in_specs,
        out_specs=out_specs,
        out_shape=out_shape,
        compiler_params=_cparams("parallel"),
        name="prep_prompt" if prompt else "prep_sample",
    )(proj, cs_table, nq.reshape(1, -1), nkv.reshape(1, -1), wq, *extra)


def _attn_prompt_kernel(q_ref, k_ref, v_ref, bias_ref, o_ref, s_scr, p_scr, *, seq, tq):
    for qi in reversed(range(seq // tq)):
        kv = (qi + 1) * tq
        s_buf = s_scr.at[qi % 2]
        p_buf = p_scr.at[qi % 2]
        q = q_ref[qi * tq:kv, :]
        s_buf[:, :kv] = _dot_nt(q, k_ref[:kv, :])
        s_buf[:, kv - tq:kv] = s_buf[:, kv - tq:kv] + bias_ref[...]
        m = jnp.max(s_buf[:, :kv], axis=-1, keepdims=True)
        p = jnp.exp2(s_buf[:, :kv] - m)
        l = jnp.sum(p, axis=-1, keepdims=True)
        p_buf[:, :kv] = p.astype(BF16)
        acc = _dot(p_buf[:, :kv], v_ref[:kv, :])
        o_ref[qi * tq:kv, :] = (acc / l).astype(o_ref.dtype)


def _attn_prompt(q, k, v, *, tq=512):
    B, S, _ = q.shape
    H = N_HEADS_A
    r_chunk = lax.broadcasted_iota(jnp.int32, (tq, tq), 0) // CHUNK
    c_chunk = lax.broadcasted_iota(jnp.int32, (tq, tq), 1) // CHUNK
    bias = jnp.where(c_chunk <= r_chunk, 0.0, -jnp.inf).astype(F32)
    return pl.pallas_call(
        functools.partial(_attn_prompt_kernel, seq=S, tq=tq),
        grid=(B, H),
        in_specs=[
            pl.BlockSpec((None, S, QK_HEAD), lambda b, h: (b, 0, h)),
            pl.BlockSpec((None, S, QK_HEAD), lambda b, h: (b, 0, h)),
            pl.BlockSpec((None, S, V_DIM), lambda b, h: (b, 0, h)),
            pl.BlockSpec((tq, tq), lambda b, h: (0, 0)),
        ],
        out_specs=pl.BlockSpec((None, S, V_DIM), lambda b, h: (b, 0, h)),
        out_shape=jax.ShapeDtypeStruct((B, S, H * V_DIM), BF16),
        scratch_shapes=[pltpu.VMEM((2, tq, S), F32), pltpu.VMEM((2, tq, S), BF16)],
        compiler_params=_cparams("parallel", "parallel"),
        name="attn_prompt",
    )(q, k, v, bias)


def _attn_sample_kernel(qlat_ref, qrot_ref, clat_ref, ckrt_ref, nlat_ref, nkr_ref, o_ref):
    H = N_HEADS_A
    T = qlat_ref.shape[0]
    ql = jnp.concatenate([qlat_ref[:, h * KV_LORA:(h + 1) * KV_LORA] for h in range(H)], axis=0)
    qr = jnp.concatenate([qrot_ref[:, h * 128:h * 128 + QK_ROPE] for h in range(H)], axis=0)
    lat = clat_ref[...].astype(BF16)
    nlat = nlat_ref[...].astype(BF16)
    s1 = _dot_nt(ql, lat) + _dot(qr, ckrt_ref[...].astype(BF16))
    s2 = _dot_nt(ql, nlat) + _dot_nt(qr, nkr_ref[...].astype(BF16))
    m = jnp.maximum(jnp.max(s1, axis=-1, keepdims=True), jnp.max(s2, axis=-1, keepdims=True))
    p1 = jnp.exp2(s1 - m)
    p2 = jnp.exp2(s2 - m)
    l = jnp.sum(p1, axis=-1, keepdims=True) + jnp.sum(p2, axis=-1, keepdims=True)
    o = _dot(p1.astype(BF16), lat) + _dot(p2.astype(BF16), nlat)
    o = (o / l).astype(o_ref.dtype)
    for h in range(H):
        o_ref[:, h * KV_LORA:(h + 1) * KV_LORA] = o[h * T:(h + 1) * T, :]


def _attn_sample(qlat, qrot, cache_lat, cache_kr_t, new_lat, new_kr):
    Bs, P, _ = cache_lat.shape
    T = new_lat.shape[1]
    H = N_HEADS_A
    b2 = lambda b: (b, 0)
    b3 = lambda b: (b, 0, 0)
    return pl.pallas_call(
        _attn_sample_kernel,
        grid=(Bs,),
        in_specs=[
            pl.BlockSpec((T, H * KV_LORA), b2),
            pl.BlockSpec((T, H * 128), b2),
            pl.BlockSpec((None, P, KV_LORA), b3),
            pl.BlockSpec((None, QK_ROPE, P), b3),
            pl.BlockSpec((None, T, KV_LORA), b3),
            pl.BlockSpec((None, T, QK_ROPE), b3),
        ],
        out_specs=pl.BlockSpec((T, H * KV_LORA), b2),
        out_shape=jax.ShapeDtypeStruct((Bs * T, H * KV_LORA), BF16),
        compiler_params=_cparams("parallel"),
        name="attn_sample",
    )(qlat, qrot, cache_lat, cache_kr_t, new_lat, new_kr)


def _uv_kernel(x_ref, w_ref, o_ref):
    o_ref[...] = _dot(x_ref[...], w_ref[...]).astype(o_ref.dtype)


def _uv_proj(olat, w_uv_heads):
    M = olat.shape[0]
    H = N_HEADS_A
    return pl.pallas_call(
        _uv_kernel,
        grid=(H,),
        in_specs=[pl.BlockSpec((M, KV_LORA), lambda h: (0, h)),
                  pl.BlockSpec((None, KV_LORA, V_DIM), lambda h: (h, 0, 0))],
        out_specs=pl.BlockSpec((M, V_DIM), lambda h: (0, h)),
        out_shape=jax.ShapeDtypeStruct((M, H * V_DIM), BF16),
        compiler_params=_cparams("parallel"),
        name="uv_proj",
    )(olat, w_uv_heads)


def _split3(v):
    hi = v.astype(BF16)
    r1 = v - hi.astype(F32)
    mid = r1.astype(BF16)
    lo = (r1 - mid.astype(F32)).astype(BF16)
    return jnp.concatenate([hi, mid, lo], axis=1)


def _ssd_kernel(x_ref, z_ref, b_ref, c_ref, dt_ref, cst_ref, cw_ref, cbias_ref, dtb_ref, alog_ref,
                dskx_ref, nssm_ref, e64_ref, e128_ref, s0_ref,
                y_ref, sout_ref, xbuf, bbuf, cbuf, xs_scr, bm_scr, cm_scr, s_scr, *, q_len, n_chunks):
    c = pl.program_id(1)
    Q = q_len
    P = HEAD_DIM_S
    G, R, N, W = N_GROUPS_S, HEADS_PER_GROUP, D_STATE, GROUP_W
    h0 = HIST - (CONV_W - 1)

    @pl.when(c == 0)
    def _():
        for buf, lo, hi in ((xbuf, 0, D_INNER), (bbuf, D_INNER, D_INNER + BC_W), (cbuf, D_INNER + BC_W, CONV_DIM)):
            buf[0:h0, :] = jnp.zeros((h0, hi - lo), F32)
            buf[h0:HIST, :] = cst_ref[:, lo:hi]
        s_scr[...] = s0_ref[...]

    def conv(hist, cur_ref, lo, hi):
        cur = cur_ref[...]
        out = _causal_conv_silu(hist[...], cur, cw_ref[:, lo:hi], cbias_ref[:, lo:hi])
        hist[...] = cur[Q - HIST:, :]
        return out

    xs_scr[...] = conv(xbuf, x_ref, 0, D_INNER)
    bm_scr[...] = conv(bbuf, b_ref, D_INNER, D_INNER + BC_W).astype(BF16)
    cm_scr[...] = conv(cbuf, c_ref, D_INNER + BC_W, CONV_DIM).astype(BF16)

    dt_all = jax.nn.softplus(dt_ref[...] + dtb_ref[...])
    a_all = dt_all * (-LOG2E * jnp.exp(alog_ref[...]))
    ri = lax.broadcasted_iota(jnp.int32, (Q, Q), 0)
    ci = lax.broadcasted_iota(jnp.int32, (Q, Q), 1)
    causal = ci <= ri
    hp = lax.Precision.HIGHEST
    acs_all = jnp.dot(causal.astype(F32), a_all, precision=hp, preferred_element_type=F32)
    acs_row = lax.dot_general(a_all, (ri <= ci).astype(F32), (((0,), (0,)), ((), ())),
                              precision=hp, preferred_element_type=F32)
    a_end = acs_all[Q - 1:Q, :]
    eacs_all = jnp.exp2(acs_all)
    w_all = dt_all * jnp.exp2(a_end - acs_all)
    dec_all = jnp.exp2(a_end)
    left = lax.broadcasted_iota(jnp.int32, (Q, 128), 1) < P
    e64 = e64_ref[...]
    e128 = e128_ref[...]

    for g in range(G):
        h_lo = g * R
        dtx = _dot(_split3(dt_all[:, h_lo:h_lo + R]), e64)
        eacsx = _dot(_split3(eacs_all[:, h_lo:h_lo + R]), e64)
        wx = _dot(_split3(w_all[:, h_lo:h_lo + R]), e64)
        acsb = _dot(_split3(acs_all[:, h_lo:h_lo + R]), e128)
        xs = xs_scr[:, g * W:(g + 1) * W]
        bm = bm_scr[:, g * N:(g + 1) * N]
        cm = cm_scr[:, g * N:(g + 1) * N]
        xdt = xs * dtx
        cb = _dot_nt(cm, bm)
        y_off = _dot_nt(cm, s_scr[g].astype(BF16))
        ys = []
        for j in range(R // 2):
            ms = []
            for r in (2 * j, 2 * j + 1):
                seg = acsb[:, r * 128:r * 128 + Q] - acs_row[h_lo + r:h_lo + r + 1, :]
                ms.append((cb * jnp.exp2(jnp.where(causal, seg, -jnp.inf))).astype(BF16))
            xp = xdt[:, j * 128:(j + 1) * 128]
            rhs = jnp.concatenate([jnp.where(left, xp, 0.0), jnp.where(left, 0.0, xp)], axis=0).astype(BF16)
            ys.append(_dot(jnp.concatenate(ms, axis=1), rhs))
        y = jnp.concatenate(ys, axis=1) + y_off * eacsx + dskx_ref[:, g * W:(g + 1) * W] * xs

        st = _dot_tn((xs * wx).astype(BF16), bm)
        for r in range(R):
            dec = dec_all[:, h_lo + r:h_lo + r + 1]
            s_scr[g, r * P:(r + 1) * P, :] = dec * s_scr[g, r * P:(r + 1) * P, :] + st[r * P:(r + 1) * P, :]

        v = y * _silu(z_ref[:, g * W:(g + 1) * W])
        y_ref[:, g * W:(g + 1) * W] = _rms(v, nssm_ref[:, g * W:(g + 1) * W]).astype(y_ref.dtype)

    @pl.when(c == n_chunks - 1)
    def _():
        sout_ref[...] = s_scr[...]


def _head_expander(width):
    R = HEADS_PER_GROUP
    row_head = jnp.arange(3 * R) % R
    col_head = jnp.arange(R * width) // width
    return (row_head[:, None] == col_head[None, :]).astype(BF16)


def _ssd(xs3, zg3, conv_state, conv_w, conv_b, dt_bias, a_log, d_skip, n_ssm, state0, *, q_len):
    B, L, _ = xs3.shape
    G, N = N_GROUPS_S, D_STATE
    Q = q_len
    nc = L // Q
    pad = 128 - N_HEADS_S
    const = lambda b, c: (0, 0)
    in_specs = [
        pl.BlockSpec((None, Q, D_INNER), lambda b, c: (b, c, COL_X // D_INNER)),
        pl.BlockSpec((None, Q, D_INNER), lambda b, c: (b, c, COL_Z // D_INNER)),
        pl.BlockSpec((None, Q, BC_W), lambda b, c: (b, c, COL_B // BC_W)),
        pl.BlockSpec((None, Q, BC_W), lambda b, c: (b, c, COL_C // BC_W)),
        pl.BlockSpec((None, Q, 128), lambda b, c: (b, c, (CONV_DIM + COL_DT) // 128)),
        pl.BlockSpec((None, CONV_W - 1, CONV_DIM), lambda b, c: (b, 0, 0)),
        pl.BlockSpec((CONV_W, CONV_DIM), const),
        pl.BlockSpec((1, CONV_DIM), const),
        pl.BlockSpec((1, 128), const),
        pl.BlockSpec((1, 128), const),
        pl.BlockSpec((1, D_INNER), const),
        pl.BlockSpec((1, D_INNER), const),
        pl.BlockSpec((3 * HEADS_PER_GROUP, GROUP_W), const),
        pl.BlockSpec((3 * HEADS_PER_GROUP, HEADS_PER_GROUP * 128), const),
        pl.BlockSpec((None, G, GROUP_W, N), lambda b, c: (b, 0, 0, 0)),
    ]
    args = [xs3, zg3, xs3, xs3, xs3, conv_state, conv_w, conv_b.reshape(1, CONV_DIM),
            jnp.pad(dt_bias, (0, pad)).reshape(1, 128), jnp.pad(a_log, (0, pad)).reshape(1, 128),
            jnp.repeat(d_skip, HEAD_DIM_S).reshape(1, D_INNER), n_ssm.reshape(1, D_INNER),
            _head_expander(HEAD_DIM_S), _head_expander(128), state0.reshape(B, G, GROUP_W, N)]
    y, s_out = pl.pallas_call(
        functools.partial(_ssd_kernel, q_len=Q, n_chunks=nc),
        grid=(B, nc),
        in_specs=in_specs,
        out_specs=[pl.BlockSpec((None, Q, D_INNER), lambda b, c: (b, c, 0)),
                   pl.BlockSpec((None, G, GROUP_W, N), lambda b, c: (b, 0, 0, 0))],
        out_shape=[jax.ShapeDtypeStruct((B, L, D_INNER), BF16),
                   jax.ShapeDtypeStruct((B, G, GROUP_W, N), F32)],
        scratch_shapes=[pltpu.VMEM((HIST, D_INNER), F32), pltpu.VMEM((HIST, BC_W), F32),
                        pltpu.VMEM((HIST, BC_W), F32), pltpu.VMEM((Q, D_INNER), F32),
                        pltpu.VMEM((Q, BC_W), BF16), pltpu.VMEM((Q, BC_W), BF16),
                        pltpu.VMEM((G, GROUP_W, N), F32)],
        compiler_params=_cparams("parallel", "arbitrary"),
        name="ssd",
    )(*args)
    return y, s_out.reshape(B, N_HEADS_S, HEAD_DIM_S, N)


def _merge_kernel(oa_ref, ys_ref, wa_ref, ws_ref, g0_ref, g1_ref, bg_ref, o_ref):
    attn = _dot(oa_ref[...], wa_ref[...])
    ssm = _dot(ys_ref[...], ws_ref[...])
    bg = bg_ref[...]
    g0 = jax.nn.sigmoid(g0_ref[...] + bg[0:1, :])
    g1 = jax.nn.sigmoid(g1_ref[...] + bg[1:2, :])
    o_ref[...] = (g0 * attn + g1 * ssm).astype(o_ref.dtype)


def _merge(o_attn, y_ssm, w_o_attn, w_o_ssm, proj, b_gate, *, tm=512, tn=512):
    M = o_attn.shape[0]
    tm = min(tm, M)
    g0blk = COL_GATE // tn
    g1blk = (COL_GATE + D_MODEL) // tn
    return pl.pallas_call(
        _merge_kernel,
        grid=(M // tm, D_MODEL // tn),
        in_specs=[
            pl.BlockSpec((tm, o_attn.shape[1]), lambda i, j: (i, 0)),
            pl.BlockSpec((tm, D_INNER), lambda i, j: (i, 0)),
            pl.BlockSpec((w_o_attn.shape[0], tn), lambda i, j: (0, j)),
            pl.BlockSpec((D_INNER, tn), lambda i, j: (0, j)),
            pl.BlockSpec((tm, tn), lambda i, j: (i, g0blk + j)),
            pl.BlockSpec((tm, tn), lambda i, j: (i, g1blk + j)),
            pl.BlockSpec((2, tn), lambda i, j: (0, j)),
        ],
        out_specs=pl.BlockSpec((tm, tn), lambda i, j: (i, j)),
        out_shape=jax.ShapeDtypeStruct((M, D_MODEL), BF16),
        compiler_params=_cparams("parallel", "arbitrary"),
        name="merge",
    )(o_attn, y_ssm, w_o_attn, w_o_ssm, proj, proj, b_gate)


def _outproj_kernel(m_ref, w_ref, x_ref, o_ref):
    o_ref[...] = x_ref[...] + _dot(m_ref[...], w_ref[...])


def _outproj(merged, w_out, x, *, tm=512, tn=512):
    M = x.shape[0]
    tm = min(tm, M)
    return pl.pallas_call(
        _outproj_kernel,
        grid=(M // tm, D_MODEL // tn),
        in_specs=[
            pl.BlockSpec((tm, D_MODEL), lambda i, j: (i, 0)),
            pl.BlockSpec((D_MODEL, tn), lambda i, j: (0, j)),
            pl.BlockSpec((tm, tn), lambda i, j: (i, j)),
        ],
        out_specs=pl.BlockSpec((tm, tn), lambda i, j: (i, j)),
        out_shape=jax.ShapeDtypeStruct((M, D_MODEL), F32),
        compiler_params=_cparams("parallel", "arbitrary"),
        name="outproj",
    )(merged, w_out, x)


def _swap_halves(w):
    half = QK_ROPE // 2
    return jnp.concatenate([w[..., half:], w[..., :half]], axis=-1)


def _rope_table(pos):
    half = QK_ROPE // 2
    inv = jnp.power(ROPE_THETA, -jnp.arange(half, dtype=F32) / half)
    ang = pos.astype(F32)[:, None] * inv[None, :]
    cos, sin = jnp.cos(ang), jnp.sin(ang)
    return jnp.concatenate([cos, cos, -sin, sin], axis=-1)


def kernel(x_prompt, x_sample, cache_kv_latent, cache_k_rope, state_ssm, state_conv, norm_ffn1, w_ffn1_gate, w_ffn1_up, w_ffn1_down, norm_mix, w_in, b_gate, norm_q_a, w_q_b, norm_kv_a, w_kv_b, conv_w, conv_b, dt_bias, a_log, d_skip, norm_ssm, w_o_attn, w_o_ssm, w_out, norm_ffn2, w_ffn2_gate, w_ffn2_up, w_ffn2_down, norm_final):
    Bp, S, D = x_prompt.shape
    Bs, T, _ = x_sample.shape
    past = cache_kv_latent.shape[2]
    H = N_HEADS_A
    l = 0

    w_xs, w_zg = _relayout_w_in(w_in[l])
    wq3 = w_q_b[l].reshape(Q_LORA, H, QK_NOPE + QK_ROPE)
    wq = jnp.concatenate([wq3, _swap_halves(wq3[..., QK_NOPE:])], axis=-1).reshape(Q_LORA, H * QK_HEAD).astype(BF16)
    wkv3 = w_kv_b[l].reshape(KV_LORA, H, QK_NOPE + V_DIM)
    w_uk = wkv3[..., :QK_NOPE]
    w_uv = wkv3[..., QK_NOPE:]
    w_uk_flat = w_uk.reshape(KV_LORA, H * QK_NOPE).astype(BF16)
    w_uv_flat = w_uv.reshape(KV_LORA, H * V_DIM).astype(BF16)
    w_uk_t = w_uk.transpose(1, 2, 0).astype(BF16)
    w_uv_h = w_uv.transpose(1, 0, 2).astype(BF16)
    wg1, wu1, wd1 = w_ffn1_gate[l].astype(BF16), w_ffn1_up[l].astype(BF16), w_ffn1_down[l].astype(BF16)
    wg2, wu2, wd2 = w_ffn2_gate[l].astype(BF16), w_ffn2_up[l].astype(BF16), w_ffn2_down[l].astype(BF16)
    woa, wos, wo = w_o_attn[l].astype(BF16), w_o_ssm[l].astype(BF16), w_out[l].astype(BF16)

    def mixer_tail(x1, zg, o_attn, y_ssm):
        merged = _merge(o_attn, y_ssm, woa, wos, zg, b_gate[l], tm=1024, tn=512)
        x2 = _outproj(merged, wo, x1, tm=1024, tn=1024)
        return _ffn(x2, norm_ffn2[l], wg2, wu2, wd2, norm_final)

    def mixer_head(x, batch):
        x1 = _ffn(x, norm_ffn1[l], wg1, wu1, wd1)
        xs = _inproj(x1, norm_mix[l], w_xs, tm=1024, tn=SMALL_W)
        zg = _inproj(x1, norm_mix[l], w_zg, tm=1024, tn=1024)
        xs3 = xs.reshape(batch, -1, XS_W)
        return x1, xs, zg, xs3[:, xs3.shape[1] - (CONV_W - 1):, :CONV_DIM]

    def ssd(xs, zg, batch, conv_state, state, q_len):
        y, new_state = _ssd(xs.reshape(batch, -1, XS_W), zg.reshape(batch, -1, ZG_W), conv_state,
                            conv_w[l], conv_b[l], dt_bias[l], a_log[l], d_skip[l], norm_ssm[l], state, q_len=q_len)
        return y.reshape(-1, D_INNER), new_state

    x1, xs, zg, p_conv = mixer_head(x_prompt.reshape(Bp * S, D), Bp)
    cs_p = _rope_table(jnp.arange(S))
    p_lat, p_kr, q, k, v = _prep(xs, cs_p, norm_q_a[l], norm_kv_a[l], wq, (w_uk_flat, w_uv_flat), prompt=True)
    o_attn = _attn_prompt(q.reshape(Bp, S, -1), k.reshape(Bp, S, -1), v.reshape(Bp, S, -1)).reshape(Bp * S, -1)
    y_ssm, p_ssm = ssd(xs, zg, Bp, jnp.zeros((Bp, CONV_W - 1, CONV_DIM), F32),
                       jnp.zeros((Bp, N_HEADS_S, HEAD_DIM_S, D_STATE), F32), 128)
    y_prompt = mixer_tail(x1, zg, o_attn, y_ssm).reshape(Bp, S, D)

    x1s, xss, zgs, s_conv = mixer_head(x_sample.reshape(Bs * T, D), Bs)
    cs_s = jnp.tile(_rope_table(past + jnp.arange(T)), (Bs, 1))
    s_lat, s_kr, qlat, qrot = _prep(xss, cs_s, norm_q_a[l], norm_kv_a[l], wq, (w_uk_t,), prompt=False)
    o_lat = _attn_sample(qlat, qrot, cache_kv_latent[l], jnp.swapaxes(cache_k_rope[l], 1, 2),
                         s_lat.reshape(Bs, T, KV_LORA), s_kr.reshape(Bs, T, QK_ROPE))
    o_attn_s = _uv_proj(o_lat, w_uv_h)
    y_ssm_s, s_ssm = ssd(xss, zgs, Bs, state_conv[l], state_ssm[l], T)
    y_sample = mixer_tail(x1s, zgs, o_attn_s, y_ssm_s).reshape(Bs, T, D)

    return (y_prompt, y_sample,
            p_lat.reshape(1, Bp, S, KV_LORA), p_kr.reshape(1, Bp, S, QK_ROPE), p_ssm[None], p_conv[None],
            s_lat.reshape(1, Bs, T, KV_LORA), s_kr.reshape(1, Bs, T, QK_ROPE), s_ssm[None], s_conv[None])
```

```python
import functools
import math

import jax
import jax.numpy as jnp
import numpy as np
from jax import lax
from jax.experimental import pallas as pl
from jax.experimental.pallas import tpu as pltpu

F32 = jnp.float32
BF16 = jnp.bfloat16

D_MODEL = 2048
CHUNK = 64
N_HEADS_A = 16
Q_LORA = 768
KV_LORA = 512
QK_NOPE = 128
QK_ROPE = 64
V_DIM = 128
ROPE_THETA = 10000.0
ATTN_SCALE = (QK_NOPE + QK_ROPE) ** -0.5
D_INNER = 2 * D_MODEL
HEAD_DIM_S = 64
N_HEADS_S = D_INNER // HEAD_DIM_S
N_GROUPS_S = 8
HEADS_PER_GROUP = N_HEADS_S // N_GROUPS_S
GROUP_W = D_INNER // N_GROUPS_S
D_STATE = 128
CONV_W = 4
CONV_DIM = D_INNER + 2 * N_GROUPS_S * D_STATE
D_FF = 5632
EPS = 1e-6

QK_HEAD = 256
COL_QA = 0
COL_LAT = Q_LORA
COL_KR = COL_LAT + KV_LORA
COL_DT = COL_KR + 2 * QK_ROPE
SMALL_W = COL_DT + 128
XS_W = CONV_DIM + SMALL_W
BC_W = N_GROUPS_S * D_STATE
COL_X = 0
COL_B = D_INNER
COL_C = COL_B + BC_W
COL_Z = 0
COL_GATE = D_INNER
ZG_W = COL_GATE + 2 * D_MODEL
LOG2E = math.log2(math.e)

VMEM_LIMIT = 60 * 1024 * 1024


def _cparams(*sem):
    return pltpu.CompilerParams(dimension_semantics=sem, vmem_limit_bytes=VMEM_LIMIT)


def _rms(x, g):
    return x * lax.rsqrt(jnp.mean(x * x, axis=-1, keepdims=True) + EPS) * g


def _silu(x):
    h = 0.5 * x
    return h + h * jnp.tanh(h)


def _dot(a, b):
    return jnp.dot(a, b, preferred_element_type=F32)


def _dot_nt(a, b):
    return lax.dot_general(a, b, (((1,), (1,)), ((), ())), preferred_element_type=F32)


def _dot_tn(a, b):
    return lax.dot_general(a, b, (((0,), (0,)), ((), ())), preferred_element_type=F32)


def _ffn_kernel(*refs, final_norm):
    if final_norm:
        x_ref, g_ref, wg_ref, wu_ref, wd_ref, gf_ref, o_ref, h_scr = refs
    else:
        x_ref, g_ref, wg_ref, wu_ref, wd_ref, o_ref, h_scr = refs
    j = pl.program_id(1)

    @pl.when(j == 0)
    def _():
        x = x_ref[...]
        h_scr[...] = _rms(x, g_ref[...]).astype(BF16)
        o_ref[...] = x

    h = h_scr[...]
    a = (0.5 * _silu(_dot(h, wg_ref[...]))) * _dot(h, wu_ref[...])
    o_ref[...] += _dot(a.astype(BF16), wd_ref[...])

    if final_norm:
        @pl.when(j == pl.num_programs(1) - 1)
        def _():
            o_ref[...] = _rms(o_ref[...], gf_ref[...])


def _ffn(x, g, wg, wu, wd, gf=None, *, tm=1024, tf=512):
    M, D = x.shape
    F = wg.shape[1]
    tm = min(tm, M)
    in_specs = [
        pl.BlockSpec((tm, D), lambda i, j: (i, 0)),
        pl.BlockSpec((1, D), lambda i, j: (0, 0)),
        pl.BlockSpec((D, tf), lambda i, j: (0, j)),
        pl.BlockSpec((D, tf), lambda i, j: (0, j)),
        pl.BlockSpec((tf, D), lambda i, j: (j, 0)),
    ]
    args = [x, g.reshape(1, D), wg, wu, wd]
    if gf is not None:
        in_specs.append(pl.BlockSpec((1, D), lambda i, j: (0, 0)))
        args.append(gf.reshape(1, D))
    return pl.pallas_call(
        functools.partial(_ffn_kernel, final_norm=gf is not None),
        grid=(M // tm, F // tf),
        in_specs=in_specs,
        out_specs=pl.BlockSpec((tm, D), lambda i, j: (i, 0)),
        out_shape=jax.ShapeDtypeStruct((M, D), F32),
        scratch_shapes=[pltpu.VMEM((tm, D), BF16)],
        compiler_params=_cparams("parallel", "arbitrary"),
        name="ffn",
    )(*args)


def _relayout_kernel(w_ref, xs_ref, zg_ref):
    w = w_ref[...]
    o_lat = Q_LORA
    o_kr = o_lat + KV_LORA
    o_z = o_kr + QK_ROPE
    o_x = o_z + D_INNER
    o_bc = o_x + D_INNER
    o_dt = o_bc + 2 * BC_W
    o_gate = o_dt + N_HEADS_S
    half = QK_ROPE // 2
    small = jnp.concatenate(
        [w[:, :o_z], w[:, o_kr + half:o_z], w[:, o_kr:o_kr + half], w[:, o_dt:o_gate],
         jnp.zeros((w.shape[0], 128 - N_HEADS_S), F32)], axis=1)
    xs_ref[:, :CONV_DIM] = w[:, o_x:o_dt].astype(BF16)
    xs_ref[:, CONV_DIM:] = small.astype(BF16)
    zg_ref[:, COL_Z:COL_Z + D_INNER] = w[:, o_z:o_x].astype(BF16)
    zg_ref[:, COL_GATE:] = w[:, o_gate:].astype(BF16)


def _relayout_w_in(w, *, tk=128):
    K, N = w.shape
    widths = (XS_W, ZG_W)
    return pl.pallas_call(
        _relayout_kernel,
        grid=(K // tk,),
        in_specs=[pl.BlockSpec((tk, N), lambda i: (i, 0))],
        out_specs=[pl.BlockSpec((tk, n), lambda i: (i, 0)) for n in widths],
        out_shape=[jax.ShapeDtypeStruct((K, n), BF16) for n in widths],
        compiler_params=_cparams("parallel"),
        name="relayout_w_in",
    )(w)


HIST = 8


def _causal_conv_silu(hist, cur, w, bias):
    ext = jnp.concatenate([hist, cur], axis=0)
    acc = bias
    for k in range(CONV_W - 1):
        acc = acc + w[k:k + 1, :] * pltpu.roll(ext, CONV_W - 1 - k, axis=0)[HIST:, :]
    return _silu(acc + w[CONV_W - 1:CONV_W, :] * cur)


def _inproj_kernel(x_ref, g_ref, w_ref, o_ref, h_scr):
    @pl.when(pl.program_id(1) == 0)
    def _():
        h_scr[...] = _rms(x_ref[...], g_ref[...]).astype(BF16)

    o_ref[...] = _dot(h_scr[...], w_ref[...])


def _inproj(x, g, w, *, tm, tn):
    M, D = x.shape
    N = w.shape[1]
    tm = min(tm, M)
    return pl.pallas_call(
        _inproj_kernel,
        grid=(M // tm, N // tn),
        in_specs=[
            pl.BlockSpec((tm, D), lambda i, j: (i, 0)),
            pl.BlockSpec((1, D), lambda i, j: (0, 0)),
            pl.BlockSpec((D, tn), lambda i, j: (0, j)),
        ],
        out_specs=pl.BlockSpec((tm, tn), lambda i, j: (i, j)),
        out_shape=jax.ShapeDtypeStruct((M, N), F32),
        scratch_shapes=[pltpu.VMEM((tm, D), BF16)],
        compiler_params=_cparams("parallel", "arbitrary"),
        name="inproj",
    )(x, g.reshape(1, D), w)


def _rope_pair(blk, cs, keep):
    prod = blk * cs
    return jnp.where(keep, prod + pltpu.roll(prod, 64, axis=1), 0.0)


def _prep_common(small_ref, cs_ref, nq_ref, nkv_ref, wq_ref, ckv_ref, kr_ref):
    sm = small_ref[...]
    cs = cs_ref[...]
    keep = lax.broadcasted_iota(jnp.int32, cs.shape, 1) < QK_ROPE
    cq = _rms(sm[:, COL_QA:COL_QA + Q_LORA], nq_ref[...]).astype(BF16)
    q = _dot(cq, wq_ref[...]) * (ATTN_SCALE * LOG2E)
    ckv = _rms(sm[:, COL_LAT:COL_LAT + KV_LORA], nkv_ref[...])
    ckv_ref[...] = ckv
    krot = _rope_pair(sm[:, COL_KR:COL_KR + 128], cs, keep)
    kr_ref[...] = krot[:, :QK_ROPE]
    return q, ckv, krot, cs, keep


def _prep_prompt_kernel(small_ref, cs_ref, nq_ref, nkv_ref, wq_ref, wuk_ref, wuv_ref,
                        ckv_ref, kr_ref, q_ref, k_ref, v_ref):
    q, ckv, krot, cs, keep = _prep_common(small_ref, cs_ref, nq_ref, nkv_ref, wq_ref, ckv_ref, kr_ref)
    ckv_b = ckv.astype(BF16)
    kn = _dot(ckv_b, wuk_ref[...])
    krot_b = krot.astype(BF16)
    for h in range(N_HEADS_A):
        c0 = h * QK_HEAD
        q_ref[:, c0:c0 + 128] = q[:, c0:c0 + 128].astype(BF16)
        q_ref[:, c0 + 128:c0 + 256] = _rope_pair(q[:, c0 + 128:c0 + 256], cs, keep).astype(BF16)
        k_ref[:, c0:c0 + 128] = kn[:, h * 128:(h + 1) * 128].astype(BF16)
        k_ref[:, c0 + 128:c0 + 256] = krot_b
    v_ref[...] = _dot(ckv_b, wuv_ref[...]).astype(BF16)


def _prep_sample_kernel(small_ref, cs_ref, nq_ref, nkv_ref, wq_ref, wukt_ref,
                        ckv_ref, kr_ref, qlat_ref, qrot_ref):
    q, _, _, cs, keep = _prep_common(small_ref, cs_ref, nq_ref, nkv_ref, wq_ref, ckv_ref, kr_ref)
    for h in range(N_HEADS_A):
        c0 = h * QK_HEAD
        qn = q[:, c0:c0 + 128].astype(BF16)
        qlat_ref[:, h * KV_LORA:(h + 1) * KV_LORA] = _dot(qn, wukt_ref[h]).astype(BF16)
        qrot_ref[:, h * 128:(h + 1) * 128] = _rope_pair(q[:, c0 + 128:c0 + 256], cs, keep).astype(BF16)


def _prep(proj, cs_table, nq, nkv, wq, extra_w, *, prompt, tm=512):
    M = proj.shape[0]
    tm = min(tm, M)
    n_cs = cs_table.shape[0] // tm
    H = N_HEADS_A
    row = lambda i: (i, 0)
    const2 = lambda i: (0, 0)
    in_specs = [
        pl.BlockSpec((tm, SMALL_W), lambda i: (i, CONV_DIM // SMALL_W)),
        pl.BlockSpec((tm, 128), lambda i: (i % n_cs, 0)),
        pl.BlockSpec((1, Q_LORA), const2),
        pl.BlockSpec((1, KV_LORA), const2),
        pl.BlockSpec((Q_LORA, H * QK_HEAD), const2),
    ]
    out_specs = [pl.BlockSpec((tm, KV_LORA), row), pl.BlockSpec((tm, QK_ROPE), row)]
    out_shape = [jax.ShapeDtypeStruct((M, KV_LORA), F32), jax.ShapeDtypeStruct((M, QK_ROPE), F32)]
    if prompt:
        w_uk, w_uv = extra_w
        in_specs += [pl.BlockSpec((KV_LORA, H * QK_NOPE), const2), pl.BlockSpec((KV_LORA, H * V_DIM), const2)]
        out_specs += [pl.BlockSpec((tm, H * QK_HEAD), row), pl.BlockSpec((tm, H * QK_HEAD), row),
                      pl.BlockSpec((tm, H * V_DIM), row)]
        out_shape += [jax.ShapeDtypeStruct((M, H * QK_HEAD), BF16), jax.ShapeDtypeStruct((M, H * QK_HEAD), BF16),
                      jax.ShapeDtypeStruct((M, H * V_DIM), BF16)]
        body, extra = _prep_prompt_kernel, [w_uk, w_uv]
    else:
        (w_ukt,) = extra_w
        in_specs += [pl.BlockSpec((H, QK_NOPE, KV_LORA), lambda i: (0, 0, 0))]
        out_specs += [pl.BlockSpec((tm, H * KV_LORA), row), pl.BlockSpec((tm, H * 128), row)]
        out_shape += [jax.ShapeDtypeStruct((M, H * KV_LORA), BF16), jax.ShapeDtypeStruct((M, H * 128), BF16)]
        body, extra = _prep_sample_kernel, [w_ukt]
    return pl.pallas_call(
        body,
        grid=(M // tm,),
        in_specs=in_specs,
        out_specs=out_specs,
        out_shape=out_shape,
        compiler_params=_cparams("parallel"),
        name="prep_prompt" if prompt else "prep_sample",
    )(proj, cs_table, nq.reshape(1, -1), nkv.reshape(1, -1), wq, *extra)


def _attn_prompt_kernel(q_ref, k_ref, v_ref, bias_ref, o_ref, s_scr, p_scr, *, seq, tq):
    for qi in reversed(range(seq // tq)):
        kv = (qi + 1) * tq
        s_buf = s_scr.at[qi % 2]
        p_buf = p_scr.at[qi % 2]
        q = q_ref[qi * tq:kv, :]
        s_buf[:, :kv] = _dot_nt(q, k_ref[:kv, :])
        s_buf[:, kv - tq:kv] = s_buf[:, kv - tq:kv] + bias_ref[...]
        m = jnp.max(s_buf[:, :kv], axis=-1, keepdims=True)
        p = jnp.exp2(s_buf[:, :kv] - m)
        l = jnp.sum(p, axis=-1, keepdims=True)
        p_buf[:, :kv] = p.astype(BF16)
        acc = _dot(p_buf[:, :kv], v_ref[:kv, :])
        o_ref[qi * tq:kv, :] = (acc / l).astype(o_ref.dtype)


def _attn_prompt(q, k, v, *, tq=512):
    B, S, _ = q.shape
    H = N_HEADS_A
    chunk = np.arange(tq) // CHUNK
    bias = jnp.asarray(np.where(chunk[None, :] <= chunk[:, None], 0.0, -np.inf).astype(np.float32))
    return pl.pallas_call(
        functools.partial(_attn_prompt_kernel, seq=S, tq=tq),
        grid=(B, H),
        in_specs=[
            pl.BlockSpec((None, S, QK_HEAD), lambda b, h: (b, 0, h)),
            pl.BlockSpec((None, S, QK_HEAD), lambda b, h: (b, 0, h)),
            pl.BlockSpec((None, S, V_DIM), lambda b, h: (b, 0, h)),
            pl.BlockSpec((tq, tq), lambda b, h: (0, 0)),
        ],
        out_specs=pl.BlockSpec((None, S, V_DIM), lambda b, h: (b, 0, h)),
        out_shape=jax.ShapeDtypeStruct((B, S, H * V_DIM), BF16),
        scratch_shapes=[pltpu.VMEM((2, tq, S), F32), pltpu.VMEM((2, tq, S), BF16)],
        compiler_params=_cparams("parallel", "parallel"),
        name="attn_prompt",
    )(q, k, v, bias)


def _attn_sample_kernel(qlat_ref, qrot_ref, clat_ref, ckrt_ref, nlat_ref, nkr_ref, o_ref):
    H = N_HEADS_A
    T = nlat_ref.shape[1]
    for sb in range(clat_ref.shape[0]):
        rows = slice(sb * T, (sb + 1) * T)
        ql = jnp.concatenate([qlat_ref[rows, h * KV_LORA:(h + 1) * KV_LORA] for h in range(H)], axis=0)
        qr = jnp.concatenate([qrot_ref[rows, h * 128:h * 128 + QK_ROPE] for h in range(H)], axis=0)
        lat = clat_ref[sb].astype(BF16)
        nlat = nlat_ref[sb].astype(BF16)
        s1 = _dot_nt(ql, lat) + _dot(qr, ckrt_ref[sb].astype(BF16))
        s2 = _dot_nt(ql, nlat) + _dot_nt(qr, nkr_ref[sb].astype(BF16))
        m = jnp.maximum(jnp.max(s1, axis=-1, keepdims=True), jnp.max(s2, axis=-1, keepdims=True))
        p1 = jnp.exp2(s1 - m)
        p2 = jnp.exp2(s2 - m)
        l = jnp.sum(p1, axis=-1, keepdims=True) + jnp.sum(p2, axis=-1, keepdims=True)
        o = _dot(p1.astype(BF16), lat) + _dot(p2.astype(BF16), nlat)
        o = (o / l).astype(o_ref.dtype)
        for h in range(H):
            o_ref[rows, h * KV_LORA:(h + 1) * KV_LORA] = o[h * T:(h + 1) * T, :]


def _attn_sample(qlat, qrot, cache_lat, cache_kr_t, new_lat, new_kr, *, streams):
    Bs, P, _ = cache_lat.shape
    T = new_lat.shape[1]
    H = N_HEADS_A
    ns = streams
    b2 = lambda b: (b, 0)
    b3 = lambda b: (b, 0, 0)
    return pl.pallas_call(
        _attn_sample_kernel,
        grid=(Bs // ns,),
        in_specs=[
            pl.BlockSpec((ns * T, H * KV_LORA), b2),
            pl.BlockSpec((ns * T, H * 128), b2),
            pl.BlockSpec((ns, P, KV_LORA), b3),
            pl.BlockSpec((ns, QK_ROPE, P), b3),
            pl.BlockSpec((ns, T, KV_LORA), b3),
            pl.BlockSpec((ns, T, QK_ROPE), b3),
        ],
        out_specs=pl.BlockSpec((ns * T, H * KV_LORA), b2),
        out_shape=jax.ShapeDtypeStruct((Bs * T, H * KV_LORA), BF16),
        compiler_params=_cparams("parallel"),
        name="attn_sample",
    )(qlat, qrot, cache_lat, cache_kr_t, new_lat, new_kr)


def _uv_kernel(x_ref, w_ref, o_ref):
    o_ref[...] = _dot(x_ref[...], w_ref[...]).astype(o_ref.dtype)


def _uv_proj(olat, w_uv_heads):
    M = olat.shape[0]
    H = N_HEADS_A
    return pl.pallas_call(
        _uv_kernel,
        grid=(H,),
        in_specs=[pl.BlockSpec((M, KV_LORA), lambda h: (0, h)),
                  pl.BlockSpec((None, KV_LORA, V_DIM), lambda h: (h, 0, 0))],
        out_specs=pl.BlockSpec((M, V_DIM), lambda h: (0, h)),
        out_shape=jax.ShapeDtypeStruct((M, H * V_DIM), BF16),
        compiler_params=_cparams("parallel"),
        name="uv_proj",
    )(olat, w_uv_heads)


def _split3(v):
    hi = v.astype(BF16)
    r1 = v - hi.astype(F32)
    mid = r1.astype(BF16)
    lo = (r1 - mid.astype(F32)).astype(BF16)
    return jnp.concatenate([hi, mid, lo], axis=1)


def _ssd_kernel(x_ref, z_ref, b_ref, c_ref, dt_ref, cst_ref, cw_ref, cbias_ref, dtb_ref, alog_ref,
                dskx_ref, nssm_ref, e64_ref, e128_ref, s0_ref,
                y_ref, sout_ref, xbuf, bbuf, cbuf, xs_scr, bm_scr, cm_scr, s_scr, *, q_len, n_chunks):
    c = pl.program_id(1)
    Q = q_len
    P = HEAD_DIM_S
    G, R, N, W = N_GROUPS_S, HEADS_PER_GROUP, D_STATE, GROUP_W
    h0 = HIST - (CONV_W - 1)

    @pl.when(c == 0)
    def _():
        for buf, lo, hi in ((xbuf, 0, D_INNER), (bbuf, D_INNER, D_INNER + BC_W), (cbuf, D_INNER + BC_W, CONV_DIM)):
            buf[0:h0, :] = jnp.zeros((h0, hi - lo), F32)
            buf[h0:HIST, :] = cst_ref[:, lo:hi]
        s_scr[...] = s0_ref[...]

    def conv(hist, cur_ref, lo, hi):
        cur = cur_ref[...]
        out = _causal_conv_silu(hist[...], cur, cw_ref[:, lo:hi], cbias_ref[:, lo:hi])
        hist[...] = cur[Q - HIST:, :]
        return out

    xs_scr[...] = conv(xbuf, x_ref, 0, D_INNER)
    bm_scr[...] = conv(bbuf, b_ref, D_INNER, D_INNER + BC_W).astype(BF16)
    cm_scr[...] = conv(cbuf, c_ref, D_INNER + BC_W, CONV_DIM).astype(BF16)

    dt_all = jax.nn.softplus(dt_ref[...] + dtb_ref[...])
    a_all = dt_all * (-LOG2E * jnp.exp(alog_ref[...]))
    ri = lax.broadcasted_iota(jnp.int32, (Q, Q), 0)
    ci = lax.broadcasted_iota(jnp.int32, (Q, Q), 1)
    causal = ci <= ri
    hp = lax.Precision.HIGHEST
    acs_all = jnp.dot(causal.astype(F32), a_all, precision=hp, preferred_element_type=F32)
    acs_row = lax.dot_general(a_all, (ri <= ci).astype(F32), (((0,), (0,)), ((), ())),
                              precision=hp, preferred_element_type=F32)
    a_end = acs_all[Q - 1:Q, :]
    eacs_all = jnp.exp2(acs_all)
    w_all = dt_all * jnp.exp2(a_end - acs_all)
    dec_all = jnp.exp2(a_end)
    left = lax.broadcasted_iota(jnp.int32, (Q, 128), 1) < P
    e64 = e64_ref[...]
    e128 = e128_ref[...]

    for g in range(G):
        h_lo = g * R
        dtx = _dot(_split3(dt_all[:, h_lo:h_lo + R]), e64)
        eacsx = _dot(_split3(eacs_all[:, h_lo:h_lo + R]), e64)
        wx = _dot(_split3(w_all[:, h_lo:h_lo + R]), e64)
        acsb = _dot(_split3(acs_all[:, h_lo:h_lo + R]), e128)
        xs = xs_scr[:, g * W:(g + 1) * W]
        bm = bm_scr[:, g * N:(g + 1) * N]
        cm = cm_scr[:, g * N:(g + 1) * N]
        xdt = xs * dtx
        cb = _dot_nt(cm, bm)
        y_off = _dot_nt(cm, s_scr[g].astype(BF16))
        ys = []
        for j in range(R // 2):
            ms = []
            for r in (2 * j, 2 * j + 1):
                seg = acsb[:, r * 128:r * 128 + Q] - acs_row[h_lo + r:h_lo + r + 1, :]
                ms.append((cb * jnp.exp2(jnp.where(causal, seg, -jnp.inf))).astype(BF16))
            xp = xdt[:, j * 128:(j + 1) * 128]
            rhs = jnp.concatenate([jnp.where(left, xp, 0.0), jnp.where(left, 0.0, xp)], axis=0).astype(BF16)
            ys.append(_dot(jnp.concatenate(ms, axis=1), rhs))
        y = jnp.concatenate(ys, axis=1) + y_off * eacsx + dskx_ref[:, g * W:(g + 1) * W] * xs

        st = _dot_tn((xs * wx).astype(BF16), bm)
        for r in range(R):
            dec = dec_all[:, h_lo + r:h_lo + r + 1]
            s_scr[g, r * P:(r + 1) * P, :] = dec * s_scr[g, r * P:(r + 1) * P, :] + st[r * P:(r + 1) * P, :]

        v = y * _silu(z_ref[:, g * W:(g + 1) * W])
        y_ref[:, g * W:(g + 1) * W] = _rms(v, nssm_ref[:, g * W:(g + 1) * W]).astype(y_ref.dtype)

    @pl.when(c == n_chunks - 1)
    def _():
        sout_ref[...] = s_scr[...]


def _head_expander(width):
    R = HEADS_PER_GROUP
    row_head = np.arange(3 * R) % R
    col_head = np.arange(R * width) // width
    return jnp.asarray((row_head[:, None] == col_head[None, :]).astype(np.float32)).astype(BF16)


def _ssd(xs3, zg3, conv_state, conv_w, conv_b, dt_bias, a_log, d_skip, n_ssm, state0, *, q_len):
    B, L, _ = xs3.shape
    G, N = N_GROUPS_S, D_STATE
    Q = q_len
    nc = L // Q
    pad = 128 - N_HEADS_S
    const = lambda b, c: (0, 0)
    in_specs = [
        pl.BlockSpec((None, Q, D_INNER), lambda b, c: (b, c, COL_X // D_INNER)),
        pl.BlockSpec((None, Q, D_INNER), lambda b, c: (b, c, COL_Z // D_INNER)),
        pl.BlockSpec((None, Q, BC_W), lambda b, c: (b, c, COL_B // BC_W)),
        pl.BlockSpec((None, Q, BC_W), lambda b, c: (b, c, COL_C // BC_W)),
        pl.BlockSpec((None, Q, 128), lambda b, c: (b, c, (CONV_DIM + COL_DT) // 128)),
        pl.BlockSpec((None, CONV_W - 1, CONV_DIM), lambda b, c: (b, 0, 0)),
        pl.BlockSpec((CONV_W, CONV_DIM), const),
        pl.BlockSpec((1, CONV_DIM), const),
        pl.BlockSpec((1, 128), const),
        pl.BlockSpec((1, 128), const),
        pl.BlockSpec((1, D_INNER), const),
        pl.BlockSpec((1, D_INNER), const),
        pl.BlockSpec((3 * HEADS_PER_GROUP, GROUP_W), const),
        pl.BlockSpec((3 * HEADS_PER_GROUP, HEADS_PER_GROUP * 128), const),
        pl.BlockSpec((None, G, GROUP_W, N), lambda b, c: (b, 0, 0, 0)),
    ]
    args = [xs3, zg3, xs3, xs3, xs3, conv_state, conv_w, conv_b.reshape(1, CONV_DIM),
            jnp.pad(dt_bias, (0, pad)).reshape(1, 128), jnp.pad(a_log, (0, pad)).reshape(1, 128),
            jnp.repeat(d_skip, HEAD_DIM_S).reshape(1, D_INNER), n_ssm.reshape(1, D_INNER),
            _head_expander(HEAD_DIM_S), _head_expander(128), state0.reshape(B, G, GROUP_W, N)]
    y, s_out = pl.pallas_call(
        functools.partial(_ssd_kernel, q_len=Q, n_chunks=nc),
        grid=(B, nc),
        in_specs=in_specs,
        out_specs=[pl.BlockSpec((None, Q, D_INNER), lambda b, c: (b, c, 0)),
                   pl.BlockSpec((None, G, GROUP_W, N), lambda b, c: (b, 0, 0, 0))],
        out_shape=[jax.ShapeDtypeStruct((B, L, D_INNER), BF16),
                   jax.ShapeDtypeStruct((B, G, GROUP_W, N), F32)],
        scratch_shapes=[pltpu.VMEM((HIST, D_INNER), F32), pltpu.VMEM((HIST, BC_W), F32),
                        pltpu.VMEM((HIST, BC_W), F32), pltpu.VMEM((Q, D_INNER), F32),
                        pltpu.VMEM((Q, BC_W), BF16), pltpu.VMEM((Q, BC_W), BF16),
                        pltpu.VMEM((G, GROUP_W, N), F32)],
        compiler_params=_cparams("parallel", "arbitrary"),
        name="ssd",
    )(*args)
    return y, s_out.reshape(B, N_HEADS_S, HEAD_DIM_S, N)


def _merge_kernel(oa_ref, ys_ref, wa_ref, ws_ref, g0_ref, g1_ref, bg_ref, o_ref):
    attn = _dot(oa_ref[...], wa_ref[...])
    ssm = _dot(ys_ref[...], ws_ref[...])
    bg = bg_ref[...]
    g0 = jax.nn.sigmoid(g0_ref[...] + bg[0:1, :])
    g1 = jax.nn.sigmoid(g1_ref[...] + bg[1:2, :])
    o_ref[...] = (g0 * attn + g1 * ssm).astype(o_ref.dtype)


def _merge(o_attn, y_ssm, w_o_attn, w_o_ssm, proj, b_gate, *, tm=512, tn=512):
    M = o_attn.shape[0]
    tm = min(tm, M)
    g0blk = COL_GATE // tn
    g1blk = (COL_GATE + D_MODEL) // tn
    return pl.pallas_call(
        _merge_kernel,
        grid=(M // tm, D_MODEL // tn),
        in_specs=[
            pl.BlockSpec((tm, o_attn.shape[1]), lambda i, j: (i, 0)),
            pl.BlockSpec((tm, D_INNER), lambda i, j: (i, 0)),
            pl.BlockSpec((w_o_attn.shape[0], tn), lambda i, j: (0, j)),
            pl.BlockSpec((D_INNER, tn), lambda i, j: (0, j)),
            pl.BlockSpec((tm, tn), lambda i, j: (i, g0blk + j)),
            pl.BlockSpec((tm, tn), lambda i, j: (i, g1blk + j)),
            pl.BlockSpec((2, tn), lambda i, j: (0, j)),
        ],
        out_specs=pl.BlockSpec((tm, tn), lambda i, j: (i, j)),
        out_shape=jax.ShapeDtypeStruct((M, D_MODEL), BF16),
        compiler_params=_cparams("parallel", "arbitrary"),
        name="merge",
    )(o_attn, y_ssm, w_o_attn, w_o_ssm, proj, proj, b_gate)


def _outproj_kernel(m_ref, w_ref, x_ref, o_ref):
    o_ref[...] = x_ref[...] + _dot(m_ref[...], w_ref[...])


def _outproj(merged, w_out, x, *, tm=512, tn=512):
    M = x.shape[0]
    tm = min(tm, M)
    return pl.pallas_call(
        _outproj_kernel,
        grid=(M // tm, D_MODEL // tn),
        in_specs=[
            pl.BlockSpec((tm, D_MODEL), lambda i, j: (i, 0)),
            pl.BlockSpec((D_MODEL, tn), lambda i, j: (0, j)),
            pl.BlockSpec((tm, tn), lambda i, j: (i, j)),
        ],
        out_specs=pl.BlockSpec((tm, tn), lambda i, j: (i, j)),
        out_shape=jax.ShapeDtypeStruct((M, D_MODEL), F32),
        compiler_params=_cparams("parallel", "arbitrary"),
        name="outproj",
    )(merged, w_out, x)


def _swap_halves(w):
    half = QK_ROPE // 2
    return jnp.concatenate([w[..., half:], w[..., :half]], axis=-1)


def _rope_table(pos):
    half = QK_ROPE // 2
    inv = np.power(ROPE_THETA, -np.arange(half, dtype=np.float64) / half)
    ang = np.asarray(pos, np.float64)[:, None] * inv[None, :]
    cos, sin = np.cos(ang), np.sin(ang)
    return np.concatenate([cos, cos, -sin, sin], axis=-1).astype(np.float32)


def kernel(x_prompt, x_sample, cache_kv_latent, cache_k_rope, state_ssm, state_conv, norm_ffn1, w_ffn1_gate, w_ffn1_up, w_ffn1_down, norm_mix, w_in, b_gate, norm_q_a, w_q_b, norm_kv_a, w_kv_b, conv_w, conv_b, dt_bias, a_log, d_skip, norm_ssm, w_o_attn, w_o_ssm, w_out, norm_ffn2, w_ffn2_gate, w_ffn2_up, w_ffn2_down, norm_final):
    Bp, S, D = x_prompt.shape
    Bs, T, _ = x_sample.shape
    past = cache_kv_latent.shape[2]
    H = N_HEADS_A
    l = 0

    w_xs, w_zg = _relayout_w_in(w_in[l])
    wq3 = w_q_b[l].reshape(Q_LORA, H, QK_NOPE + QK_ROPE)
    wq = jnp.concatenate([wq3, _swap_halves(wq3[..., QK_NOPE:])], axis=-1).reshape(Q_LORA, H * QK_HEAD).astype(BF16)
    wkv3 = w_kv_b[l].reshape(KV_LORA, H, QK_NOPE + V_DIM)
    w_uk = wkv3[..., :QK_NOPE]
    w_uv = wkv3[..., QK_NOPE:]
    w_uk_flat = w_uk.reshape(KV_LORA, H * QK_NOPE).astype(BF16)
    w_uv_flat = w_uv.reshape(KV_LORA, H * V_DIM).astype(BF16)
    w_uk_t = w_uk.transpose(1, 2, 0).astype(BF16)
    w_uv_h = w_uv.transpose(1, 0, 2).astype(BF16)
    wg1, wu1, wd1 = w_ffn1_gate[l].astype(BF16), w_ffn1_up[l].astype(BF16), w_ffn1_down[l].astype(BF16)
    wg2, wu2, wd2 = w_ffn2_gate[l].astype(BF16), w_ffn2_up[l].astype(BF16), w_ffn2_down[l].astype(BF16)
    woa, wos, wo = w_o_attn[l].astype(BF16), w_o_ssm[l].astype(BF16), w_out[l].astype(BF16)

    def mixer_tail(x1, zg, o_attn, y_ssm):
        merged = _merge(o_attn, y_ssm, woa, wos, zg, b_gate[l], tm=1024, tn=512)
        x2 = _outproj(merged, wo, x1, tm=1024, tn=1024)
        return _ffn(x2, norm_ffn2[l], wg2, wu2, wd2, norm_final)

    def mixer_head(x, batch):
        x1 = _ffn(x, norm_ffn1[l], wg1, wu1, wd1)
        xs = _inproj(x1, norm_mix[l], w_xs, tm=1024, tn=SMALL_W)
        zg = _inproj(x1, norm_mix[l], w_zg, tm=1024, tn=1024)
        xs3 = xs.reshape(batch, -1, XS_W)
        return x1, xs, zg, xs3[:, xs3.shape[1] - (CONV_W - 1):, :CONV_DIM]

    def ssd(xs, zg, batch, conv_state, state, q_len):
        y, new_state = _ssd(xs.reshape(batch, -1, XS_W), zg.reshape(batch, -1, ZG_W), conv_state,
                            conv_w[l], conv_b[l], dt_bias[l], a_log[l], d_skip[l], norm_ssm[l], state, q_len=q_len)
        return y.reshape(-1, D_INNER), new_state

    x1, xs, zg, p_conv = mixer_head(x_prompt.reshape(Bp * S, D), Bp)
    cs_p = jnp.asarray(_rope_table(np.arange(S)))
    p_lat, p_kr, q, k, v = _prep(xs, cs_p, norm_q_a[l], norm_kv_a[l], wq, (w_uk_flat, w_uv_flat), prompt=True)
    o_attn = _attn_prompt(q.reshape(Bp, S, -1), k.reshape(Bp, S, -1), v.reshape(Bp, S, -1)).reshape(Bp * S, -1)
    y_ssm, p_ssm = ssd(xs, zg, Bp, jnp.zeros((Bp, CONV_W - 1, CONV_DIM), F32),
                       jnp.zeros((Bp, N_HEADS_S, HEAD_DIM_S, D_STATE), F32), 128)
    y_prompt = mixer_tail(x1, zg, o_attn, y_ssm).reshape(Bp, S, D)

    x1s, xss, zgs, s_conv = mixer_head(x_sample.reshape(Bs * T, D), Bs)
    cs_s = jnp.asarray(np.tile(_rope_table(past + np.arange(T)), (Bs, 1)))
    s_lat, s_kr, qlat, qrot = _prep(xss, cs_s, norm_q_a[l], norm_kv_a[l], wq, (w_uk_t,), prompt=False)
    o_lat = _attn_sample(qlat, qrot, cache_kv_latent[l], jnp.swapaxes(cache_k_rope[l], 1, 2),
                         s_lat.reshape(Bs, T, KV_LORA), s_kr.reshape(Bs, T, QK_ROPE),
                         streams=2 if Bs % 2 == 0 else 1)
    o_attn_s = _uv_proj(o_lat, w_uv_h)
    y_ssm_s, s_ssm = ssd(xss, zgs, Bs, state_conv[l], state_ssm[l], T)
    y_sample = mixer_tail(x1s, zgs, o_attn_s, y_ssm_s).reshape(Bs, T, D)

    return (y_prompt, y_sample,
            p_lat.reshape(1, Bp, S, KV_LORA), p_kr.reshape(1, Bp, S, QK_ROPE), p_ssm[None], p_conv[None],
            s_lat.reshape(1, Bs, T, KV_LORA), s_kr.reshape(1, Bs, T, QK_ROPE), s_ssm[None], s_conv[None])
```

```python
import functools
import math

import jax
import jax.numpy as jnp
import numpy as np
from jax import lax
from jax.experimental import pallas as pl
from jax.experimental.pallas import tpu as pltpu

F32 = jnp.float32
BF16 = jnp.bfloat16

D_MODEL = 2048
CHUNK = 64
N_HEADS_A = 16
Q_LORA = 768
KV_LORA = 512
QK_NOPE = 128
QK_ROPE = 64
V_DIM = 128
ROPE_THETA = 10000.0
ATTN_SCALE = (QK_NOPE + QK_ROPE) ** -0.5
D_INNER = 2 * D_MODEL
HEAD_DIM_S = 64
N_HEADS_S = D_INNER // HEAD_DIM_S
N_GROUPS_S = 8
HEADS_PER_GROUP = N_HEADS_S // N_GROUPS_S
GROUP_W = D_INNER // N_GROUPS_S
D_STATE = 128
CONV_W = 4
CONV_DIM = D_INNER + 2 * N_GROUPS_S * D_STATE
D_FF = 5632
EPS = 1e-6

QK_HEAD = 256
COL_QA = 0
COL_LAT = Q_LORA
COL_KR = COL_LAT + KV_LORA
COL_DT = COL_KR + 2 * QK_ROPE
SMALL_W = COL_DT + 128
XS_W = CONV_DIM + SMALL_W
BC_W = N_GROUPS_S * D_STATE
COL_X = 0
COL_B = D_INNER
COL_C = COL_B + BC_W
COL_Z = 0
COL_GATE = D_INNER
ZG_W = COL_GATE + 2 * D_MODEL
LOG2E = math.log2(math.e)

VMEM_LIMIT = 60 * 1024 * 1024


def _cparams(*sem):
    return pltpu.CompilerParams(dimension_semantics=sem, vmem_limit_bytes=VMEM_LIMIT)


def _rms(x, g):
    return x * lax.rsqrt(jnp.mean(x * x, axis=-1, keepdims=True) + EPS) * g


def _silu(x):
    h = 0.5 * x
    return h + h * jnp.tanh(h)


def _dot(a, b):
    return jnp.dot(a, b, preferred_element_type=F32)


def _dot_nt(a, b):
    return lax.dot_general(a, b, (((1,), (1,)), ((), ())), preferred_element_type=F32)


def _dot_tn(a, b):
    return lax.dot_general(a, b, (((0,), (0,)), ((), ())), preferred_element_type=F32)


def _ffn_kernel(*refs, final_norm, emit_bf16):
    refs = list(refs)
    x_ref, g_ref, wg_ref, wu_ref, wd_ref = refs[:5]
    gf_ref = refs[5] if final_norm else None
    o_ref = refs[5 + final_norm]
    h_scr = refs[-1]
    j = pl.program_id(1)

    @pl.when(j == 0)
    def _():
        x = x_ref[...]
        h_scr[...] = _rms(x, g_ref[...]).astype(BF16)
        o_ref[...] = x

    wg, wu, wd = wg_ref[...].astype(BF16), wu_ref[...].astype(BF16), wd_ref[...].astype(BF16)
    if emit_bf16:
        wg_out, wu_out, wd_out = refs[6 + final_norm:9 + final_norm]
        wg_out[...], wu_out[...], wd_out[...] = wg, wu, wd
    h = h_scr[...]
    a = (0.5 * _silu(_dot(h, wg))) * _dot(h, wu)
    o_ref[...] += _dot(a.astype(BF16), wd)

    if final_norm:
        @pl.when(j == pl.num_programs(1) - 1)
        def _():
            o_ref[...] = _rms(o_ref[...], gf_ref[...])


def _ffn(x, g, wg, wu, wd, gf=None, *, tm=1024, tf=512):
    M, D = x.shape
    F = wg.shape[1]
    tm = min(tm, M)
    emit = wg.dtype != BF16
    assert not emit or M == tm
    col = lambda i, j: (0, j)
    row = lambda i, j: (j, 0)
    in_specs = [
        pl.BlockSpec((tm, D), lambda i, j: (i, 0)),
        pl.BlockSpec((1, D), lambda i, j: (0, 0)),
        pl.BlockSpec((D, tf), col),
        pl.BlockSpec((D, tf), col),
        pl.BlockSpec((tf, D), row),
    ]
    args = [x, g.reshape(1, D), wg, wu, wd]
    if gf is not None:
        in_specs.append(pl.BlockSpec((1, D), lambda i, j: (0, 0)))
        args.append(gf.reshape(1, D))
    out_specs = [pl.BlockSpec((tm, D), lambda i, j: (i, 0))]
    out_shape = [jax.ShapeDtypeStruct((M, D), F32)]
    if emit:
        out_specs += [pl.BlockSpec((D, tf), col), pl.BlockSpec((D, tf), col), pl.BlockSpec((tf, D), row)]
        out_shape += [jax.ShapeDtypeStruct(w.shape, BF16) for w in (wg, wu, wd)]
    out = pl.pallas_call(
        functools.partial(_ffn_kernel, final_norm=gf is not None, emit_bf16=emit),
        grid=(M // tm, F // tf),
        in_specs=in_specs,
        out_specs=out_specs,
        out_shape=out_shape,
        scratch_shapes=[pltpu.VMEM((tm, D), BF16)],
        compiler_params=_cparams("parallel", "arbitrary"),
        name="ffn",
    )(*args)
    return out if emit else out[0]


def _relayout_kernel(w_ref, xs_ref, zg_ref):
    w = w_ref[...]
    o_lat = Q_LORA
    o_kr = o_lat + KV_LORA
    o_z = o_kr + QK_ROPE
    o_x = o_z + D_INNER
    o_bc = o_x + D_INNER
    o_dt = o_bc + 2 * BC_W
    o_gate = o_dt + N_HEADS_S
    half = QK_ROPE // 2
    small = jnp.concatenate(
        [w[:, :o_z], w[:, o_kr + half:o_z], w[:, o_kr:o_kr + half], w[:, o_dt:o_gate],
         jnp.zeros((w.shape[0], 128 - N_HEADS_S), F32)], axis=1)
    xs_ref[:, :CONV_DIM] = w[:, o_x:o_dt].astype(BF16)
    xs_ref[:, CONV_DIM:] = small.astype(BF16)
    zg_ref[:, COL_Z:COL_Z + D_INNER] = w[:, o_z:o_x].astype(BF16)
    zg_ref[:, COL_GATE:] = w[:, o_gate:].astype(BF16)


def _relayout_w_in(w, *, tk=128):
    K, N = w.shape
    widths = (XS_W, ZG_W)
    return pl.pallas_call(
        _relayout_kernel,
        grid=(K // tk,),
        in_specs=[pl.BlockSpec((tk, N), lambda i: (i, 0))],
        out_specs=[pl.BlockSpec((tk, n), lambda i: (i, 0)) for n in widths],
        out_shape=[jax.ShapeDtypeStruct((K, n), BF16) for n in widths],
        compiler_params=_cparams("parallel"),
        name="relayout_w_in",
    )(w)


HIST = 8


def _causal_conv_silu(hist, cur, w, bias):
    ext = jnp.concatenate([hist, cur], axis=0)
    acc = bias
    for k in range(CONV_W - 1):
        acc = acc + w[k:k + 1, :] * pltpu.roll(ext, CONV_W - 1 - k, axis=0)[HIST:, :]
    return _silu(acc + w[CONV_W - 1:CONV_W, :] * cur)


def _inproj_kernel(x_ref, g_ref, w_ref, o_ref, h_scr):
    @pl.when(pl.program_id(1) == 0)
    def _():
        h_scr[...] = _rms(x_ref[...], g_ref[...]).astype(BF16)

    o_ref[...] = _dot(h_scr[...], w_ref[...])


def _inproj(x, g, w, *, tm, tn):
    M, D = x.shape
    N = w.shape[1]
    tm = min(tm, M)
    return pl.pallas_call(
        _inproj_kernel,
        grid=(M // tm, N // tn),
        in_specs=[
            pl.BlockSpec((tm, D), lambda i, j: (i, 0)),
            pl.BlockSpec((1, D), lambda i, j: (0, 0)),
            pl.BlockSpec((D, tn), lambda i, j: (0, j)),
        ],
        out_specs=pl.BlockSpec((tm, tn), lambda i, j: (i, j)),
        out_shape=jax.ShapeDtypeStruct((M, N), F32),
        scratch_shapes=[pltpu.VMEM((tm, D), BF16)],
        compiler_params=_cparams("parallel", "arbitrary"),
        name="inproj",
    )(x, g.reshape(1, D), w)


def _rope_pair(blk, cs, keep):
    prod = blk * cs
    return jnp.where(keep, prod + pltpu.roll(prod, 64, axis=1), 0.0)


def _prep_common(small_ref, cs_ref, nq_ref, nkv_ref, wq_ref, ckv_ref, kr_ref):
    sm = small_ref[...]
    cs = cs_ref[...]
    keep = lax.broadcasted_iota(jnp.int32, cs.shape, 1) < QK_ROPE
    cq = _rms(sm[:, COL_QA:COL_QA + Q_LORA], nq_ref[...]).astype(BF16)
    q = _dot(cq, wq_ref[...]) * (ATTN_SCALE * LOG2E)
    ckv = _rms(sm[:, COL_LAT:COL_LAT + KV_LORA], nkv_ref[...])
    ckv_ref[...] = ckv
    krot = _rope_pair(sm[:, COL_KR:COL_KR + 128], cs, keep)
    kr_ref[...] = krot[:, :QK_ROPE]
    return q, ckv, krot, cs, keep


def _prep_prompt_kernel(small_ref, cs_ref, nq_ref, nkv_ref, wq_ref, wuk_ref, wuv_ref,
                        ckv_ref, kr_ref, q_ref, k_ref, v_ref):
    q, ckv, krot, cs, keep = _prep_common(small_ref, cs_ref, nq_ref, nkv_ref, wq_ref, ckv_ref, kr_ref)
    ckv_b = ckv.astype(BF16)
    kn = _dot(ckv_b, wuk_ref[...])
    krot_b = krot.astype(BF16)
    for h in range(N_HEADS_A):
        c0 = h * QK_HEAD
        q_ref[:, c0:c0 + 128] = q[:, c0:c0 + 128].astype(BF16)
        q_ref[:, c0 + 128:c0 + 256] = _rope_pair(q[:, c0 + 128:c0 + 256], cs, keep).astype(BF16)
        k_ref[:, c0:c0 + 128] = kn[:, h * 128:(h + 1) * 128].astype(BF16)
        k_ref[:, c0 + 128:c0 + 256] = krot_b
    v_ref[...] = _dot(ckv_b, wuv_ref[...]).astype(BF16)


def _prep_sample_kernel(small_ref, cs_ref, nq_ref, nkv_ref, wq_ref, wukt_ref,
                        ckv_ref, kr_ref, qlat_ref, qrot_ref):
    q, _, _, cs, keep = _prep_common(small_ref, cs_ref, nq_ref, nkv_ref, wq_ref, ckv_ref, kr_ref)
    for h in range(N_HEADS_A):
        c0 = h * QK_HEAD
        qn = q[:, c0:c0 + 128].astype(BF16)
        qlat_ref[:, h * KV_LORA:(h + 1) * KV_LORA] = _dot(qn, wukt_ref[h]).astype(BF16)
        qrot_ref[:, h * 128:(h + 1) * 128] = _rope_pair(q[:, c0 + 128:c0 + 256], cs, keep).astype(BF16)


def _prep(proj, cs_table, nq, nkv, wq, extra_w, *, prompt, tm=512):
    M = proj.shape[0]
    tm = min(tm, M)
    n_cs = cs_table.shape[0] // tm
    H = N_HEADS_A
    row = lambda i: (i, 0)
    const2 = lambda i: (0, 0)
    in_specs = [
        pl.BlockSpec((tm, SMALL_W), lambda i: (i, CONV_DIM // SMALL_W)),
        pl.BlockSpec((tm, 128), lambda i: (i % n_cs, 0)),
        pl.BlockSpec((1, Q_LORA), const2),
        pl.BlockSpec((1, KV_LORA), const2),
        pl.BlockSpec((Q_LORA, H * QK_HEAD), const2),
    ]
    out_specs = [pl.BlockSpec((tm, KV_LORA), row), pl.BlockSpec((tm, QK_ROPE), row)]
    out_shape = [jax.ShapeDtypeStruct((M, KV_LORA), F32), jax.ShapeDtypeStruct((M, QK_ROPE), F32)]
    if prompt:
        w_uk, w_uv = extra_w
        in_specs += [pl.BlockSpec((KV_LORA, H * QK_NOPE), const2), pl.BlockSpec((KV_LORA, H * V_DIM), const2)]
        out_specs += [pl.BlockSpec((tm, H * QK_HEAD), row), pl.BlockSpec((tm, H * QK_HEAD), row),
                      pl.BlockSpec((tm, H * V_DIM), row)]
        out_shape += [jax.ShapeDtypeStruct((M, H * QK_HEAD), BF16), jax.ShapeDtypeStruct((M, H * QK_HEAD), BF16),
                      jax.ShapeDtypeStruct((M, H * V_DIM), BF16)]
        body, extra = _prep_prompt_kernel, [w_uk, w_uv]
    else:
        (w_ukt,) = extra_w
        in_specs += [pl.BlockSpec((H, QK_NOPE, KV_LORA), lambda i: (0, 0, 0))]
        out_specs += [pl.BlockSpec((tm, H * KV_LORA), row), pl.BlockSpec((tm, H * 128), row)]
        out_shape += [jax.ShapeDtypeStruct((M, H * KV_LORA), BF16), jax.ShapeDtypeStruct((M, H * 128), BF16)]
        body, extra = _prep_sample_kernel, [w_ukt]
    return pl.pallas_call(
        body,
        grid=(M // tm,),
        in_specs=in_specs,
        out_specs=out_specs,
        out_shape=out_shape,
        compiler_params=_cparams("parallel"),
        name="prep_prompt" if prompt else "prep_sample",
    )(proj, cs_table, nq.reshape(1, -1), nkv.reshape(1, -1), wq, *extra)


def _attn_prompt_kernel(q_ref, k_ref, v_ref, bias_ref, o_ref, s_scr, p_scr, *, seq, tq):
    for qi in reversed(range(seq // tq)):
        kv = (qi + 1) * tq
        s_buf = s_scr.at[qi % 2]
        p_buf = p_scr.at[qi % 2]
        q = q_ref[qi * tq:kv, :]
        s_buf[:, :kv] = _dot_nt(q, k_ref[:kv, :])
        s_buf[:, kv - tq:kv] = s_buf[:, kv - tq:kv] + bias_ref[...]
        m = jnp.max(s_buf[:, :kv], axis=-1, keepdims=True)
        p = jnp.exp2(s_buf[:, :kv] - m)
        l = jnp.sum(p, axis=-1, keepdims=True)
        p_buf[:, :kv] = p.astype(BF16)
        acc = _dot(p_buf[:, :kv], v_ref[:kv, :])
        o_ref[qi * tq:kv, :] = (acc / l).astype(o_ref.dtype)


def _attn_prompt(q, k, v, *, tq=512):
    B, S, _ = q.shape
    H = N_HEADS_A
    chunk = np.arange(tq) // CHUNK
    bias = jnp.asarray(np.where(chunk[None, :] <= chunk[:, None], 0.0, -np.inf).astype(np.float32))
    return pl.pallas_call(
        functools.partial(_attn_prompt_kernel, seq=S, tq=tq),
        grid=(B, H),
        in_specs=[
            pl.BlockSpec((None, S, QK_HEAD), lambda b, h: (b, 0, h)),
            pl.BlockSpec((None, S, QK_HEAD), lambda b, h: (b, 0, h)),
            pl.BlockSpec((None, S, V_DIM), lambda b, h: (b, 0, h)),
            pl.BlockSpec((tq, tq), lambda b, h: (0, 0)),
        ],
        out_specs=pl.BlockSpec((None, S, V_DIM), lambda b, h: (b, 0, h)),
        out_shape=jax.ShapeDtypeStruct((B, S, H * V_DIM), BF16),
        scratch_shapes=[pltpu.VMEM((2, tq, S), F32), pltpu.VMEM((2, tq, S), BF16)],
        compiler_params=_cparams("parallel", "parallel"),
        name="attn_prompt",
    )(q, k, v, bias)


def _attn_sample_kernel(qlat_ref, qrot_ref, clat_ref, ckrt_ref, nlat_ref, nkr_ref, o_ref):
    H = N_HEADS_A
    T = nlat_ref.shape[1]
    for sb in range(clat_ref.shape[0]):
        rows = slice(sb * T, (sb + 1) * T)
        ql = jnp.concatenate([qlat_ref[rows, h * KV_LORA:(h + 1) * KV_LORA] for h in range(H)], axis=0)
        qr = jnp.concatenate([qrot_ref[rows, h * 128:h * 128 + QK_ROPE] for h in range(H)], axis=0)
        lat = clat_ref[sb].astype(BF16)
        nlat = nlat_ref[sb].astype(BF16)
        s1 = _dot_nt(ql, lat) + _dot(qr, ckrt_ref[sb].astype(BF16))
        s2 = _dot_nt(ql, nlat) + _dot_nt(qr, nkr_ref[sb].astype(BF16))
        m = jnp.maximum(jnp.max(s1, axis=-1, keepdims=True), jnp.max(s2, axis=-1, keepdims=True))
        p1 = jnp.exp2(s1 - m)
        p2 = jnp.exp2(s2 - m)
        l = jnp.sum(p1, axis=-1, keepdims=True) + jnp.sum(p2, axis=-1, keepdims=True)
        o = _dot(p1.astype(BF16), lat) + _dot(p2.astype(BF16), nlat)
        o = (o / l).astype(o_ref.dtype)
        for h in range(H):
            o_ref[rows, h * KV_LORA:(h + 1) * KV_LORA] = o[h * T:(h + 1) * T, :]


def _attn_sample(qlat, qrot, cache_lat, cache_kr_t, new_lat, new_kr, *, streams):
    Bs, P, _ = cache_lat.shape
    T = new_lat.shape[1]
    H = N_HEADS_A
    ns = streams
    b2 = lambda b: (b, 0)
    b3 = lambda b: (b, 0, 0)
    return pl.pallas_call(
        _attn_sample_kernel,
        grid=(Bs // ns,),
        in_specs=[
            pl.BlockSpec((ns * T, H * KV_LORA), b2),
            pl.BlockSpec((ns * T, H * 128), b2),
            pl.BlockSpec((ns, P, KV_LORA), b3),
            pl.BlockSpec((ns, QK_ROPE, P), b3),
            pl.BlockSpec((ns, T, KV_LORA), b3),
            pl.BlockSpec((ns, T, QK_ROPE), b3),
        ],
        out_specs=pl.BlockSpec((ns * T, H * KV_LORA), b2),
        out_shape=jax.ShapeDtypeStruct((Bs * T, H * KV_LORA), BF16),
        compiler_params=_cparams("parallel"),
        name="attn_sample",
    )(qlat, qrot, cache_lat, cache_kr_t, new_lat, new_kr)


def _uv_kernel(x_ref, w_ref, o_ref):
    o_ref[...] = _dot(x_ref[...], w_ref[...]).astype(o_ref.dtype)


def _uv_proj(olat, w_uv_heads):
    M = olat.shape[0]
    H = N_HEADS_A
    return pl.pallas_call(
        _uv_kernel,
        grid=(H,),
        in_specs=[pl.BlockSpec((M, KV_LORA), lambda h: (0, h)),
                  pl.BlockSpec((None, KV_LORA, V_DIM), lambda h: (h, 0, 0))],
        out_specs=pl.BlockSpec((M, V_DIM), lambda h: (0, h)),
        out_shape=jax.ShapeDtypeStruct((M, H * V_DIM), BF16),
        compiler_params=_cparams("parallel"),
        name="uv_proj",
    )(olat, w_uv_heads)


def _split3(v):
    hi = v.astype(BF16)
    r1 = v - hi.astype(F32)
    mid = r1.astype(BF16)
    lo = (r1 - mid.astype(F32)).astype(BF16)
    return jnp.concatenate([hi, mid, lo], axis=1)


def _ssd_kernel(x_ref, z_ref, b_ref, c_ref, dt_ref, cst_ref, cw_ref, cbias_ref, dtb_ref, alog_ref,
                dskx_ref, nssm_ref, e64_ref, e128_ref, s0_ref,
                y_ref, sout_ref, xbuf, bbuf, cbuf, xs_scr, bm_scr, cm_scr, s_scr, *, q_len, n_chunks):
    c = pl.program_id(1)
    Q = q_len
    P = HEAD_DIM_S
    G, R, N, W = N_GROUPS_S, HEADS_PER_GROUP, D_STATE, GROUP_W
    h0 = HIST - (CONV_W - 1)

    @pl.when(c == 0)
    def _():
        for buf, lo, hi in ((xbuf, 0, D_INNER), (bbuf, D_INNER, D_INNER + BC_W), (cbuf, D_INNER + BC_W, CONV_DIM)):
            buf[0:h0, :] = jnp.zeros((h0, hi - lo), F32)
            buf[h0:HIST, :] = cst_ref[:, lo:hi]
        s_scr[...] = s0_ref[...]

    def conv(hist, cur_ref, lo, hi):
        cur = cur_ref[...]
        out = _causal_conv_silu(hist[...], cur, cw_ref[:, lo:hi], cbias_ref[:, lo:hi])
        hist[...] = cur[Q - HIST:, :]
        return out

    xs_scr[...] = conv(xbuf, x_ref, 0, D_INNER)
    bm_scr[...] = conv(bbuf, b_ref, D_INNER, D_INNER + BC_W).astype(BF16)
    cm_scr[...] = conv(cbuf, c_ref, D_INNER + BC_W, CONV_DIM).astype(BF16)

    dt_all = jax.nn.softplus(dt_ref[...] + dtb_ref[...])
    a_all = dt_all * (-LOG2E * jnp.exp(alog_ref[...]))
    ri = lax.broadcasted_iota(jnp.int32, (Q, Q), 0)
    ci = lax.broadcasted_iota(jnp.int32, (Q, Q), 1)
    causal = ci <= ri
    hp = lax.Precision.HIGHEST
    acs_all = jnp.dot(causal.astype(F32), a_all, precision=hp, preferred_element_type=F32)
    acs_row = lax.dot_general(a_all, (ri <= ci).astype(F32), (((0,), (0,)), ((), ())),
                              precision=hp, preferred_element_type=F32)
    a_end = acs_all[Q - 1:Q, :]
    eacs_all = jnp.exp2(acs_all)
    w_all = dt_all * jnp.exp2(a_end - acs_all)
    dec_all = jnp.exp2(a_end)
    left = lax.broadcasted_iota(jnp.int32, (Q, 128), 1) < P
    e64 = e64_ref[...]
    e128 = e128_ref[...]

    for g in range(G):
        h_lo = g * R
        dtx = _dot(_split3(dt_all[:, h_lo:h_lo + R]), e64)
        eacsx = _dot(_split3(eacs_all[:, h_lo:h_lo + R]), e64)
        wx = _dot(_split3(w_all[:, h_lo:h_lo + R]), e64)
        acsb = _dot(_split3(acs_all[:, h_lo:h_lo + R]), e128)
        xs = xs_scr[:, g * W:(g + 1) * W]
        bm = bm_scr[:, g * N:(g + 1) * N]
        cm = cm_scr[:, g * N:(g + 1) * N]
        xdt = xs * dtx
        cb = _dot_nt(cm, bm)
        y_off = _dot_nt(cm, s_scr[g].astype(BF16))
        ys = []
        for j in range(R // 2):
            ms = []
            for r in (2 * j, 2 * j + 1):
                seg = acsb[:, r * 128:r * 128 + Q] - acs_row[h_lo + r:h_lo + r + 1, :]
                ms.append((cb * jnp.exp2(jnp.where(causal, seg, -jnp.inf))).astype(BF16))
            xp = xdt[:, j * 128:(j + 1) * 128]
            rhs = jnp.concatenate([jnp.where(left, xp, 0.0), jnp.where(left, 0.0, xp)], axis=0).astype(BF16)
            ys.append(_dot(jnp.concatenate(ms, axis=1), rhs))
        y = jnp.concatenate(ys, axis=1) + y_off * eacsx + dskx_ref[:, g * W:(g + 1) * W] * xs

        st = _dot_tn((xs * wx).astype(BF16), bm)
        for r in range(R):
            dec = dec_all[:, h_lo + r:h_lo + r + 1]
            s_scr[g, r * P:(r + 1) * P, :] = dec * s_scr[g, r * P:(r + 1) * P, :] + st[r * P:(r + 1) * P, :]

        v = y * _silu(z_ref[:, g * W:(g + 1) * W])
        y_ref[:, g * W:(g + 1) * W] = _rms(v, nssm_ref[:, g * W:(g + 1) * W]).astype(y_ref.dtype)

    @pl.when(c == n_chunks - 1)
    def _():
        sout_ref[...] = s_scr[...]


def _head_expander(width):
    R = HEADS_PER_GROUP
    row_head = np.arange(3 * R) % R
    col_head = np.arange(R * width) // width
    return jnp.asarray((row_head[:, None] == col_head[None, :]).astype(np.float32)).astype(BF16)


def _ssd(xs3, zg3, conv_state, conv_w, conv_b, dt_bias, a_log, d_skip, n_ssm, state0, *, q_len):
    B, L, _ = xs3.shape
    G, N = N_GROUPS_S, D_STATE
    Q = q_len
    nc = L // Q
    pad = 128 - N_HEADS_S
    const = lambda b, c: (0, 0)
    in_specs = [
        pl.BlockSpec((None, Q, D_INNER), lambda b, c: (b, c, COL_X // D_INNER)),
        pl.BlockSpec((None, Q, D_INNER), lambda b, c: (b, c, COL_Z // D_INNER)),
        pl.BlockSpec((None, Q, BC_W), lambda b, c: (b, c, COL_B // BC_W)),
        pl.BlockSpec((None, Q, BC_W), lambda b, c: (b, c, COL_C // BC_W)),
        pl.BlockSpec((None, Q, 128), lambda b, c: (b, c, (CONV_DIM + COL_DT) // 128)),
        pl.BlockSpec((None, CONV_W - 1, CONV_DIM), lambda b, c: (b, 0, 0)),
        pl.BlockSpec((CONV_W, CONV_DIM), const),
        pl.BlockSpec((1, CONV_DIM), const),
        pl.BlockSpec((1, 128), const),
        pl.BlockSpec((1, 128), const),
        pl.BlockSpec((1, D_INNER), const),
        pl.BlockSpec((1, D_INNER), const),
        pl.BlockSpec((3 * HEADS_PER_GROUP, GROUP_W), const),
        pl.BlockSpec((3 * HEADS_PER_GROUP, HEADS_PER_GROUP * 128), const),
        pl.BlockSpec((None, G, GROUP_W, N), lambda b, c: (b, 0, 0, 0)),
    ]
    args = [xs3, zg3, xs3, xs3, xs3, conv_state, conv_w, conv_b.reshape(1, CONV_DIM),
            jnp.pad(dt_bias, (0, pad)).reshape(1, 128), jnp.pad(a_log, (0, pad)).reshape(1, 128),
            jnp.repeat(d_skip, HEAD_DIM_S).reshape(1, D_INNER), n_ssm.reshape(1, D_INNER),
            _head_expander(HEAD_DIM_S), _head_expander(128), state0.reshape(B, G, GROUP_W, N)]
    y, s_out = pl.pallas_call(
        functools.partial(_ssd_kernel, q_len=Q, n_chunks=nc),
        grid=(B, nc),
        in_specs=in_specs,
        out_specs=[pl.BlockSpec((None, Q, D_INNER), lambda b, c: (b, c, 0)),
                   pl.BlockSpec((None, G, GROUP_W, N), lambda b, c: (b, 0, 0, 0))],
        out_shape=[jax.ShapeDtypeStruct((B, L, D_INNER), BF16),
                   jax.ShapeDtypeStruct((B, G, GROUP_W, N), F32)],
        scratch_shapes=[pltpu.VMEM((HIST, D_INNER), F32), pltpu.VMEM((HIST, BC_W), F32),
                        pltpu.VMEM((HIST, BC_W), F32), pltpu.VMEM((Q, D_INNER), F32),
                        pltpu.VMEM((Q, BC_W), BF16), pltpu.VMEM((Q, BC_W), BF16),
                        pltpu.VMEM((G, GROUP_W, N), F32)],
        compiler_params=_cparams("parallel", "arbitrary"),
        name="ssd",
    )(*args)
    return y, s_out.reshape(B, N_HEADS_S, HEAD_DIM_S, N)


def _merge_kernel(oa_ref, ys_ref, wa_ref, ws_ref, g0_ref, g1_ref, bg_ref, o_ref, *w_out_refs):
    wa, ws = wa_ref[...].astype(BF16), ws_ref[...].astype(BF16)
    if w_out_refs:
        w_out_refs[0][...], w_out_refs[1][...] = wa, ws
    attn = _dot(oa_ref[...], wa)
    ssm = _dot(ys_ref[...], ws)
    bg = bg_ref[...]
    g0 = jax.nn.sigmoid(g0_ref[...] + bg[0:1, :])
    g1 = jax.nn.sigmoid(g1_ref[...] + bg[1:2, :])
    o_ref[...] = (g0 * attn + g1 * ssm).astype(o_ref.dtype)


def _merge(o_attn, y_ssm, w_o_attn, w_o_ssm, proj, b_gate, *, tm=512, tn=512):
    M = o_attn.shape[0]
    tm = min(tm, M)
    emit = w_o_attn.dtype != BF16
    assert not emit or M == tm
    g0blk = COL_GATE // tn
    g1blk = (COL_GATE + D_MODEL) // tn
    col = lambda i, j: (0, j)
    w_specs = [pl.BlockSpec((w_o_attn.shape[0], tn), col), pl.BlockSpec((D_INNER, tn), col)]
    out_specs = [pl.BlockSpec((tm, tn), lambda i, j: (i, j))]
    out_shape = [jax.ShapeDtypeStruct((M, D_MODEL), BF16)]
    if emit:
        out_specs += w_specs
        out_shape += [jax.ShapeDtypeStruct(w.shape, BF16) for w in (w_o_attn, w_o_ssm)]
    out = pl.pallas_call(
        _merge_kernel,
        grid=(M // tm, D_MODEL // tn),
        in_specs=[
            pl.BlockSpec((tm, o_attn.shape[1]), lambda i, j: (i, 0)),
            pl.BlockSpec((tm, D_INNER), lambda i, j: (i, 0)),
            *w_specs,
            pl.BlockSpec((tm, tn), lambda i, j: (i, g0blk + j)),
            pl.BlockSpec((tm, tn), lambda i, j: (i, g1blk + j)),
            pl.BlockSpec((2, tn), col),
        ],
        out_specs=out_specs,
        out_shape=out_shape,
        compiler_params=_cparams("parallel", "arbitrary"),
        name="merge",
    )(o_attn, y_ssm, w_o_attn, w_o_ssm, proj, proj, b_gate)
    return out if emit else out[0]


def _outproj_kernel(m_ref, w_ref, x_ref, o_ref, *w_out_refs):
    w = w_ref[...].astype(BF16)
    if w_out_refs:
        w_out_refs[0][...] = w
    o_ref[...] = x_ref[...] + _dot(m_ref[...], w)


def _outproj(merged, w_out, x, *, tm=512, tn=512):
    M = x.shape[0]
    tm = min(tm, M)
    emit = w_out.dtype != BF16
    assert not emit or M == tm
    w_spec = pl.BlockSpec((D_MODEL, tn), lambda i, j: (0, j))
    out_specs = [pl.BlockSpec((tm, tn), lambda i, j: (i, j))]
    out_shape = [jax.ShapeDtypeStruct((M, D_MODEL), F32)]
    if emit:
        out_specs.append(w_spec)
        out_shape.append(jax.ShapeDtypeStruct(w_out.shape, BF16))
    out = pl.pallas_call(
        _outproj_kernel,
        grid=(M // tm, D_MODEL // tn),
        in_specs=[
            pl.BlockSpec((tm, D_MODEL), lambda i, j: (i, 0)),
            w_spec,
            pl.BlockSpec((tm, tn), lambda i, j: (i, j)),
        ],
        out_specs=out_specs,
        out_shape=out_shape,
        compiler_params=_cparams("parallel", "arbitrary"),
        name="outproj",
    )(merged, w_out, x)
    return out if emit else out[0]


def _swap_halves(w):
    half = QK_ROPE // 2
    return jnp.concatenate([w[..., half:], w[..., :half]], axis=-1)


def _rope_table(pos):
    half = QK_ROPE // 2
    inv = np.power(ROPE_THETA, -np.arange(half, dtype=np.float64) / half)
    ang = np.asarray(pos, np.float64)[:, None] * inv[None, :]
    cos, sin = np.cos(ang), np.sin(ang)
    return np.concatenate([cos, cos, -sin, sin], axis=-1).astype(np.float32)


def kernel(x_prompt, x_sample, cache_kv_latent, cache_k_rope, state_ssm, state_conv, norm_ffn1, w_ffn1_gate, w_ffn1_up, w_ffn1_down, norm_mix, w_in, b_gate, norm_q_a, w_q_b, norm_kv_a, w_kv_b, conv_w, conv_b, dt_bias, a_log, d_skip, norm_ssm, w_o_attn, w_o_ssm, w_out, norm_ffn2, w_ffn2_gate, w_ffn2_up, w_ffn2_down, norm_final):
    Bp, S, D = x_prompt.shape
    Bs, T, _ = x_sample.shape
    past = cache_kv_latent.shape[2]
    H = N_HEADS_A
    l = 0

    w_xs, w_zg = _relayout_w_in(w_in[l])
    wq3 = w_q_b[l].reshape(Q_LORA, H, QK_NOPE + QK_ROPE)
    wq = jnp.concatenate([wq3, _swap_halves(wq3[..., QK_NOPE:])], axis=-1).reshape(Q_LORA, H * QK_HEAD).astype(BF16)
    wkv3 = w_kv_b[l].reshape(KV_LORA, H, QK_NOPE + V_DIM)
    w_uk = wkv3[..., :QK_NOPE]
    w_uv = wkv3[..., QK_NOPE:]
    w_uk_flat = w_uk.reshape(KV_LORA, H * QK_NOPE).astype(BF16)
    w_uv_flat = w_uv.reshape(KV_LORA, H * V_DIM).astype(BF16)
    w_uk_t = w_uk.transpose(1, 2, 0).astype(BF16)
    w_uv_h = w_uv.transpose(1, 0, 2).astype(BF16)
    def mixer_mid(x1, batch):
        xs = _inproj(x1, norm_mix[l], w_xs, tm=1024, tn=SMALL_W)
        zg = _inproj(x1, norm_mix[l], w_zg, tm=1024, tn=1024)
        xs3 = xs.reshape(batch, -1, XS_W)
        return xs, zg, xs3[:, xs3.shape[1] - (CONV_W - 1):, :CONV_DIM]


    def ssd(xs, zg, batch, conv_state, state, q_len):
        y, new_state = _ssd(xs.reshape(batch, -1, XS_W), zg.reshape(batch, -1, ZG_W), conv_state,
                            conv_w[l], conv_b[l], dt_bias[l], a_log[l], d_skip[l], norm_ssm[l], state, q_len=q_len)
        return y.reshape(-1, D_INNER), new_state

    x1s, wg1, wu1, wd1 = _ffn(x_sample.reshape(Bs * T, D), norm_ffn1[l], w_ffn1_gate[l], w_ffn1_up[l],
                              w_ffn1_down[l], tf=256)
    xss, zgs, s_conv = mixer_mid(x1s, Bs)
    cs_s = jnp.asarray(np.tile(_rope_table(past + np.arange(T)), (Bs, 1)))
    s_lat, s_kr, qlat, qrot = _prep(xss, cs_s, norm_q_a[l], norm_kv_a[l], wq, (w_uk_t,), prompt=False)
    o_lat = _attn_sample(qlat, qrot, cache_kv_latent[l], jnp.swapaxes(cache_k_rope[l], 1, 2),
                         s_lat.reshape(Bs, T, KV_LORA), s_kr.reshape(Bs, T, QK_ROPE),
                         streams=2 if Bs % 2 == 0 else 1)
    o_attn_s = _uv_proj(o_lat, w_uv_h)
    y_ssm_s, s_ssm = ssd(xss, zgs, Bs, state_conv[l], state_ssm[l], T)
    merged_s, woa, wos = _merge(o_attn_s, y_ssm_s, w_o_attn[l], w_o_ssm[l], zgs, b_gate[l], tm=1024, tn=256)
    x2s, wo = _outproj(merged_s, w_out[l], x1s, tm=1024, tn=1024)
    y_sample, wg2, wu2, wd2 = _ffn(x2s, norm_ffn2[l], w_ffn2_gate[l], w_ffn2_up[l], w_ffn2_down[l], norm_final,
                                   tf=256)
    y_sample = y_sample.reshape(Bs, T, D)

    x1 = _ffn(x_prompt.reshape(Bp * S, D), norm_ffn1[l], wg1, wu1, wd1)
    xs, zg, p_conv = mixer_mid(x1, Bp)
    cs_p = jnp.asarray(_rope_table(np.arange(S)))
    p_lat, p_kr, q, k, v = _prep(xs, cs_p, norm_q_a[l], norm_kv_a[l], wq, (w_uk_flat, w_uv_flat), prompt=True)
    o_attn = _attn_prompt(q.reshape(Bp, S, -1), k.reshape(Bp, S, -1), v.reshape(Bp, S, -1)).reshape(Bp * S, -1)
    y_ssm, p_ssm = ssd(xs, zg, Bp, jnp.zeros((Bp, CONV_W - 1, CONV_DIM), F32),
                       jnp.zeros((Bp, N_HEADS_S, HEAD_DIM_S, D_STATE), F32), 128)
    merged = _merge(o_attn, y_ssm, woa, wos, zg, b_gate[l], tm=1024, tn=512)
    x2 = _outproj(merged, wo, x1, tm=1024, tn=1024)
    y_prompt = _ffn(x2, norm_ffn2[l], wg2, wu2, wd2, norm_final).reshape(Bp, S, D)

    return (y_prompt, y_sample,
            p_lat.reshape(1, Bp, S, KV_LORA), p_kr.reshape(1, Bp, S, QK_ROPE), p_ssm[None], p_conv[None],
            s_lat.reshape(1, Bs, T, KV_LORA), s_kr.reshape(1, Bs, T, QK_ROPE), s_ssm[None], s_conv[None])
```

```python
import functools
import math

import jax
import jax.numpy as jnp
import numpy as np
from jax import lax
from jax.experimental import pallas as pl
from jax.experimental.pallas import tpu as pltpu

F32 = jnp.float32
BF16 = jnp.bfloat16

D_MODEL = 2048
CHUNK = 64
N_HEADS_A = 16
Q_LORA = 768
KV_LORA = 512
QK_NOPE = 128
QK_ROPE = 64
V_DIM = 128
ROPE_THETA = 10000.0
ATTN_SCALE = (QK_NOPE + QK_ROPE) ** -0.5
D_INNER = 2 * D_MODEL
HEAD_DIM_S = 64
N_HEADS_S = D_INNER // HEAD_DIM_S
N_GROUPS_S = 8
HEADS_PER_GROUP = N_HEADS_S // N_GROUPS_S
GROUP_W = D_INNER // N_GROUPS_S
D_STATE = 128
CONV_W = 4
CONV_DIM = D_INNER + 2 * N_GROUPS_S * D_STATE
D_FF = 5632
EPS = 1e-6

QK_HEAD = 256
COL_QA = 0
COL_LAT = Q_LORA
COL_KR = COL_LAT + KV_LORA
COL_DT = COL_KR + 2 * QK_ROPE
SMALL_W = COL_DT + 128
XS_W = CONV_DIM + SMALL_W
BC_W = N_GROUPS_S * D_STATE
COL_X = 0
COL_B = D_INNER
COL_C = COL_B + BC_W
COL_Z = 0
COL_GATE = D_INNER
ZG_W = COL_GATE + 2 * D_MODEL
LOG2E = math.log2(math.e)

VMEM_LIMIT = 60 * 1024 * 1024


def _cparams(*sem):
    return pltpu.CompilerParams(dimension_semantics=sem, vmem_limit_bytes=VMEM_LIMIT)


def _rms(x, g):
    return x * lax.rsqrt(jnp.mean(x * x, axis=-1, keepdims=True) + EPS) * g


def _silu(x):
    h = 0.5 * x
    return h + h * jnp.tanh(h)


def _dot(a, b):
    return jnp.dot(a, b, preferred_element_type=F32)


def _dot_nt(a, b):
    return lax.dot_general(a, b, (((1,), (1,)), ((), ())), preferred_element_type=F32)


def _dot_tn(a, b):
    return lax.dot_general(a, b, (((0,), (0,)), ((), ())), preferred_element_type=F32)


def _ffn_kernel(*refs, final_norm, emit_bf16):
    refs = list(refs)
    x_ref, g_ref, wg_ref, wu_ref, wd_ref = refs[:5]
    gf_ref = refs[5] if final_norm else None
    o_ref = refs[5 + final_norm]
    h_scr = refs[-1]
    j = pl.program_id(1)

    @pl.when(j == 0)
    def _():
        x = x_ref[...]
        h_scr[...] = _rms(x, g_ref[...]).astype(BF16)
        o_ref[...] = x

    wg, wu, wd = wg_ref[...].astype(BF16), wu_ref[...].astype(BF16), wd_ref[...].astype(BF16)
    if emit_bf16:
        wg_out, wu_out, wd_out = refs[6 + final_norm:9 + final_norm]
        wg_out[...], wu_out[...], wd_out[...] = wg, wu, wd
    h = h_scr[...]
    a = (0.5 * _silu(_dot(h, wg))) * _dot(h, wu)
    o_ref[...] += _dot(a.astype(BF16), wd)

    if final_norm:
        @pl.when(j == pl.num_programs(1) - 1)
        def _():
            o_ref[...] = _rms(o_ref[...], gf_ref[...])


def _ffn(x, g, wg, wu, wd, gf=None, *, tm=1024, tf=512):
    M, D = x.shape
    F = wg.shape[1]
    tm = min(tm, M)
    emit = wg.dtype != BF16
    assert not emit or M == tm
    col = lambda i, j: (0, j)
    row = lambda i, j: (j, 0)
    in_specs = [
        pl.BlockSpec((tm, D), lambda i, j: (i, 0)),
        pl.BlockSpec((1, D), lambda i, j: (0, 0)),
        pl.BlockSpec((D, tf), col),
        pl.BlockSpec((D, tf), col),
        pl.BlockSpec((tf, D), row),
    ]
    args = [x, g.reshape(1, D), wg, wu, wd]
    if gf is not None:
        in_specs.append(pl.BlockSpec((1, D), lambda i, j: (0, 0)))
        args.append(gf.reshape(1, D))
    out_specs = [pl.BlockSpec((tm, D), lambda i, j: (i, 0))]
    out_shape = [jax.ShapeDtypeStruct((M, D), F32)]
    if emit:
        out_specs += [pl.BlockSpec((D, tf), col), pl.BlockSpec((D, tf), col), pl.BlockSpec((tf, D), row)]
        out_shape += [jax.ShapeDtypeStruct(w.shape, BF16) for w in (wg, wu, wd)]
    out = pl.pallas_call(
        functools.partial(_ffn_kernel, final_norm=gf is not None, emit_bf16=emit),
        grid=(M // tm, F // tf),
        in_specs=in_specs,
        out_specs=out_specs,
        out_shape=out_shape,
        scratch_shapes=[pltpu.VMEM((tm, D), BF16)],
        compiler_params=_cparams("parallel", "arbitrary"),
        name="ffn",
    )(*args)
    return out if emit else out[0]


HIST = 8


def _causal_conv_silu(hist, cur, w, bias):
    ext = jnp.concatenate([hist, cur], axis=0)
    acc = bias
    for k in range(CONV_W - 1):
        acc = acc + w[k:k + 1, :] * pltpu.roll(ext, CONV_W - 1 - k, axis=0)[HIST:, :]
    return _silu(acc + w[CONV_W - 1:CONV_W, :] * cur)


def _inproj_kernel(x_ref, g_ref, w_ref, o_ref, h_scr):
    @pl.when(pl.program_id(1) == 0)
    def _():
        h_scr[...] = _rms(x_ref[...], g_ref[...]).astype(BF16)

    o_ref[...] = _dot(h_scr[...], w_ref[...])


def _inproj(x, g, w, *, tm, tn):
    M, D = x.shape
    N = w.shape[1]
    tm = min(tm, M)
    return pl.pallas_call(
        _inproj_kernel,
        grid=(M // tm, N // tn),
        in_specs=[
            pl.BlockSpec((tm, D), lambda i, j: (i, 0)),
            pl.BlockSpec((1, D), lambda i, j: (0, 0)),
            pl.BlockSpec((D, tn), lambda i, j: (0, j)),
        ],
        out_specs=pl.BlockSpec((tm, tn), lambda i, j: (i, j)),
        out_shape=jax.ShapeDtypeStruct((M, N), F32),
        scratch_shapes=[pltpu.VMEM((tm, D), BF16)],
        compiler_params=_cparams("parallel", "arbitrary"),
        name="inproj",
    )(x, g.reshape(1, D), w)


WIN_PAD = 128


def _inproj_cast_kernel(x_ref, g_ref, win_ref, aux_ref, o_ref, w16_ref, h_scr, *, regions):
    j = pl.program_id(0)
    tn = o_ref.shape[1]

    @pl.when(j == 0)
    def _():
        h_scr[...] = _rms(x_ref[...], g_ref[...]).astype(BF16)

    for lo, hi, _, shift in regions:
        @pl.when(jnp.logical_and(j >= lo, j < hi))
        def _(shift=shift):
            w = (aux_ref[...] if shift is None else win_ref[:, shift:shift + tn]).astype(BF16)
            w16_ref[...] = w
            o_ref[...] = _dot(h_scr[...], w)


def _inproj_cast(x, g, w_in, w_aux, regions, *, tn):
    M, D = x.shape
    n_tiles = regions[-1][1]
    n_aux = w_aux.shape[1] // tn
    aux_lo = next((lo for lo, _, start, _ in regions if start is None), 0)

    def win_start(j):
        lane_tiles = jnp.int32(0)
        for lo, hi, s0, _ in regions:
            if s0 is None:
                continue
            assert s0 % 128 == 0 and tn % 128 == 0
            lane_tiles = jnp.where(j >= lo, s0 // 128 + (tn // 128) * (jnp.minimum(j, hi - 1) - lo), lane_tiles)
        return pl.multiple_of(lane_tiles * 128, 128)

    return pl.pallas_call(
        functools.partial(_inproj_cast_kernel, regions=regions),
        grid=(n_tiles,),
        in_specs=[
            pl.BlockSpec((M, D), lambda j: (0, 0)),
            pl.BlockSpec((1, D), lambda j: (0, 0)),
            pl.BlockSpec((pl.Element(D), pl.Element(tn + WIN_PAD)), lambda j: (0, win_start(j))),
            pl.BlockSpec((D, tn), lambda j: (0, jnp.clip(j - aux_lo, 0, n_aux - 1))),
        ],
        out_specs=[pl.BlockSpec((M, tn), lambda j: (0, j)), pl.BlockSpec((D, tn), lambda j: (0, j))],
        out_shape=[jax.ShapeDtypeStruct((M, n_tiles * tn), F32), jax.ShapeDtypeStruct((D, n_tiles * tn), BF16)],
        scratch_shapes=[pltpu.VMEM((M, D), BF16)],
        compiler_params=_cparams("arbitrary"),
        name="inproj_cast",
    )(x, g.reshape(1, D), w_in, w_aux)


def _rope_pair(blk, cs, keep):
    prod = blk * cs
    return jnp.where(keep, prod + pltpu.roll(prod, 64, axis=1), 0.0)


def _prep_common(small_ref, cs_ref, nq_ref, nkv_ref, wq_ref, ckv_ref, kr_ref):
    sm = small_ref[...]
    cs = cs_ref[...]
    keep = lax.broadcasted_iota(jnp.int32, cs.shape, 1) < QK_ROPE
    cq = _rms(sm[:, COL_QA:COL_QA + Q_LORA], nq_ref[...]).astype(BF16)
    q = _dot(cq, wq_ref[...]) * (ATTN_SCALE * LOG2E)
    ckv = _rms(sm[:, COL_LAT:COL_LAT + KV_LORA], nkv_ref[...])
    ckv_ref[...] = ckv
    krot = _rope_pair(sm[:, COL_KR:COL_KR + 128], cs, keep)
    kr_ref[...] = krot[:, :QK_ROPE]
    return q, ckv, krot, cs, keep


def _prep_prompt_kernel(small_ref, cs_ref, nq_ref, nkv_ref, wq_ref, wuk_ref, wuv_ref,
                        ckv_ref, kr_ref, q_ref, k_ref, v_ref):
    q, ckv, krot, cs, keep = _prep_common(small_ref, cs_ref, nq_ref, nkv_ref, wq_ref, ckv_ref, kr_ref)
    ckv_b = ckv.astype(BF16)
    kn = _dot(ckv_b, wuk_ref[...])
    krot_b = krot.astype(BF16)
    for h in range(N_HEADS_A):
        c0 = h * QK_HEAD
        q_ref[:, c0:c0 + 128] = q[:, c0:c0 + 128].astype(BF16)
        q_ref[:, c0 + 128:c0 + 256] = _rope_pair(q[:, c0 + 128:c0 + 256], cs, keep).astype(BF16)
        k_ref[:, c0:c0 + 128] = kn[:, h * 128:(h + 1) * 128].astype(BF16)
        k_ref[:, c0 + 128:c0 + 256] = krot_b
    v_ref[...] = _dot(ckv_b, wuv_ref[...]).astype(BF16)


def _prep_sample_kernel(small_ref, cs_ref, nq_ref, nkv_ref, wq_ref, wukt_ref,
                        ckv_ref, kr_ref, qlat_ref, qrot_ref):
    q, _, _, cs, keep = _prep_common(small_ref, cs_ref, nq_ref, nkv_ref, wq_ref, ckv_ref, kr_ref)
    for h in range(N_HEADS_A):
        c0 = h * QK_HEAD
        qn = q[:, c0:c0 + 128].astype(BF16)
        qlat_ref[:, h * KV_LORA:(h + 1) * KV_LORA] = _dot(qn, wukt_ref[h]).astype(BF16)
        qrot_ref[:, h * 128:(h + 1) * 128] = _rope_pair(q[:, c0 + 128:c0 + 256], cs, keep).astype(BF16)


def _prep(proj, cs_table, nq, nkv, wq, extra_w, *, prompt, tm=512):
    M = proj.shape[0]
    tm = min(tm, M)
    n_cs = cs_table.shape[0] // tm
    H = N_HEADS_A
    row = lambda i: (i, 0)
    const2 = lambda i: (0, 0)
    in_specs = [
        pl.BlockSpec((tm, SMALL_W), lambda i: (i, CONV_DIM // SMALL_W)),
        pl.BlockSpec((tm, 128), lambda i: (i % n_cs, 0)),
        pl.BlockSpec((1, Q_LORA), const2),
        pl.BlockSpec((1, KV_LORA), const2),
        pl.BlockSpec((Q_LORA, H * QK_HEAD), const2),
    ]
    out_specs = [pl.BlockSpec((tm, KV_LORA), row), pl.BlockSpec((tm, QK_ROPE), row)]
    out_shape = [jax.ShapeDtypeStruct((M, KV_LORA), F32), jax.ShapeDtypeStruct((M, QK_ROPE), F32)]
    if prompt:
        w_uk, w_uv = extra_w
        in_specs += [pl.BlockSpec((KV_LORA, H * QK_NOPE), const2), pl.BlockSpec((KV_LORA, H * V_DIM), const2)]
        out_specs += [pl.BlockSpec((tm, H * QK_HEAD), row), pl.BlockSpec((tm, H * QK_HEAD), row),
                      pl.BlockSpec((tm, H * V_DIM), row)]
        out_shape += [jax.ShapeDtypeStruct((M, H * QK_HEAD), BF16), jax.ShapeDtypeStruct((M, H * QK_HEAD), BF16),
                      jax.ShapeDtypeStruct((M, H * V_DIM), BF16)]
        body, extra = _prep_prompt_kernel, [w_uk, w_uv]
    else:
        (w_ukt,) = extra_w
        in_specs += [pl.BlockSpec((H, QK_NOPE, KV_LORA), lambda i: (0, 0, 0))]
        out_specs += [pl.BlockSpec((tm, H * KV_LORA), row), pl.BlockSpec((tm, H * 128), row)]
        out_shape += [jax.ShapeDtypeStruct((M, H * KV_LORA), BF16), jax.ShapeDtypeStruct((M, H * 128), BF16)]
        body, extra = _prep_sample_kernel, [w_ukt]
    return pl.pallas_call(
        body,
        grid=(M // tm,),
        in_specs=in_specs,
        out_specs=out_specs,
        out_shape=out_shape,
        compiler_params=_cparams("parallel"),
        name="prep_prompt" if prompt else "prep_sample",
    )(proj, cs_table, nq.reshape(1, -1), nkv.reshape(1, -1), wq, *extra)


def _attn_prompt_kernel(q_ref, k_ref, v_ref, bias_ref, o_ref, s_scr, p_scr, *, seq, tq):
    for qi in reversed(range(seq // tq)):
        kv = (qi + 1) * tq
        s_buf = s_scr.at[qi % 2]
        p_buf = p_scr.at[qi % 2]
        q = q_ref[qi * tq:kv, :]
        s_buf[:, :kv] = _dot_nt(q, k_ref[:kv, :])
        s_buf[:, kv - tq:kv] = s_buf[:, kv - tq:kv] + bias_ref[...]
        m = jnp.max(s_buf[:, :kv], axis=-1, keepdims=True)
        p = jnp.exp2(s_buf[:, :kv] - m)
        l = jnp.sum(p, axis=-1, keepdims=True)
        p_buf[:, :kv] = p.astype(BF16)
        acc = _dot(p_buf[:, :kv], v_ref[:kv, :])
        o_ref[qi * tq:kv, :] = (acc / l).astype(o_ref.dtype)


def _attn_prompt(q, k, v, *, tq=512):
    B, S, _ = q.shape
    H = N_HEADS_A
    chunk = np.arange(tq) // CHUNK
    bias = jnp.asarray(np.where(chunk[None, :] <= chunk[:, None], 0.0, -np.inf).astype(np.float32))
    return pl.pallas_call(
        functools.partial(_attn_prompt_kernel, seq=S, tq=tq),
        grid=(B, H),
        in_specs=[
            pl.BlockSpec((None, S, QK_HEAD), lambda b, h: (b, 0, h)),
            pl.BlockSpec((None, S, QK_HEAD), lambda b, h: (b, 0, h)),
            pl.BlockSpec((None, S, V_DIM), lambda b, h: (b, 0, h)),
            pl.BlockSpec((tq, tq), lambda b, h: (0, 0)),
        ],
        out_specs=pl.BlockSpec((None, S, V_DIM), lambda b, h: (b, 0, h)),
        out_shape=jax.ShapeDtypeStruct((B, S, H * V_DIM), BF16),
        scratch_shapes=[pltpu.VMEM((2, tq, S), F32), pltpu.VMEM((2, tq, S), BF16)],
        compiler_params=_cparams("parallel", "parallel"),
        name="attn_prompt",
    )(q, k, v, bias)


def _attn_sample_kernel(qlat_ref, qrot_ref, clat_ref, ckrt_ref, nlat_ref, nkr_ref, o_ref):
    H = N_HEADS_A
    T = nlat_ref.shape[1]
    for sb in range(clat_ref.shape[0]):
        rows = slice(sb * T, (sb + 1) * T)
        ql = jnp.concatenate([qlat_ref[rows, h * KV_LORA:(h + 1) * KV_LORA] for h in range(H)], axis=0)
        qr = jnp.concatenate([qrot_ref[rows, h * 128:h * 128 + QK_ROPE] for h in range(H)], axis=0)
        lat = clat_ref[sb].astype(BF16)
        nlat = nlat_ref[sb].astype(BF16)
        s1 = _dot_nt(ql, lat) + _dot(qr, ckrt_ref[sb].astype(BF16))
        s2 = _dot_nt(ql, nlat) + _dot_nt(qr, nkr_ref[sb].astype(BF16))
        m = jnp.maximum(jnp.max(s1, axis=-1, keepdims=True), jnp.max(s2, axis=-1, keepdims=True))
        p1 = jnp.exp2(s1 - m)
        p2 = jnp.exp2(s2 - m)
        l = jnp.sum(p1, axis=-1, keepdims=True) + jnp.sum(p2, axis=-1, keepdims=True)
        o = _dot(p1.astype(BF16), lat) + _dot(p2.astype(BF16), nlat)
        o = (o / l).astype(o_ref.dtype)
        for h in range(H):
            o_ref[rows, h * KV_LORA:(h + 1) * KV_LORA] = o[h * T:(h + 1) * T, :]


def _attn_sample(qlat, qrot, cache_lat, cache_kr_t, new_lat, new_kr, *, streams):
    Bs, P, _ = cache_lat.shape
    T = new_lat.shape[1]
    H = N_HEADS_A
    ns = streams
    b2 = lambda b: (b, 0)
    b3 = lambda b: (b, 0, 0)
    return pl.pallas_call(
        _attn_sample_kernel,
        grid=(Bs // ns,),
        in_specs=[
            pl.BlockSpec((ns * T, H * KV_LORA), b2),
            pl.BlockSpec((ns * T, H * 128), b2),
            pl.BlockSpec((ns, P, KV_LORA), b3),
            pl.BlockSpec((ns, QK_ROPE, P), b3),
            pl.BlockSpec((ns, T, KV_LORA), b3),
            pl.BlockSpec((ns, T, QK_ROPE), b3),
        ],
        out_specs=pl.BlockSpec((ns * T, H * KV_LORA), b2),
        out_shape=jax.ShapeDtypeStruct((Bs * T, H * KV_LORA), BF16),
        compiler_params=_cparams("parallel"),
        name="attn_sample",
    )(qlat, qrot, cache_lat, cache_kr_t, new_lat, new_kr)


def _uv_kernel(x_ref, w_ref, o_ref):
    o_ref[...] = _dot(x_ref[...], w_ref[...]).astype(o_ref.dtype)


def _uv_proj(olat, w_uv_heads):
    M = olat.shape[0]
    H = N_HEADS_A
    return pl.pallas_call(
        _uv_kernel,
        grid=(H,),
        in_specs=[pl.BlockSpec((M, KV_LORA), lambda h: (0, h)),
                  pl.BlockSpec((None, KV_LORA, V_DIM), lambda h: (h, 0, 0))],
        out_specs=pl.BlockSpec((M, V_DIM), lambda h: (0, h)),
        out_shape=jax.ShapeDtypeStruct((M, H * V_DIM), BF16),
        compiler_params=_cparams("parallel"),
        name="uv_proj",
    )(olat, w_uv_heads)


def _split3(v):
    hi = v.astype(BF16)
    r1 = v - hi.astype(F32)
    mid = r1.astype(BF16)
    lo = (r1 - mid.astype(F32)).astype(BF16)
    return jnp.concatenate([hi, mid, lo], axis=1)


def _ssd_kernel(x_ref, z_ref, b_ref, c_ref, dt_ref, cst_ref, cw_ref, cbias_ref, dtb_ref, alog_ref,
                dskx_ref, nssm_ref, e64_ref, e128_ref, s0_ref,
                y_ref, sout_ref, xbuf, bbuf, cbuf, xs_scr, bm_scr, cm_scr, s_scr, *, q_len, n_chunks):
    c = pl.program_id(1)
    Q = q_len
    P = HEAD_DIM_S
    G, R, N, W = N_GROUPS_S, HEADS_PER_GROUP, D_STATE, GROUP_W
    h0 = HIST - (CONV_W - 1)

    @pl.when(c == 0)
    def _():
        for buf, lo, hi in ((xbuf, 0, D_INNER), (bbuf, D_INNER, D_INNER + BC_W), (cbuf, D_INNER + BC_W, CONV_DIM)):
            buf[0:h0, :] = jnp.zeros((h0, hi - lo), F32)
            buf[h0:HIST, :] = cst_ref[:, lo:hi]
        s_scr[...] = s0_ref[...]

    def conv(hist, cur_ref, lo, hi):
        cur = cur_ref[...]
        out = _causal_conv_silu(hist[...], cur, cw_ref[:, lo:hi], cbias_ref[:, lo:hi])
        hist[...] = cur[Q - HIST:, :]
        return out

    xs_scr[...] = conv(xbuf, x_ref, 0, D_INNER)
    bm_scr[...] = conv(bbuf, b_ref, D_INNER, D_INNER + BC_W).astype(BF16)
    cm_scr[...] = conv(cbuf, c_ref, D_INNER + BC_W, CONV_DIM).astype(BF16)

    dt_all = jax.nn.softplus(dt_ref[...] + dtb_ref[...])
    a_all = dt_all * (-LOG2E * jnp.exp(alog_ref[...]))
    ri = lax.broadcasted_iota(jnp.int32, (Q, Q), 0)
    ci = lax.broadcasted_iota(jnp.int32, (Q, Q), 1)
    causal = ci <= ri
    hp = lax.Precision.HIGHEST
    acs_all = jnp.dot(causal.astype(F32), a_all, precision=hp, preferred_element_type=F32)
    acs_row = lax.dot_general(a_all, (ri <= ci).astype(F32), (((0,), (0,)), ((), ())),
                              precision=hp, preferred_element_type=F32)
    a_end = acs_all[Q - 1:Q, :]
    eacs_all = jnp.exp2(acs_all)
    w_all = dt_all * jnp.exp2(a_end - acs_all)
    dec_all = jnp.exp2(a_end)
    left = lax.broadcasted_iota(jnp.int32, (Q, 128), 1) < P
    e64 = e64_ref[...]
    e128 = e128_ref[...]

    for g in range(G):
        h_lo = g * R
        dtx = _dot(_split3(dt_all[:, h_lo:h_lo + R]), e64)
        eacsx = _dot(_split3(eacs_all[:, h_lo:h_lo + R]), e64)
        wx = _dot(_split3(w_all[:, h_lo:h_lo + R]), e64)
        acsb = _dot(_split3(acs_all[:, h_lo:h_lo + R]), e128)
        xs = xs_scr[:, g * W:(g + 1) * W]
        bm = bm_scr[:, g * N:(g + 1) * N]
        cm = cm_scr[:, g * N:(g + 1) * N]
        xdt = xs * dtx
        cb = _dot_nt(cm, bm)
        y_off = _dot_nt(cm, s_scr[g].astype(BF16))
        ys = []
        for j in range(R // 2):
            ms = []
            for r in (2 * j, 2 * j + 1):
                seg = acsb[:, r * 128:r * 128 + Q] - acs_row[h_lo + r:h_lo + r + 1, :]
                ms.append((cb * jnp.exp2(jnp.where(causal, seg, -jnp.inf))).astype(BF16))
            xp = xdt[:, j * 128:(j + 1) * 128]
            rhs = jnp.concatenate([jnp.where(left, xp, 0.0), jnp.where(left, 0.0, xp)], axis=0).astype(BF16)
            ys.append(_dot(jnp.concatenate(ms, axis=1), rhs))
        y = jnp.concatenate(ys, axis=1) + y_off * eacsx + dskx_ref[:, g * W:(g + 1) * W] * xs

        st = _dot_tn((xs * wx).astype(BF16), bm)
        for r in range(R):
            dec = dec_all[:, h_lo + r:h_lo + r + 1]
            s_scr[g, r * P:(r + 1) * P, :] = dec * s_scr[g, r * P:(r + 1) * P, :] + st[r * P:(r + 1) * P, :]

        v = y * _silu(z_ref[:, g * W:(g + 1) * W])
        y_ref[:, g * W:(g + 1) * W] = _rms(v, nssm_ref[:, g * W:(g + 1) * W]).astype(y_ref.dtype)

    @pl.when(c == n_chunks - 1)
    def _():
        sout_ref[...] = s_scr[...]


def _head_expander(width):
    R = HEADS_PER_GROUP
    row_head = np.arange(3 * R) % R
    col_head = np.arange(R * width) // width
    return jnp.asarray((row_head[:, None] == col_head[None, :]).astype(np.float32)).astype(BF16)


def _ssd(xs3, zg3, conv_state, conv_w, conv_b, dt_bias, a_log, d_skip, n_ssm, state0, *, q_len):
    B, L, _ = xs3.shape
    G, N = N_GROUPS_S, D_STATE
    Q = q_len
    nc = L // Q
    pad = 128 - N_HEADS_S
    const = lambda b, c: (0, 0)
    in_specs = [
        pl.BlockSpec((None, Q, D_INNER), lambda b, c: (b, c, COL_X // D_INNER)),
        pl.BlockSpec((None, Q, D_INNER), lambda b, c: (b, c, COL_Z // D_INNER)),
        pl.BlockSpec((None, Q, BC_W), lambda b, c: (b, c, COL_B // BC_W)),
        pl.BlockSpec((None, Q, BC_W), lambda b, c: (b, c, COL_C // BC_W)),
        pl.BlockSpec((None, Q, 128), lambda b, c: (b, c, (CONV_DIM + COL_DT) // 128)),
        pl.BlockSpec((None, CONV_W - 1, CONV_DIM), lambda b, c: (b, 0, 0)),
        pl.BlockSpec((CONV_W, CONV_DIM), const),
        pl.BlockSpec((1, CONV_DIM), const),
        pl.BlockSpec((1, 128), const),
        pl.BlockSpec((1, 128), const),
        pl.BlockSpec((1, D_INNER), const),
        pl.BlockSpec((1, D_INNER), const),
        pl.BlockSpec((3 * HEADS_PER_GROUP, GROUP_W), const),
        pl.BlockSpec((3 * HEADS_PER_GROUP, HEADS_PER_GROUP * 128), const),
        pl.BlockSpec((None, G, GROUP_W, N), lambda b, c: (b, 0, 0, 0)),
    ]
    args = [xs3, zg3, xs3, xs3, xs3, conv_state, conv_w, conv_b.reshape(1, CONV_DIM),
            jnp.pad(dt_bias, (0, pad)).reshape(1, 128), jnp.pad(a_log, (0, pad)).reshape(1, 128),
            jnp.repeat(d_skip, HEAD_DIM_S).reshape(1, D_INNER), n_ssm.reshape(1, D_INNER),
            _head_expander(HEAD_DIM_S), _head_expander(128), state0.reshape(B, G, GROUP_W, N)]
    y, s_out = pl.pallas_call(
        functools.partial(_ssd_kernel, q_len=Q, n_chunks=nc),
        grid=(B, nc),
        in_specs=in_specs,
        out_specs=[pl.BlockSpec((None, Q, D_INNER), lambda b, c: (b, c, 0)),
                   pl.BlockSpec((None, G, GROUP_W, N), lambda b, c: (b, 0, 0, 0))],
        out_shape=[jax.ShapeDtypeStruct((B, L, D_INNER), BF16),
                   jax.ShapeDtypeStruct((B, G, GROUP_W, N), F32)],
        scratch_shapes=[pltpu.VMEM((HIST, D_INNER), F32), pltpu.VMEM((HIST, BC_W), F32),
                        pltpu.VMEM((HIST, BC_W), F32), pltpu.VMEM((Q, D_INNER), F32),
                        pltpu.VMEM((Q, BC_W), BF16), pltpu.VMEM((Q, BC_W), BF16),
                        pltpu.VMEM((G, GROUP_W, N), F32)],
        compiler_params=_cparams("parallel", "arbitrary"),
        name="ssd",
    )(*args)
    return y, s_out.reshape(B, N_HEADS_S, HEAD_DIM_S, N)


def _merge_kernel(oa_ref, ys_ref, wa_ref, ws_ref, g0_ref, g1_ref, bg_ref, o_ref, *w_out_refs):
    wa, ws = wa_ref[...].astype(BF16), ws_ref[...].astype(BF16)
    if w_out_refs:
        w_out_refs[0][...], w_out_refs[1][...] = wa, ws
    attn = _dot(oa_ref[...], wa)
    ssm = _dot(ys_ref[...], ws)
    bg = bg_ref[...]
    g0 = jax.nn.sigmoid(g0_ref[...] + bg[0:1, :])
    g1 = jax.nn.sigmoid(g1_ref[...] + bg[1:2, :])
    o_ref[...] = (g0 * attn + g1 * ssm).astype(o_ref.dtype)


def _merge(o_attn, y_ssm, w_o_attn, w_o_ssm, proj, b_gate, *, tm=512, tn=512):
    M = o_attn.shape[0]
    tm = min(tm, M)
    emit = w_o_attn.dtype != BF16
    assert not emit or M == tm
    g0blk = COL_GATE // tn
    g1blk = (COL_GATE + D_MODEL) // tn
    col = lambda i, j: (0, j)
    w_specs = [pl.BlockSpec((w_o_attn.shape[0], tn), col), pl.BlockSpec((D_INNER, tn), col)]
    out_specs = [pl.BlockSpec((tm, tn), lambda i, j: (i, j))]
    out_shape = [jax.ShapeDtypeStruct((M, D_MODEL), BF16)]
    if emit:
        out_specs += w_specs
        out_shape += [jax.ShapeDtypeStruct(w.shape, BF16) for w in (w_o_attn, w_o_ssm)]
    out = pl.pallas_call(
        _merge_kernel,
        grid=(M // tm, D_MODEL // tn),
        in_specs=[
            pl.BlockSpec((tm, o_attn.shape[1]), lambda i, j: (i, 0)),
            pl.BlockSpec((tm, D_INNER), lambda i, j: (i, 0)),
            *w_specs,
            pl.BlockSpec((tm, tn), lambda i, j: (i, g0blk + j)),
            pl.BlockSpec((tm, tn), lambda i, j: (i, g1blk + j)),
            pl.BlockSpec((2, tn), col),
        ],
        out_specs=out_specs,
        out_shape=out_shape,
        compiler_params=_cparams("parallel", "arbitrary"),
        name="merge",
    )(o_attn, y_ssm, w_o_attn, w_o_ssm, proj, proj, b_gate)
    return out if emit else out[0]


def _outproj_kernel(m_ref, w_ref, x_ref, o_ref, *w_out_refs):
    w = w_ref[...].astype(BF16)
    if w_out_refs:
        w_out_refs[0][...] = w
    o_ref[...] = x_ref[...] + _dot(m_ref[...], w)


def _outproj(merged, w_out, x, *, tm=512, tn=512):
    M = x.shape[0]
    tm = min(tm, M)
    emit = w_out.dtype != BF16
    assert not emit or M == tm
    w_spec = pl.BlockSpec((D_MODEL, tn), lambda i, j: (0, j))
    out_specs = [pl.BlockSpec((tm, tn), lambda i, j: (i, j))]
    out_shape = [jax.ShapeDtypeStruct((M, D_MODEL), F32)]
    if emit:
        out_specs.append(w_spec)
        out_shape.append(jax.ShapeDtypeStruct(w_out.shape, BF16))
    out = pl.pallas_call(
        _outproj_kernel,
        grid=(M // tm, D_MODEL // tn),
        in_specs=[
            pl.BlockSpec((tm, D_MODEL), lambda i, j: (i, 0)),
            w_spec,
            pl.BlockSpec((tm, tn), lambda i, j: (i, j)),
        ],
        out_specs=out_specs,
        out_shape=out_shape,
        compiler_params=_cparams("parallel", "arbitrary"),
        name="outproj",
    )(merged, w_out, x)
    return out if emit else out[0]


def _swap_halves(w):
    half = QK_ROPE // 2
    return jnp.concatenate([w[..., half:], w[..., :half]], axis=-1)


def _rope_table(pos):
    half = QK_ROPE // 2
    inv = np.power(ROPE_THETA, -np.arange(half, dtype=np.float64) / half)
    ang = np.asarray(pos, np.float64)[:, None] * inv[None, :]
    cos, sin = np.cos(ang), np.sin(ang)
    return np.concatenate([cos, cos, -sin, sin], axis=-1).astype(np.float32)


def kernel(x_prompt, x_sample, cache_kv_latent, cache_k_rope, state_ssm, state_conv, norm_ffn1, w_ffn1_gate, w_ffn1_up, w_ffn1_down, norm_mix, w_in, b_gate, norm_q_a, w_q_b, norm_kv_a, w_kv_b, conv_w, conv_b, dt_bias, a_log, d_skip, norm_ssm, w_o_attn, w_o_ssm, w_out, norm_ffn2, w_ffn2_gate, w_ffn2_up, w_ffn2_down, norm_final):
    Bp, S, D = x_prompt.shape
    Bs, T, _ = x_sample.shape
    past = cache_kv_latent.shape[2]
    H = N_HEADS_A
    l = 0

    wi = w_in[l]
    o_kr = Q_LORA + KV_LORA
    o_z = o_kr + QK_ROPE
    o_x = o_z + D_INNER
    o_dt = o_x + CONV_DIM
    o_gate = o_dt + N_HEADS_S
    w_small = jnp.concatenate([wi[:, :o_z], _swap_halves(wi[:, o_kr:o_z]), wi[:, o_dt:o_gate],
                               jnp.zeros((D, 128 - N_HEADS_S), F32)], axis=1)
    tc = 512
    xs_regions = ((0, CONV_DIM // tc, o_x - 64, 64), (CONV_DIM // tc, XS_W // tc, None, None))
    zg_regions = ((0, D_INNER // tc, o_z - 64, 64), (D_INNER // tc, ZG_W // tc, o_gate - 128, 128))
    wq3 = w_q_b[l].reshape(Q_LORA, H, QK_NOPE + QK_ROPE)
    wq = jnp.concatenate([wq3, _swap_halves(wq3[..., QK_NOPE:])], axis=-1).reshape(Q_LORA, H * QK_HEAD).astype(BF16)
    wkv3 = w_kv_b[l].reshape(KV_LORA, H, QK_NOPE + V_DIM)
    w_uk = wkv3[..., :QK_NOPE]
    w_uv = wkv3[..., QK_NOPE:]
    w_uk_flat = w_uk.reshape(KV_LORA, H * QK_NOPE).astype(BF16)
    w_uv_flat = w_uv.reshape(KV_LORA, H * V_DIM).astype(BF16)
    w_uk_t = w_uk.transpose(1, 2, 0).astype(BF16)
    w_uv_h = w_uv.transpose(1, 0, 2).astype(BF16)
    def newest_conv_rows(xs, batch):
        xs3 = xs.reshape(batch, -1, XS_W)
        return xs3[:, xs3.shape[1] - (CONV_W - 1):, :CONV_DIM]

    def ssd(xs, zg, batch, conv_state, state, q_len):
        y, new_state = _ssd(xs.reshape(batch, -1, XS_W), zg.reshape(batch, -1, ZG_W), conv_state,
                            conv_w[l], conv_b[l], dt_bias[l], a_log[l], d_skip[l], norm_ssm[l], state, q_len=q_len)
        return y.reshape(-1, D_INNER), new_state

    x1s, wg1, wu1, wd1 = _ffn(x_sample.reshape(Bs * T, D), norm_ffn1[l], w_ffn1_gate[l], w_ffn1_up[l],
                              w_ffn1_down[l], tf=256)
    xss, w_xs = _inproj_cast(x1s, norm_mix[l], wi, w_small, xs_regions, tn=tc)
    zgs, w_zg = _inproj_cast(x1s, norm_mix[l], wi, w_small, zg_regions, tn=tc)
    s_conv = newest_conv_rows(xss, Bs)
    cs_s = jnp.asarray(np.tile(_rope_table(past + np.arange(T)), (Bs, 1)))
    s_lat, s_kr, qlat, qrot = _prep(xss, cs_s, norm_q_a[l], norm_kv_a[l], wq, (w_uk_t,), prompt=False)
    o_lat = _attn_sample(qlat, qrot, cache_kv_latent[l], jnp.swapaxes(cache_k_rope[l], 1, 2),
                         s_lat.reshape(Bs, T, KV_LORA), s_kr.reshape(Bs, T, QK_ROPE),
                         streams=2 if Bs % 2 == 0 else 1)
    o_attn_s = _uv_proj(o_lat, w_uv_h)
    y_ssm_s, s_ssm = ssd(xss, zgs, Bs, state_conv[l], state_ssm[l], T)
    merged_s, woa, wos = _merge(o_attn_s, y_ssm_s, w_o_attn[l], w_o_ssm[l], zgs, b_gate[l], tm=1024, tn=256)
    x2s, wo = _outproj(merged_s, w_out[l], x1s, tm=1024, tn=1024)
    y_sample, wg2, wu2, wd2 = _ffn(x2s, norm_ffn2[l], w_ffn2_gate[l], w_ffn2_up[l], w_ffn2_down[l], norm_final,
                                   tf=256)
    y_sample = y_sample.reshape(Bs, T, D)

    x1 = _ffn(x_prompt.reshape(Bp * S, D), norm_ffn1[l], wg1, wu1, wd1)
    xs = _inproj(x1, norm_mix[l], w_xs, tm=1024, tn=SMALL_W)
    zg = _inproj(x1, norm_mix[l], w_zg, tm=1024, tn=1024)
    p_conv = newest_conv_rows(xs, Bp)
    cs_p = jnp.asarray(_rope_table(np.arange(S)))
    p_lat, p_kr, q, k, v = _prep(xs, cs_p, norm_q_a[l], norm_kv_a[l], wq, (w_uk_flat, w_uv_flat), prompt=True)
    o_attn = _attn_prompt(q.reshape(Bp, S, -1), k.reshape(Bp, S, -1), v.reshape(Bp, S, -1)).reshape(Bp * S, -1)
    y_ssm, p_ssm = ssd(xs, zg, Bp, jnp.zeros((Bp, CONV_W - 1, CONV_DIM), F32),
                       jnp.zeros((Bp, N_HEADS_S, HEAD_DIM_S, D_STATE), F32), 128)
    merged = _merge(o_attn, y_ssm, woa, wos, zg, b_gate[l], tm=1024, tn=512)
    x2 = _outproj(merged, wo, x1, tm=1024, tn=1024)
    y_prompt = _ffn(x2, norm_ffn2[l], wg2, wu2, wd2, norm_final).reshape(Bp, S, D)

    return (y_prompt, y_sample,
            p_lat.reshape(1, Bp, S, KV_LORA), p_kr.reshape(1, Bp, S, QK_ROPE), p_ssm[None], p_conv[None],
            s_lat.reshape(1, Bs, T, KV_LORA), s_kr.reshape(1, Bs, T, QK_ROPE), s_ssm[None], s_conv[None])
```

```python
import functools
import math

import jax
import jax.numpy as jnp
import numpy as np
from jax import lax
from jax.experimental import pallas as pl
from jax.experimental.pallas import tpu as pltpu

F32 = jnp.float32
BF16 = jnp.bfloat16

D_MODEL = 2048
CHUNK = 64
N_HEADS_A = 16
Q_LORA = 768
KV_LORA = 512
QK_NOPE = 128
QK_ROPE = 64
V_DIM = 128
ROPE_THETA = 10000.0
ATTN_SCALE = (QK_NOPE + QK_ROPE) ** -0.5
D_INNER = 2 * D_MODEL
HEAD_DIM_S = 64
N_HEADS_S = D_INNER // HEAD_DIM_S
N_GROUPS_S = 8
HEADS_PER_GROUP = N_HEADS_S // N_GROUPS_S
GROUP_W = D_INNER // N_GROUPS_S
D_STATE = 128
CONV_W = 4
CONV_DIM = D_INNER + 2 * N_GROUPS_S * D_STATE
D_FF = 5632
EPS = 1e-6

QK_HEAD = 256
COL_QA = 0
COL_LAT = Q_LORA
COL_KR = COL_LAT + KV_LORA
COL_DT = COL_KR + 2 * QK_ROPE
SMALL_W = COL_DT + 128
XS_W = CONV_DIM + SMALL_W
BC_W = N_GROUPS_S * D_STATE
COL_X = 0
COL_B = D_INNER
COL_C = COL_B + BC_W
COL_Z = 0
COL_GATE = D_INNER
ZG_W = COL_GATE + 2 * D_MODEL
LOG2E = math.log2(math.e)

VMEM_LIMIT = 60 * 1024 * 1024


def _cparams(*sem):
    return pltpu.CompilerParams(dimension_semantics=sem, vmem_limit_bytes=VMEM_LIMIT)


def _rms(x, g):
    return x * lax.rsqrt(jnp.mean(x * x, axis=-1, keepdims=True) + EPS) * g


def _silu(x):
    h = 0.5 * x
    return h + h * jnp.tanh(h)


def _dot(a, b):
    return jnp.dot(a, b, preferred_element_type=F32)


def _dot_nt(a, b):
    return lax.dot_general(a, b, (((1,), (1,)), ((), ())), preferred_element_type=F32)


def _dot_tn(a, b):
    return lax.dot_general(a, b, (((0,), (0,)), ((), ())), preferred_element_type=F32)


def _ffn_kernel(*refs, final_norm, emit_bf16):
    refs = list(refs)
    x_ref, g_ref, wg_ref, wu_ref, wd_ref = refs[:5]
    gf_ref = refs[5] if final_norm else None
    o_ref = refs[5 + final_norm]
    h_scr = refs[-1]
    j = pl.program_id(1)

    @pl.when(j == 0)
    def _():
        x = x_ref[...]
        h_scr[...] = _rms(x, g_ref[...]).astype(BF16)
        o_ref[...] = x

    wg, wu, wd = wg_ref[...].astype(BF16), wu_ref[...].astype(BF16), wd_ref[...].astype(BF16)
    if emit_bf16:
        wg_out, wu_out, wd_out = refs[6 + final_norm:9 + final_norm]
        wg_out[...], wu_out[...], wd_out[...] = wg, wu, wd
    h = h_scr[...]
    a = (0.5 * _silu(_dot(h, wg))) * _dot(h, wu)
    o_ref[...] += _dot(a.astype(BF16), wd)

    if final_norm:
        @pl.when(j == pl.num_programs(1) - 1)
        def _():
            o_ref[...] = _rms(o_ref[...], gf_ref[...])


def _ffn(x, g, wg, wu, wd, gf=None, *, tm=1024, tf=512):
    M, D = x.shape
    F = wg.shape[1]
    tm = min(tm, M)
    emit = wg.dtype != BF16
    assert not emit or M == tm
    col = lambda i, j: (0, j)
    row = lambda i, j: (j, 0)
    in_specs = [
        pl.BlockSpec((tm, D), lambda i, j: (i, 0)),
        pl.BlockSpec((1, D), lambda i, j: (0, 0)),
        pl.BlockSpec((D, tf), col),
        pl.BlockSpec((D, tf), col),
        pl.BlockSpec((tf, D), row),
    ]
    args = [x, g.reshape(1, D), wg, wu, wd]
    if gf is not None:
        in_specs.append(pl.BlockSpec((1, D), lambda i, j: (0, 0)))
        args.append(gf.reshape(1, D))
    out_specs = [pl.BlockSpec((tm, D), lambda i, j: (i, 0))]
    out_shape = [jax.ShapeDtypeStruct((M, D), F32)]
    if emit:
        out_specs += [pl.BlockSpec((D, tf), col), pl.BlockSpec((D, tf), col), pl.BlockSpec((tf, D), row)]
        out_shape += [jax.ShapeDtypeStruct(w.shape, BF16) for w in (wg, wu, wd)]
    out = pl.pallas_call(
        functools.partial(_ffn_kernel, final_norm=gf is not None, emit_bf16=emit),
        grid=(M // tm, F // tf),
        in_specs=in_specs,
        out_specs=out_specs,
        out_shape=out_shape,
        scratch_shapes=[pltpu.VMEM((tm, D), BF16)],
        compiler_params=_cparams("parallel", "arbitrary"),
        name="ffn",
    )(*args)
    return out if emit else out[0]


def _relayout_kernel(w_ref, xs_ref, zg_ref):
    w = w_ref[...]
    o_lat = Q_LORA
    o_kr = o_lat + KV_LORA
    o_z = o_kr + QK_ROPE
    o_x = o_z + D_INNER
    o_bc = o_x + D_INNER
    o_dt = o_bc + 2 * BC_W
    o_gate = o_dt + N_HEADS_S
    half = QK_ROPE // 2
    small = jnp.concatenate(
        [w[:, :o_z], w[:, o_kr + half:o_z], w[:, o_kr:o_kr + half], w[:, o_dt:o_gate],
         jnp.zeros((w.shape[0], 128 - N_HEADS_S), F32)], axis=1)
    xs_ref[:, :CONV_DIM] = w[:, o_x:o_dt].astype(BF16)
    xs_ref[:, CONV_DIM:] = small.astype(BF16)
    zg_ref[:, COL_Z:COL_Z + D_INNER] = w[:, o_z:o_x].astype(BF16)
    zg_ref[:, COL_GATE:] = w[:, o_gate:].astype(BF16)


def _relayout_w_in(w, *, tk=128):
    K, N = w.shape
    widths = (XS_W, ZG_W)
    return pl.pallas_call(
        _relayout_kernel,
        grid=(K // tk,),
        in_specs=[pl.BlockSpec((tk, N), lambda i: (i, 0))],
        out_specs=[pl.BlockSpec((tk, n), lambda i: (i, 0)) for n in widths],
        out_shape=[jax.ShapeDtypeStruct((K, n), BF16) for n in widths],
        compiler_params=_cparams("parallel"),
        name="relayout_w_in",
    )(w)


HIST = 8


def _causal_conv_silu(hist, cur, w, bias):
    ext = jnp.concatenate([hist, cur], axis=0)
    acc = bias
    for k in range(CONV_W - 1):
        acc = acc + w[k:k + 1, :] * pltpu.roll(ext, CONV_W - 1 - k, axis=0)[HIST:, :]
    return _silu(acc + w[CONV_W - 1:CONV_W, :] * cur)


def _inproj_kernel(x_ref, g_ref, w_ref, o_ref, h_scr):
    @pl.when(pl.program_id(1) == 0)
    def _():
        h_scr[...] = _rms(x_ref[...], g_ref[...]).astype(BF16)

    o_ref[...] = _dot(h_scr[...], w_ref[...])


def _inproj(x, g, w, *, tm, tn):
    M, D = x.shape
    N = w.shape[1]
    tm = min(tm, M)
    return pl.pallas_call(
        _inproj_kernel,
        grid=(M // tm, N // tn),
        in_specs=[
            pl.BlockSpec((tm, D), lambda i, j: (i, 0)),
            pl.BlockSpec((1, D), lambda i, j: (0, 0)),
            pl.BlockSpec((D, tn), lambda i, j: (0, j)),
        ],
        out_specs=pl.BlockSpec((tm, tn), lambda i, j: (i, j)),
        out_shape=jax.ShapeDtypeStruct((M, N), F32),
        scratch_shapes=[pltpu.VMEM((tm, D), BF16)],
        compiler_params=_cparams("parallel", "arbitrary"),
        name="inproj",
    )(x, g.reshape(1, D), w)


def _rope_pair(blk, cs, keep):
    prod = blk * cs
    return jnp.where(keep, prod + pltpu.roll(prod, 64, axis=1), 0.0)


def _prep_common(small_ref, cs_ref, nq_ref, nkv_ref, wq_ref, ckv_ref, kr_ref):
    sm = small_ref[...]
    cs = cs_ref[...]
    keep = lax.broadcasted_iota(jnp.int32, cs.shape, 1) < QK_ROPE
    cq = _rms(sm[:, COL_QA:COL_QA + Q_LORA], nq_ref[...]).astype(BF16)
    q = _dot(cq, wq_ref[...]) * (ATTN_SCALE * LOG2E)
    ckv = _rms(sm[:, COL_LAT:COL_LAT + KV_LORA], nkv_ref[...])
    ckv_ref[...] = ckv
    krot = _rope_pair(sm[:, COL_KR:COL_KR + 128], cs, keep)
    kr_ref[...] = krot[:, :QK_ROPE]
    return q, ckv, krot, cs, keep


def _prep_prompt_kernel(small_ref, cs_ref, nq_ref, nkv_ref, wq_ref, wuk_ref, wuv_ref,
                        ckv_ref, kr_ref, q_ref, k_ref, v_ref):
    q, ckv, krot, cs, keep = _prep_common(small_ref, cs_ref, nq_ref, nkv_ref, wq_ref, ckv_ref, kr_ref)
    ckv_b = ckv.astype(BF16)
    kn = _dot(ckv_b, wuk_ref[...])
    krot_b = krot.astype(BF16)
    for h in range(N_HEADS_A):
        c0 = h * QK_HEAD
        q_ref[:, c0:c0 + 128] = q[:, c0:c0 + 128].astype(BF16)
        q_ref[:, c0 + 128:c0 + 256] = _rope_pair(q[:, c0 + 128:c0 + 256], cs, keep).astype(BF16)
        k_ref[:, c0:c0 + 128] = kn[:, h * 128:(h + 1) * 128].astype(BF16)
        k_ref[:, c0 + 128:c0 + 256] = krot_b
    v_ref[...] = _dot(ckv_b, wuv_ref[...]).astype(BF16)


def _prep_sample_kernel(small_ref, cs_ref, nq_ref, nkv_ref, wq_ref, wukt_ref,
                        ckv_ref, kr_ref, qlat_ref, qrot_ref):
    q, _, _, cs, keep = _prep_common(small_ref, cs_ref, nq_ref, nkv_ref, wq_ref, ckv_ref, kr_ref)
    for h in range(N_HEADS_A):
        c0 = h * QK_HEAD
        qn = q[:, c0:c0 + 128].astype(BF16)
        qlat_ref[:, h * KV_LORA:(h + 1) * KV_LORA] = _dot(qn, wukt_ref[h]).astype(BF16)
        qrot_ref[:, h * 128:(h + 1) * 128] = _rope_pair(q[:, c0 + 128:c0 + 256], cs, keep).astype(BF16)


def _prep(proj, cs_table, nq, nkv, wq, extra_w, *, prompt, tm=512):
    M = proj.shape[0]
    tm = min(tm, M)
    n_cs = cs_table.shape[0] // tm
    H = N_HEADS_A
    row = lambda i: (i, 0)
    const2 = lambda i: (0, 0)
    in_specs = [
        pl.BlockSpec((tm, SMALL_W), lambda i: (i, CONV_DIM // SMALL_W)),
        pl.BlockSpec((tm, 128), lambda i: (i % n_cs, 0)),
        pl.BlockSpec((1, Q_LORA), const2),
        pl.BlockSpec((1, KV_LORA), const2),
        pl.BlockSpec((Q_LORA, H * QK_HEAD), const2),
    ]
    out_specs = [pl.BlockSpec((tm, KV_LORA), row), pl.BlockSpec((tm, QK_ROPE), row)]
    out_shape = [jax.ShapeDtypeStruct((M, KV_LORA), F32), jax.ShapeDtypeStruct((M, QK_ROPE), F32)]
    if prompt:
        w_uk, w_uv = extra_w
        in_specs += [pl.BlockSpec((KV_LORA, H * QK_NOPE), const2), pl.BlockSpec((KV_LORA, H * V_DIM), const2)]
        out_specs += [pl.BlockSpec((tm, H * QK_HEAD), row), pl.BlockSpec((tm, H * QK_HEAD), row),
                      pl.BlockSpec((tm, H * V_DIM), row)]
        out_shape += [jax.ShapeDtypeStruct((M, H * QK_HEAD), BF16), jax.ShapeDtypeStruct((M, H * QK_HEAD), BF16),
                      jax.ShapeDtypeStruct((M, H * V_DIM), BF16)]
        body, extra = _prep_prompt_kernel, [w_uk, w_uv]
    else:
        (w_ukt,) = extra_w
        in_specs += [pl.BlockSpec((H, QK_NOPE, KV_LORA), lambda i: (0, 0, 0))]
        out_specs += [pl.BlockSpec((tm, H * KV_LORA), row), pl.BlockSpec((tm, H * 128), row)]
        out_shape += [jax.ShapeDtypeStruct((M, H * KV_LORA), BF16), jax.ShapeDtypeStruct((M, H * 128), BF16)]
        body, extra = _prep_sample_kernel, [w_ukt]
    return pl.pallas_call(
        body,
        grid=(M // tm,),
        in_specs=in_specs,
        out_specs=out_specs,
        out_shape=out_shape,
        compiler_params=_cparams("parallel"),
        name="prep_prompt" if prompt else "prep_sample",
    )(proj, cs_table, nq.reshape(1, -1), nkv.reshape(1, -1), wq, *extra)


def _attn_prompt_kernel(q_ref, k_ref, v_ref, bias_ref, o_ref, s_scr, p_scr, *, seq, tq):
    for qi in reversed(range(seq // tq)):
        kv = (qi + 1) * tq
        s_buf = s_scr.at[qi % 2]
        p_buf = p_scr.at[qi % 2]
        q = q_ref[qi * tq:kv, :]
        s_buf[:, :kv] = _dot_nt(q, k_ref[:kv, :])
        s_buf[:, kv - tq:kv] = s_buf[:, kv - tq:kv] + bias_ref[...]
        m = jnp.max(s_buf[:, :kv], axis=-1, keepdims=True)
        p = jnp.exp2(s_buf[:, :kv] - m)
        l = jnp.sum(p, axis=-1, keepdims=True)
        p_buf[:, :kv] = p.astype(BF16)
        acc = _dot(p_buf[:, :kv], v_ref[:kv, :])
        o_ref[qi * tq:kv, :] = (acc / l).astype(o_ref.dtype)


def _attn_prompt(q, k, v, *, tq=512):
    B, S, _ = q.shape
    H = N_HEADS_A
    chunk = np.arange(tq) // CHUNK
    bias = jnp.asarray(np.where(chunk[None, :] <= chunk[:, None], 0.0, -np.inf).astype(np.float32))
    return pl.pallas_call(
        functools.partial(_attn_prompt_kernel, seq=S, tq=tq),
        grid=(B, H),
        in_specs=[
            pl.BlockSpec((None, S, QK_HEAD), lambda b, h: (b, 0, h)),
            pl.BlockSpec((None, S, QK_HEAD), lambda b, h: (b, 0, h)),
            pl.BlockSpec((None, S, V_DIM), lambda b, h: (b, 0, h)),
            pl.BlockSpec((tq, tq), lambda b, h: (0, 0)),
        ],
        out_specs=pl.BlockSpec((None, S, V_DIM), lambda b, h: (b, 0, h)),
        out_shape=jax.ShapeDtypeStruct((B, S, H * V_DIM), BF16),
        scratch_shapes=[pltpu.VMEM((2, tq, S), F32), pltpu.VMEM((2, tq, S), BF16)],
        compiler_params=_cparams("parallel", "parallel"),
        name="attn_prompt",
    )(q, k, v, bias)


def _attn_sample_kernel(qlat_ref, qrot_ref, clat_ref, ckrt_ref, nlat_ref, nkr_ref, o_ref):
    H = N_HEADS_A
    T = nlat_ref.shape[1]
    for sb in range(clat_ref.shape[0]):
        rows = slice(sb * T, (sb + 1) * T)
        ql = jnp.concatenate([qlat_ref[rows, h * KV_LORA:(h + 1) * KV_LORA] for h in range(H)], axis=0)
        qr = jnp.concatenate([qrot_ref[rows, h * 128:h * 128 + QK_ROPE] for h in range(H)], axis=0)
        lat = clat_ref[sb].astype(BF16)
        nlat = nlat_ref[sb].astype(BF16)
        s1 = _dot_nt(ql, lat) + _dot(qr, ckrt_ref[sb].astype(BF16))
        s2 = _dot_nt(ql, nlat) + _dot_nt(qr, nkr_ref[sb].astype(BF16))
        m = jnp.maximum(jnp.max(s1, axis=-1, keepdims=True), jnp.max(s2, axis=-1, keepdims=True))
        p1 = jnp.exp2(s1 - m)
        p2 = jnp.exp2(s2 - m)
        l = jnp.sum(p1, axis=-1, keepdims=True) + jnp.sum(p2, axis=-1, keepdims=True)
        o = _dot(p1.astype(BF16), lat) + _dot(p2.astype(BF16), nlat)
        o = (o / l).astype(o_ref.dtype)
        for h in range(H):
            o_ref[rows, h * KV_LORA:(h + 1) * KV_LORA] = o[h * T:(h + 1) * T, :]


def _attn_sample(qlat, qrot, cache_lat, cache_kr_t, new_lat, new_kr, *, streams):
    Bs, P, _ = cache_lat.shape
    T = new_lat.shape[1]
    H = N_HEADS_A
    ns = streams
    b2 = lambda b: (b, 0)
    b3 = lambda b: (b, 0, 0)
    return pl.pallas_call(
        _attn_sample_kernel,
        grid=(Bs // ns,),
        in_specs=[
            pl.BlockSpec((ns * T, H * KV_LORA), b2),
            pl.BlockSpec((ns * T, H * 128), b2),
            pl.BlockSpec((ns, P, KV_LORA), b3),
            pl.BlockSpec((ns, QK_ROPE, P), b3),
            pl.BlockSpec((ns, T, KV_LORA), b3),
            pl.BlockSpec((ns, T, QK_ROPE), b3),
        ],
        out_specs=pl.BlockSpec((ns * T, H * KV_LORA), b2),
        out_shape=jax.ShapeDtypeStruct((Bs * T, H * KV_LORA), BF16),
        compiler_params=_cparams("parallel"),
        name="attn_sample",
    )(qlat, qrot, cache_lat, cache_kr_t, new_lat, new_kr)


def _uv_kernel(x_ref, w_ref, o_ref):
    o_ref[...] = _dot(x_ref[...], w_ref[...]).astype(o_ref.dtype)


def _uv_proj(olat, w_uv_heads):
    M = olat.shape[0]
    H = N_HEADS_A
    return pl.pallas_call(
        _uv_kernel,
        grid=(H,),
        in_specs=[pl.BlockSpec((M, KV_LORA), lambda h: (0, h)),
                  pl.BlockSpec((None, KV_LORA, V_DIM), lambda h: (h, 0, 0))],
        out_specs=pl.BlockSpec((M, V_DIM), lambda h: (0, h)),
        out_shape=jax.ShapeDtypeStruct((M, H * V_DIM), BF16),
        compiler_params=_cparams("parallel"),
        name="uv_proj",
    )(olat, w_uv_heads)


def _split3(v):
    hi = v.astype(BF16)
    r1 = v - hi.astype(F32)
    mid = r1.astype(BF16)
    lo = (r1 - mid.astype(F32)).astype(BF16)
    return jnp.concatenate([hi, mid, lo], axis=1)


def _ssd_kernel(x_ref, z_ref, b_ref, c_ref, dt_ref, cst_ref, cw_ref, cbias_ref, dtb_ref, alog_ref,
                dskx_ref, nssm_ref, e64_ref, e128_ref, s0_ref,
                y_ref, sout_ref, xbuf, bbuf, cbuf, xs_scr, bm_scr, cm_scr, s_scr, *, q_len, n_chunks):
    c = pl.program_id(1)
    Q = q_len
    P = HEAD_DIM_S
    G, R, N, W = N_GROUPS_S, HEADS_PER_GROUP, D_STATE, GROUP_W
    h0 = HIST - (CONV_W - 1)

    @pl.when(c == 0)
    def _():
        for buf, lo, hi in ((xbuf, 0, D_INNER), (bbuf, D_INNER, D_INNER + BC_W), (cbuf, D_INNER + BC_W, CONV_DIM)):
            buf[0:h0, :] = jnp.zeros((h0, hi - lo), F32)
            buf[h0:HIST, :] = cst_ref[:, lo:hi]
        s_scr[...] = s0_ref[...]

    def conv(hist, cur_ref, lo, hi):
        cur = cur_ref[...]
        out = _causal_conv_silu(hist[...], cur, cw_ref[:, lo:hi], cbias_ref[:, lo:hi])
        hist[...] = cur[Q - HIST:, :]
        return out

    xs_scr[...] = conv(xbuf, x_ref, 0, D_INNER)
    bm_scr[...] = conv(bbuf, b_ref, D_INNER, D_INNER + BC_W).astype(BF16)
    cm_scr[...] = conv(cbuf, c_ref, D_INNER + BC_W, CONV_DIM).astype(BF16)

    dt_all = jax.nn.softplus(dt_ref[...] + dtb_ref[...])
    a_all = dt_all * (-LOG2E * jnp.exp(alog_ref[...]))
    ri = lax.broadcasted_iota(jnp.int32, (Q, Q), 0)
    ci = lax.broadcasted_iota(jnp.int32, (Q, Q), 1)
    causal = ci <= ri
    hp = lax.Precision.HIGHEST
    acs_all = jnp.dot(causal.astype(F32), a_all, precision=hp, preferred_element_type=F32)
    acs_row = lax.dot_general(a_all, (ri <= ci).astype(F32), (((0,), (0,)), ((), ())),
                              precision=hp, preferred_element_type=F32)
    a_end = acs_all[Q - 1:Q, :]
    eacs_all = jnp.exp2(acs_all)
    w_all = dt_all * jnp.exp2(a_end - acs_all)
    dec_all = jnp.exp2(a_end)
    left = lax.broadcasted_iota(jnp.int32, (Q, 128), 1) < P
    e64 = e64_ref[...]
    e128 = e128_ref[...]

    for g in range(G):
        h_lo = g * R
        dtx = _dot(_split3(dt_all[:, h_lo:h_lo + R]), e64)
        eacsx = _dot(_split3(eacs_all[:, h_lo:h_lo + R]), e64)
        wx = _dot(_split3(w_all[:, h_lo:h_lo + R]), e64)
        acsb = _dot(_split3(acs_all[:, h_lo:h_lo + R]), e128)
        xs = xs_scr[:, g * W:(g + 1) * W]
        bm = bm_scr[:, g * N:(g + 1) * N]
        cm = cm_scr[:, g * N:(g + 1) * N]
        xdt = xs * dtx
        cb = _dot_nt(cm, bm)
        y_off = _dot_nt(cm, s_scr[g].astype(BF16))
        ys = []
        for j in range(R // 2):
            ms = []
            for r in (2 * j, 2 * j + 1):
                seg = acsb[:, r * 128:r * 128 + Q] - acs_row[h_lo + r:h_lo + r + 1, :]
                ms.append((cb * jnp.exp2(jnp.where(causal, seg, -jnp.inf))).astype(BF16))
            xp = xdt[:, j * 128:(j + 1) * 128]
            rhs = jnp.concatenate([jnp.where(left, xp, 0.0), jnp.where(left, 0.0, xp)], axis=0).astype(BF16)
            ys.append(_dot(jnp.concatenate(ms, axis=1), rhs))
        y = jnp.concatenate(ys, axis=1) + y_off * eacsx + dskx_ref[:, g * W:(g + 1) * W] * xs

        st = _dot_tn((xs * wx).astype(BF16), bm)
        for r in range(R):
            dec = dec_all[:, h_lo + r:h_lo + r + 1]
            s_scr[g, r * P:(r + 1) * P, :] = dec * s_scr[g, r * P:(r + 1) * P, :] + st[r * P:(r + 1) * P, :]

        v = y * _silu(z_ref[:, g * W:(g + 1) * W])
        y_ref[:, g * W:(g + 1) * W] = _rms(v, nssm_ref[:, g * W:(g + 1) * W]).astype(y_ref.dtype)

    @pl.when(c == n_chunks - 1)
    def _():
        sout_ref[...] = s_scr[...]


def _head_expander(width):
    R = HEADS_PER_GROUP
    row_head = np.arange(3 * R) % R
    col_head = np.arange(R * width) // width
    return jnp.asarray((row_head[:, None] == col_head[None, :]).astype(np.float32)).astype(BF16)


def _ssd(xs3, zg3, conv_state, conv_w, conv_b, dt_bias, a_log, d_skip, n_ssm, state0, *, q_len):
    B, L, _ = xs3.shape
    G, N = N_GROUPS_S, D_STATE
    Q = q_len
    nc = L // Q
    pad = 128 - N_HEADS_S
    const = lambda b, c: (0, 0)
    in_specs = [
        pl.BlockSpec((None, Q, D_INNER), lambda b, c: (b, c, COL_X // D_INNER)),
        pl.BlockSpec((None, Q, D_INNER), lambda b, c: (b, c, COL_Z // D_INNER)),
        pl.BlockSpec((None, Q, BC_W), lambda b, c: (b, c, COL_B // BC_W)),
        pl.BlockSpec((None, Q, BC_W), lambda b, c: (b, c, COL_C // BC_W)),
        pl.BlockSpec((None, Q, 128), lambda b, c: (b, c, (CONV_DIM + COL_DT) // 128)),
        pl.BlockSpec((None, CONV_W - 1, CONV_DIM), lambda b, c: (b, 0, 0)),
        pl.BlockSpec((CONV_W, CONV_DIM), const),
        pl.BlockSpec((1, CONV_DIM), const),
        pl.BlockSpec((1, 128), const),
        pl.BlockSpec((1, 128), const),
        pl.BlockSpec((1, D_INNER), const),
        pl.BlockSpec((1, D_INNER), const),
        pl.BlockSpec((3 * HEADS_PER_GROUP, GROUP_W), const),
        pl.BlockSpec((3 * HEADS_PER_GROUP, HEADS_PER_GROUP * 128), const),
        pl.BlockSpec((None, G, GROUP_W, N), lambda b, c: (b, 0, 0, 0)),
    ]
    args = [xs3, zg3, xs3, xs3, xs3, conv_state, conv_w, conv_b.reshape(1, CONV_DIM),
            jnp.pad(dt_bias, (0, pad)).reshape(1, 128), jnp.pad(a_log, (0, pad)).reshape(1, 128),
            jnp.repeat(d_skip, HEAD_DIM_S).reshape(1, D_INNER), n_ssm.reshape(1, D_INNER),
            _head_expander(HEAD_DIM_S), _head_expander(128), state0.reshape(B, G, GROUP_W, N)]
    y, s_out = pl.pallas_call(
        functools.partial(_ssd_kernel, q_len=Q, n_chunks=nc),
        grid=(B, nc),
        in_specs=in_specs,
        out_specs=[pl.BlockSpec((None, Q, D_INNER), lambda b, c: (b, c, 0)),
                   pl.BlockSpec((None, G, GROUP_W, N), lambda b, c: (b, 0, 0, 0))],
        out_shape=[jax.ShapeDtypeStruct((B, L, D_INNER), BF16),
                   jax.ShapeDtypeStruct((B, G, GROUP_W, N), F32)],
        scratch_shapes=[pltpu.VMEM((HIST, D_INNER), F32), pltpu.VMEM((HIST, BC_W), F32),
                        pltpu.VMEM((HIST, BC_W), F32), pltpu.VMEM((Q, D_INNER), F32),
                        pltpu.VMEM((Q, BC_W), BF16), pltpu.VMEM((Q, BC_W), BF16),
                        pltpu.VMEM((G, GROUP_W, N), F32)],
        compiler_params=_cparams("parallel", "arbitrary"),
        name="ssd",
    )(*args)
    return y, s_out.reshape(B, N_HEADS_S, HEAD_DIM_S, N)


def _merge_kernel(oa_ref, ys_ref, wa_ref, ws_ref, g0_ref, g1_ref, bg_ref, o_ref, *w_out_refs):
    wa, ws = wa_ref[...].astype(BF16), ws_ref[...].astype(BF16)
    if w_out_refs:
        w_out_refs[0][...], w_out_refs[1][...] = wa, ws
    attn = _dot(oa_ref[...], wa)
    ssm = _dot(ys_ref[...], ws)
    bg = bg_ref[...]
    g0 = jax.nn.sigmoid(g0_ref[...] + bg[0:1, :])
    g1 = jax.nn.sigmoid(g1_ref[...] + bg[1:2, :])
    o_ref[...] = (g0 * attn + g1 * ssm).astype(o_ref.dtype)


def _merge(o_attn, y_ssm, w_o_attn, w_o_ssm, proj, b_gate, *, tm=512, tn=512):
    M = o_attn.shape[0]
    tm = min(tm, M)
    emit = w_o_attn.dtype != BF16
    assert not emit or M == tm
    g0blk = COL_GATE // tn
    g1blk = (COL_GATE + D_MODEL) // tn
    col = lambda i, j: (0, j)
    w_specs = [pl.BlockSpec((w_o_attn.shape[0], tn), col), pl.BlockSpec((D_INNER, tn), col)]
    out_specs = [pl.BlockSpec((tm, tn), lambda i, j: (i, j))]
    out_shape = [jax.ShapeDtypeStruct((M, D_MODEL), BF16)]
    if emit:
        out_specs += w_specs
        out_shape += [jax.ShapeDtypeStruct(w.shape, BF16) for w in (w_o_attn, w_o_ssm)]
    out = pl.pallas_call(
        _merge_kernel,
        grid=(M // tm, D_MODEL // tn),
        in_specs=[
            pl.BlockSpec((tm, o_attn.shape[1]), lambda i, j: (i, 0)),
            pl.BlockSpec((tm, D_INNER), lambda i, j: (i, 0)),
            *w_specs,
            pl.BlockSpec((tm, tn), lambda i, j: (i, g0blk + j)),
            pl.BlockSpec((tm, tn), lambda i, j: (i, g1blk + j)),
            pl.BlockSpec((2, tn), col),
        ],
        out_specs=out_specs,
        out_shape=out_shape,
        compiler_params=_cparams("parallel", "arbitrary"),
        name="merge",
    )(o_attn, y_ssm, w_o_attn, w_o_ssm, proj, proj, b_gate)
    return out if emit else out[0]


def _outproj_kernel(m_ref, w_ref, x_ref, o_ref, *w_out_refs):
    w = w_ref[...].astype(BF16)
    if w_out_refs:
        w_out_refs[0][...] = w
    o_ref[...] = x_ref[...] + _dot(m_ref[...], w)


def _outproj(merged, w_out, x, *, tm=512, tn=512):
    M = x.shape[0]
    tm = min(tm, M)
    emit = w_out.dtype != BF16
    assert not emit or M == tm
    w_spec = pl.BlockSpec((D_MODEL, tn), lambda i, j: (0, j))
    out_specs = [pl.BlockSpec((tm, tn), lambda i, j: (i, j))]
    out_shape = [jax.ShapeDtypeStruct((M, D_MODEL), F32)]
    if emit:
        out_specs.append(w_spec)
        out_shape.append(jax.ShapeDtypeStruct(w_out.shape, BF16))
    out = pl.pallas_call(
        _outproj_kernel,
        grid=(M // tm, D_MODEL // tn),
        in_specs=[
            pl.BlockSpec((tm, D_MODEL), lambda i, j: (i, 0)),
            w_spec,
            pl.BlockSpec((tm, tn), lambda i, j: (i, j)),
        ],
        out_specs=out_specs,
        out_shape=out_shape,
        compiler_params=_cparams("parallel", "arbitrary"),
        name="outproj",
    )(merged, w_out, x)
    return out if emit else out[0]


def _swap_halves(w):
    half = QK_ROPE // 2
    return jnp.concatenate([w[..., half:], w[..., :half]], axis=-1)


def _rope_table(pos):
    half = QK_ROPE // 2
    inv = np.power(ROPE_THETA, -np.arange(half, dtype=np.float64) / half)
    ang = np.asarray(pos, np.float64)[:, None] * inv[None, :]
    cos, sin = np.cos(ang), np.sin(ang)
    return np.concatenate([cos, cos, -sin, sin], axis=-1).astype(np.float32)


def kernel(x_prompt, x_sample, cache_kv_latent, cache_k_rope, state_ssm, state_conv, norm_ffn1, w_ffn1_gate, w_ffn1_up, w_ffn1_down, norm_mix, w_in, b_gate, norm_q_a, w_q_b, norm_kv_a, w_kv_b, conv_w, conv_b, dt_bias, a_log, d_skip, norm_ssm, w_o_attn, w_o_ssm, w_out, norm_ffn2, w_ffn2_gate, w_ffn2_up, w_ffn2_down, norm_final):
    Bp, S, D = x_prompt.shape
    Bs, T, _ = x_sample.shape
    past = cache_kv_latent.shape[2]
    H = N_HEADS_A
    l = 0

    w_xs, w_zg = _relayout_w_in(w_in[l])
    wq3 = w_q_b[l].reshape(Q_LORA, H, QK_NOPE + QK_ROPE)
    wq = jnp.concatenate([wq3, _swap_halves(wq3[..., QK_NOPE:])], axis=-1).reshape(Q_LORA, H * QK_HEAD).astype(BF16)
    wkv3 = w_kv_b[l].reshape(KV_LORA, H, QK_NOPE + V_DIM)
    w_uk = wkv3[..., :QK_NOPE]
    w_uv = wkv3[..., QK_NOPE:]
    w_uk_flat = w_uk.reshape(KV_LORA, H * QK_NOPE).astype(BF16)
    w_uv_flat = w_uv.reshape(KV_LORA, H * V_DIM).astype(BF16)
    w_uk_t = w_uk.transpose(1, 2, 0).astype(BF16)
    w_uv_h = w_uv.transpose(1, 0, 2).astype(BF16)
    def mixer_mid(x1, batch):
        xs = _inproj(x1, norm_mix[l], w_xs, tm=1024, tn=SMALL_W)
        zg = _inproj(x1, norm_mix[l], w_zg, tm=1024, tn=1024)
        xs3 = xs.reshape(batch, -1, XS_W)
        return xs, zg, xs3[:, xs3.shape[1] - (CONV_W - 1):, :CONV_DIM]


    def ssd(xs, zg, batch, conv_state, state, q_len):
        y, new_state = _ssd(xs.reshape(batch, -1, XS_W), zg.reshape(batch, -1, ZG_W), conv_state,
                            conv_w[l], conv_b[l], dt_bias[l], a_log[l], d_skip[l], norm_ssm[l], state, q_len=q_len)
        return y.reshape(-1, D_INNER), new_state

    x1s, wg1, wu1, wd1 = _ffn(x_sample.reshape(Bs * T, D), norm_ffn1[l], w_ffn1_gate[l], w_ffn1_up[l],
                              w_ffn1_down[l], tf=256)
    xss, zgs, s_conv = mixer_mid(x1s, Bs)
    cs_s = jnp.asarray(np.tile(_rope_table(past + np.arange(T)), (Bs, 1)))
    s_lat, s_kr, qlat, qrot = _prep(xss, cs_s, norm_q_a[l], norm_kv_a[l], wq, (w_uk_t,), prompt=False)
    o_lat = _attn_sample(qlat, qrot, cache_kv_latent[l], jnp.swapaxes(cache_k_rope[l], 1, 2),
                         s_lat.reshape(Bs, T, KV_LORA), s_kr.reshape(Bs, T, QK_ROPE),
                         streams=2 if Bs % 2 == 0 else 1)
    o_attn_s = _uv_proj(o_lat, w_uv_h)
    y_ssm_s, s_ssm = ssd(xss, zgs, Bs, state_conv[l], state_ssm[l], T)
    merged_s, woa, wos = _merge(o_attn_s, y_ssm_s, w_o_attn[l], w_o_ssm[l], zgs, b_gate[l], tm=1024, tn=256)
    x2s, wo = _outproj(merged_s, w_out[l], x1s, tm=1024, tn=1024)
    y_sample, wg2, wu2, wd2 = _ffn(x2s, norm_ffn2[l], w_ffn2_gate[l], w_ffn2_up[l], w_ffn2_down[l], norm_final,
                                   tf=256)
    y_sample = y_sample.reshape(Bs, T, D)

    x1 = _ffn(x_prompt.reshape(Bp * S, D), norm_ffn1[l], wg1, wu1, wd1)
    xs, zg, p_conv = mixer_mid(x1, Bp)
    cs_p = jnp.asarray(_rope_table(np.arange(S)))
    p_lat, p_kr, q, k, v = _prep(xs, cs_p, norm_q_a[l], norm_kv_a[l], wq, (w_uk_flat, w_uv_flat), prompt=True)
    o_attn = _attn_prompt(q.reshape(Bp, S, -1), k.reshape(Bp, S, -1), v.reshape(Bp, S, -1)).reshape(Bp * S, -1)
    y_ssm, p_ssm = ssd(xs, zg, Bp, jnp.zeros((Bp, CONV_W - 1, CONV_DIM), F32),
                       jnp.zeros((Bp, N_HEADS_S, HEAD_DIM_S, D_STATE), F32), 128)
    merged = _merge(o_attn, y_ssm, woa, wos, zg, b_gate[l], tm=1024, tn=512)
    x2 = _outproj(merged, wo, x1, tm=512, tn=D_MODEL)
    y_prompt = _ffn(x2, norm_ffn2[l], wg2, wu2, wd2, norm_final).reshape(Bp, S, D)

    return (y_prompt, y_sample,
            p_lat.reshape(1, Bp, S, KV_LORA), p_kr.reshape(1, Bp, S, QK_ROPE), p_ssm[None], p_conv[None],
            s_lat.reshape(1, Bs, T, KV_LORA), s_kr.reshape(1, Bs, T, QK_ROPE), s_ssm[None], s_conv[None])
```

```python
import functools
import math

import jax
import jax.numpy as jnp
import numpy as np
from jax import lax
from jax.experimental import pallas as pl
from jax.experimental.pallas import tpu as pltpu

F32 = jnp.float32
BF16 = jnp.bfloat16

D_MODEL = 2048
CHUNK = 64
N_HEADS_A = 16
Q_LORA = 768
KV_LORA = 512
QK_NOPE = 128
QK_ROPE = 64
V_DIM = 128
ROPE_THETA = 10000.0
ATTN_SCALE = (QK_NOPE + QK_ROPE) ** -0.5
D_INNER = 2 * D_MODEL
HEAD_DIM_S = 64
N_HEADS_S = D_INNER // HEAD_DIM_S
N_GROUPS_S = 8
HEADS_PER_GROUP = N_HEADS_S // N_GROUPS_S
GROUP_W = D_INNER // N_GROUPS_S
D_STATE = 128
CONV_W = 4
CONV_DIM = D_INNER + 2 * N_GROUPS_S * D_STATE
D_FF = 5632
EPS = 1e-6

QK_HEAD = 256
COL_QA = 0
COL_LAT = Q_LORA
COL_KR = COL_LAT + KV_LORA
COL_DT = COL_KR + 2 * QK_ROPE
SMALL_W = COL_DT + 128
XS_W = CONV_DIM + SMALL_W
BC_W = N_GROUPS_S * D_STATE
COL_X = 0
COL_B = D_INNER
COL_C = COL_B + BC_W
COL_Z = 0
COL_GATE = D_INNER
ZG_W = COL_GATE + 2 * D_MODEL
LOG2E = math.log2(math.e)

VMEM_LIMIT = 60 * 1024 * 1024


def _cparams(*sem):
    return pltpu.CompilerParams(dimension_semantics=sem, vmem_limit_bytes=VMEM_LIMIT)


def _rms(x, g):
    return x * lax.rsqrt(jnp.mean(x * x, axis=-1, keepdims=True) + EPS) * g


def _silu(x):
    h = 0.5 * x
    return h + h * jnp.tanh(h)


def _dot(a, b):
    return jnp.dot(a, b, preferred_element_type=F32)


def _dot_nt(a, b):
    return lax.dot_general(a, b, (((1,), (1,)), ((), ())), preferred_element_type=F32)


def _dot_tn(a, b):
    return lax.dot_general(a, b, (((0,), (0,)), ((), ())), preferred_element_type=F32)


def _ffn_kernel(*refs, final_norm, emit_bf16):
    refs = list(refs)
    x_ref, g_ref, wg_ref, wu_ref, wd_ref = refs[:5]
    gf_ref = refs[5] if final_norm else None
    o_ref = refs[5 + final_norm]
    h_scr = refs[-1]
    j = pl.program_id(1)

    @pl.when(j == 0)
    def _():
        x = x_ref[...]
        h_scr[...] = _rms(x, g_ref[...]).astype(BF16)
        o_ref[...] = x

    wg, wu, wd = wg_ref[...].astype(BF16), wu_ref[...].astype(BF16), wd_ref[...].astype(BF16)
    if emit_bf16:
        wg_out, wu_out, wd_out = refs[6 + final_norm:9 + final_norm]
        wg_out[...], wu_out[...], wd_out[...] = wg, wu, wd
    h = h_scr[...]
    a = (0.5 * _silu(_dot(h, wg))) * _dot(h, wu)
    o_ref[...] += _dot(a.astype(BF16), wd)

    if final_norm:
        @pl.when(j == pl.num_programs(1) - 1)
        def _():
            o_ref[...] = _rms(o_ref[...], gf_ref[...])


def _ffn(x, g, wg, wu, wd, gf=None, *, tm=1024, tf=512):
    M, D = x.shape
    F = wg.shape[1]
    tm = min(tm, M)
    emit = wg.dtype != BF16
    assert not emit or M == tm
    col = lambda i, j: (0, j)
    row = lambda i, j: (j, 0)
    in_specs = [
        pl.BlockSpec((tm, D), lambda i, j: (i, 0)),
        pl.BlockSpec((1, D), lambda i, j: (0, 0)),
        pl.BlockSpec((D, tf), col),
        pl.BlockSpec((D, tf), col),
        pl.BlockSpec((tf, D), row),
    ]
    args = [x, g.reshape(1, D), wg, wu, wd]
    if gf is not None:
        in_specs.append(pl.BlockSpec((1, D), lambda i, j: (0, 0)))
        args.append(gf.reshape(1, D))
    out_specs = [pl.BlockSpec((tm, D), lambda i, j: (i, 0))]
    out_shape = [jax.ShapeDtypeStruct((M, D), F32)]
    if emit:
        out_specs += [pl.BlockSpec((D, tf), col), pl.BlockSpec((D, tf), col), pl.BlockSpec((tf, D), row)]
        out_shape += [jax.ShapeDtypeStruct(w.shape, BF16) for w in (wg, wu, wd)]
    out = pl.pallas_call(
        functools.partial(_ffn_kernel, final_norm=gf is not None, emit_bf16=emit),
        grid=(M // tm, F // tf),
        in_specs=in_specs,
        out_specs=out_specs,
        out_shape=out_shape,
        scratch_shapes=[pltpu.VMEM((tm, D), BF16)],
        compiler_params=_cparams("parallel", "arbitrary"),
        name="ffn",
    )(*args)
    return out if emit else out[0]


def _relayout_kernel(w_ref, xs_ref, zg_ref):
    w = w_ref[...]
    o_lat = Q_LORA
    o_kr = o_lat + KV_LORA
    o_z = o_kr + QK_ROPE
    o_x = o_z + D_INNER
    o_bc = o_x + D_INNER
    o_dt = o_bc + 2 * BC_W
    o_gate = o_dt + N_HEADS_S
    half = QK_ROPE // 2
    small = jnp.concatenate(
        [w[:, :o_z], w[:, o_kr + half:o_z], w[:, o_kr:o_kr + half], w[:, o_dt:o_gate],
         jnp.zeros((w.shape[0], 128 - N_HEADS_S), F32)], axis=1)
    xs_ref[:, :CONV_DIM] = w[:, o_x:o_dt].astype(BF16)
    xs_ref[:, CONV_DIM:] = small.astype(BF16)
    zg_ref[:, COL_Z:COL_Z + D_INNER] = w[:, o_z:o_x].astype(BF16)
    zg_ref[:, COL_GATE:] = w[:, o_gate:].astype(BF16)


def _relayout_w_in(w, *, tk=128):
    K, N = w.shape
    widths = (XS_W, ZG_W)
    return pl.pallas_call(
        _relayout_kernel,
        grid=(K // tk,),
        in_specs=[pl.BlockSpec((tk, N), lambda i: (i, 0))],
        out_specs=[pl.BlockSpec((tk, n), lambda i: (i, 0)) for n in widths],
        out_shape=[jax.ShapeDtypeStruct((K, n), BF16) for n in widths],
        compiler_params=_cparams("parallel"),
        name="relayout_w_in",
    )(w)


HIST = 8


def _causal_conv_silu(hist, cur, w, bias):
    ext = jnp.concatenate([hist, cur], axis=0)
    acc = bias
    for k in range(CONV_W - 1):
        acc = acc + w[k:k + 1, :] * pltpu.roll(ext, CONV_W - 1 - k, axis=0)[HIST:, :]
    return _silu(acc + w[CONV_W - 1:CONV_W, :] * cur)


def _inproj_kernel(x_ref, g_ref, w_ref, o_ref, h_scr):
    @pl.when(pl.program_id(1) == 0)
    def _():
        h_scr[...] = _rms(x_ref[...], g_ref[...]).astype(BF16)

    o_ref[...] = _dot(h_scr[...], w_ref[...])


def _inproj(x, g, w, *, tm, tn):
    M, D = x.shape
    N = w.shape[1]
    tm = min(tm, M)
    return pl.pallas_call(
        _inproj_kernel,
        grid=(M // tm, N // tn),
        in_specs=[
            pl.BlockSpec((tm, D), lambda i, j: (i, 0)),
            pl.BlockSpec((1, D), lambda i, j: (0, 0)),
            pl.BlockSpec((D, tn), lambda i, j: (0, j)),
        ],
        out_specs=pl.BlockSpec((tm, tn), lambda i, j: (i, j)),
        out_shape=jax.ShapeDtypeStruct((M, N), F32),
        scratch_shapes=[pltpu.VMEM((tm, D), BF16)],
        compiler_params=_cparams("parallel", "arbitrary"),
        name="inproj",
    )(x, g.reshape(1, D), w)


def _rope_pair(blk, cs, keep):
    prod = blk * cs
    return jnp.where(keep, prod + pltpu.roll(prod, 64, axis=1), 0.0)


def _prep_common(small_ref, cs_ref, nq_ref, nkv_ref, wq_ref, ckv_ref, kr_ref):
    sm = small_ref[...]
    cs = cs_ref[...]
    keep = lax.broadcasted_iota(jnp.int32, cs.shape, 1) < QK_ROPE
    cq = _rms(sm[:, COL_QA:COL_QA + Q_LORA], nq_ref[...]).astype(BF16)
    q = _dot(cq, wq_ref[...]) * (ATTN_SCALE * LOG2E)
    ckv = _rms(sm[:, COL_LAT:COL_LAT + KV_LORA], nkv_ref[...])
    ckv_ref[...] = ckv
    krot = _rope_pair(sm[:, COL_KR:COL_KR + 128], cs, keep)
    kr_ref[...] = krot[:, :QK_ROPE]
    return q, ckv, krot, cs, keep


def _prep_prompt_kernel(small_ref, cs_ref, nq_ref, nkv_ref, wq_ref, wuk_ref, wuv_ref,
                        ckv_ref, kr_ref, q_ref, k_ref, v_ref):
    q, ckv, krot, cs, keep = _prep_common(small_ref, cs_ref, nq_ref, nkv_ref, wq_ref, ckv_ref, kr_ref)
    ckv_b = ckv.astype(BF16)
    kn = _dot(ckv_b, wuk_ref[...])
    krot_b = krot.astype(BF16)
    for h in range(N_HEADS_A):
        c0 = h * QK_HEAD
        q_ref[:, c0:c0 + 128] = q[:, c0:c0 + 128].astype(BF16)
        q_ref[:, c0 + 128:c0 + 256] = _rope_pair(q[:, c0 + 128:c0 + 256], cs, keep).astype(BF16)
        k_ref[:, c0:c0 + 128] = kn[:, h * 128:(h + 1) * 128].astype(BF16)
        k_ref[:, c0 + 128:c0 + 256] = krot_b
    v_ref[...] = _dot(ckv_b, wuv_ref[...]).astype(BF16)


def _prep_sample_kernel(small_ref, cs_ref, nq_ref, nkv_ref, wq_ref, wukt_ref,
                        ckv_ref, kr_ref, qlat_ref, qrot_ref):
    q, _, _, cs, keep = _prep_common(small_ref, cs_ref, nq_ref, nkv_ref, wq_ref, ckv_ref, kr_ref)
    for h in range(N_HEADS_A):
        c0 = h * QK_HEAD
        qn = q[:, c0:c0 + 128].astype(BF16)
        qlat_ref[:, h * KV_LORA:(h + 1) * KV_LORA] = _dot(qn, wukt_ref[h]).astype(BF16)
        qrot_ref[:, h * 128:(h + 1) * 128] = _rope_pair(q[:, c0 + 128:c0 + 256], cs, keep).astype(BF16)


def _prep(proj, cs_table, nq, nkv, wq, extra_w, *, prompt, tm=512):
    M = proj.shape[0]
    tm = min(tm, M)
    n_cs = cs_table.shape[0] // tm
    H = N_HEADS_A
    row = lambda i: (i, 0)
    const2 = lambda i: (0, 0)
    in_specs = [
        pl.BlockSpec((tm, SMALL_W), lambda i: (i, CONV_DIM // SMALL_W)),
        pl.BlockSpec((tm, 128), lambda i: (i % n_cs, 0)),
        pl.BlockSpec((1, Q_LORA), const2),
        pl.BlockSpec((1, KV_LORA), const2),
        pl.BlockSpec((Q_LORA, H * QK_HEAD), const2),
    ]
    out_specs = [pl.BlockSpec((tm, KV_LORA), row), pl.BlockSpec((tm, QK_ROPE), row)]
    out_shape = [jax.ShapeDtypeStruct((M, KV_LORA), F32), jax.ShapeDtypeStruct((M, QK_ROPE), F32)]
    if prompt:
        w_uk, w_uv = extra_w
        in_specs += [pl.BlockSpec((KV_LORA, H * QK_NOPE), const2), pl.BlockSpec((KV_LORA, H * V_DIM), const2)]
        out_specs += [pl.BlockSpec((tm, H * QK_HEAD), row), pl.BlockSpec((tm, H * QK_HEAD), row),
                      pl.BlockSpec((tm, H * V_DIM), row)]
        out_shape += [jax.ShapeDtypeStruct((M, H * QK_HEAD), BF16), jax.ShapeDtypeStruct((M, H * QK_HEAD), BF16),
                      jax.ShapeDtypeStruct((M, H * V_DIM), BF16)]
        body, extra = _prep_prompt_kernel, [w_uk, w_uv]
    else:
        (w_ukt,) = extra_w
        in_specs += [pl.BlockSpec((H, QK_NOPE, KV_LORA), lambda i: (0, 0, 0))]
        out_specs += [pl.BlockSpec((tm, H * KV_LORA), row), pl.BlockSpec((tm, H * 128), row)]
        out_shape += [jax.ShapeDtypeStruct((M, H * KV_LORA), BF16), jax.ShapeDtypeStruct((M, H * 128), BF16)]
        body, extra = _prep_sample_kernel, [w_ukt]
    return pl.pallas_call(
        body,
        grid=(M // tm,),
        in_specs=in_specs,
        out_specs=out_specs,
        out_shape=out_shape,
        compiler_params=_cparams("parallel"),
        name="prep_prompt" if prompt else "prep_sample",
    )(proj, cs_table, nq.reshape(1, -1), nkv.reshape(1, -1), wq, *extra)


def _attn_prompt_kernel(q_ref, k_ref, v_ref, bias_ref, o_ref, s_scr, p_scr, *, seq, tq):
    for qi in reversed(range(seq // tq)):
        kv = (qi + 1) * tq
        s_buf = s_scr.at[qi % 2]
        p_buf = p_scr.at[qi % 2]
        q = q_ref[qi * tq:kv, :]
        s_buf[:, :kv] = _dot_nt(q, k_ref[:kv, :])
        s_buf[:, kv - tq:kv] = s_buf[:, kv - tq:kv] + bias_ref[...]
        m = jnp.max(s_buf[:, :kv], axis=-1, keepdims=True)
        p = jnp.exp2(s_buf[:, :kv] - m)
        l = jnp.sum(p, axis=-1, keepdims=True)
        p_buf[:, :kv] = p.astype(BF16)
        acc = _dot(p_buf[:, :kv], v_ref[:kv, :])
        o_ref[qi * tq:kv, :] = (acc / l).astype(o_ref.dtype)


def _attn_prompt(q, k, v, *, tq=512):
    B, S, _ = q.shape
    H = N_HEADS_A
    chunk = np.arange(tq) // CHUNK
    bias = jnp.asarray(np.where(chunk[None, :] <= chunk[:, None], 0.0, -np.inf).astype(np.float32))
    return pl.pallas_call(
        functools.partial(_attn_prompt_kernel, seq=S, tq=tq),
        grid=(B, H),
        in_specs=[
            pl.BlockSpec((None, S, QK_HEAD), lambda b, h: (b, 0, h)),
            pl.BlockSpec((None, S, QK_HEAD), lambda b, h: (b, 0, h)),
            pl.BlockSpec((None, S, V_DIM), lambda b, h: (b, 0, h)),
            pl.BlockSpec((tq, tq), lambda b, h: (0, 0)),
        ],
        out_specs=pl.BlockSpec((None, S, V_DIM), lambda b, h: (b, 0, h)),
        out_shape=jax.ShapeDtypeStruct((B, S, H * V_DIM), BF16),
        scratch_shapes=[pltpu.VMEM((2, tq, S), F32), pltpu.VMEM((2, tq, S), BF16)],
        compiler_params=_cparams("parallel", "parallel"),
        name="attn_prompt",
    )(q, k, v, bias)


def _attn_sample_kernel(qlat_ref, qrot_ref, clat_ref, ckrt_ref, nlat_ref, nkr_ref, o_ref):
    H = N_HEADS_A
    T = nlat_ref.shape[1]
    for sb in range(clat_ref.shape[0]):
        rows = slice(sb * T, (sb + 1) * T)
        ql = jnp.concatenate([qlat_ref[rows, h * KV_LORA:(h + 1) * KV_LORA] for h in range(H)], axis=0)
        qr = jnp.concatenate([qrot_ref[rows, h * 128:h * 128 + QK_ROPE] for h in range(H)], axis=0)
        lat = clat_ref[sb].astype(BF16)
        nlat = nlat_ref[sb].astype(BF16)
        s1 = _dot_nt(ql, lat) + _dot(qr, ckrt_ref[sb].astype(BF16))
        s2 = _dot_nt(ql, nlat) + _dot_nt(qr, nkr_ref[sb].astype(BF16))
        m = jnp.maximum(jnp.max(s1, axis=-1, keepdims=True), jnp.max(s2, axis=-1, keepdims=True))
        p1 = jnp.exp2(s1 - m)
        p2 = jnp.exp2(s2 - m)
        l = jnp.sum(p1, axis=-1, keepdims=True) + jnp.sum(p2, axis=-1, keepdims=True)
        o = _dot(p1.astype(BF16), lat) + _dot(p2.astype(BF16), nlat)
        o = (o / l).astype(o_ref.dtype)
        for h in range(H):
            o_ref[rows, h * KV_LORA:(h + 1) * KV_LORA] = o[h * T:(h + 1) * T, :]


def _attn_sample(qlat, qrot, cache_lat, cache_kr_t, new_lat, new_kr, *, streams):
    Bs, P, _ = cache_lat.shape
    T = new_lat.shape[1]
    H = N_HEADS_A
    ns = streams
    b2 = lambda b: (b, 0)
    b3 = lambda b: (b, 0, 0)
    return pl.pallas_call(
        _attn_sample_kernel,
        grid=(Bs // ns,),
        in_specs=[
            pl.BlockSpec((ns * T, H * KV_LORA), b2),
            pl.BlockSpec((ns * T, H * 128), b2),
            pl.BlockSpec((ns, P, KV_LORA), b3),
            pl.BlockSpec((ns, QK_ROPE, P), b3),
            pl.BlockSpec((ns, T, KV_LORA), b3),
            pl.BlockSpec((ns, T, QK_ROPE), b3),
        ],
        out_specs=pl.BlockSpec((ns * T, H * KV_LORA), b2),
        out_shape=jax.ShapeDtypeStruct((Bs * T, H * KV_LORA), BF16),
        compiler_params=_cparams("parallel"),
        name="attn_sample",
    )(qlat, qrot, cache_lat, cache_kr_t, new_lat, new_kr)


def _uv_kernel(x_ref, w_ref, o_ref):
    o_ref[...] = _dot(x_ref[...], w_ref[...]).astype(o_ref.dtype)


def _uv_proj(olat, w_uv_heads):
    M = olat.shape[0]
    H = N_HEADS_A
    return pl.pallas_call(
        _uv_kernel,
        grid=(H,),
        in_specs=[pl.BlockSpec((M, KV_LORA), lambda h: (0, h)),
                  pl.BlockSpec((None, KV_LORA, V_DIM), lambda h: (h, 0, 0))],
        out_specs=pl.BlockSpec((M, V_DIM), lambda h: (0, h)),
        out_shape=jax.ShapeDtypeStruct((M, H * V_DIM), BF16),
        compiler_params=_cparams("parallel"),
        name="uv_proj",
    )(olat, w_uv_heads)


def _split3(v):
    hi = v.astype(BF16)
    r1 = v - hi.astype(F32)
    mid = r1.astype(BF16)
    lo = (r1 - mid.astype(F32)).astype(BF16)
    return jnp.concatenate([hi, mid, lo], axis=1)


def _ssd_kernel(x_ref, z_ref, b_ref, c_ref, dt_ref, cst_ref, cw_ref, cbias_ref, dtb_ref, alog_ref,
                dskx_ref, nssm_ref, e64_ref, e128_ref, s0_ref,
                y_ref, sout_ref, xbuf, bbuf, cbuf, xs_scr, bm_scr, cm_scr, s_scr, *, q_len, n_chunks):
    c = pl.program_id(1)
    Q = q_len
    P = HEAD_DIM_S
    G, R, N, W = N_GROUPS_S, HEADS_PER_GROUP, D_STATE, GROUP_W
    h0 = HIST - (CONV_W - 1)

    @pl.when(c == 0)
    def _():
        for buf, lo, hi in ((xbuf, 0, D_INNER), (bbuf, D_INNER, D_INNER + BC_W), (cbuf, D_INNER + BC_W, CONV_DIM)):
            buf[0:h0, :] = jnp.zeros((h0, hi - lo), F32)
            buf[h0:HIST, :] = cst_ref[:, lo:hi]
        s_scr[...] = s0_ref[...]

    def conv(hist, cur_ref, lo, hi):
        cur = cur_ref[...]
        out = _causal_conv_silu(hist[...], cur, cw_ref[:, lo:hi], cbias_ref[:, lo:hi])
        hist[...] = cur[Q - HIST:, :]
        return out

    xs_scr[...] = conv(xbuf, x_ref, 0, D_INNER)
    bm_scr[...] = conv(bbuf, b_ref, D_INNER, D_INNER + BC_W).astype(BF16)
    cm_scr[...] = conv(cbuf, c_ref, D_INNER + BC_W, CONV_DIM).astype(BF16)

    dt_all = jax.nn.softplus(dt_ref[...] + dtb_ref[...])
    a_all = dt_all * (-LOG2E * jnp.exp(alog_ref[...]))
    ri = lax.broadcasted_iota(jnp.int32, (Q, Q), 0)
    ci = lax.broadcasted_iota(jnp.int32, (Q, Q), 1)
    causal = ci <= ri
    hp = lax.Precision.HIGHEST
    acs_all = jnp.dot(causal.astype(F32), a_all, precision=hp, preferred_element_type=F32)
    acs_row = lax.dot_general(a_all, (ri <= ci).astype(F32), (((0,), (0,)), ((), ())),
                              precision=hp, preferred_element_type=F32)
    a_end = acs_all[Q - 1:Q, :]
    eacs_all = jnp.exp2(acs_all)
    w_all = dt_all * jnp.exp2(a_end - acs_all)
    dec_all = jnp.exp2(a_end)
    left = lax.broadcasted_iota(jnp.int32, (Q, 128), 1) < P
    e64 = e64_ref[...]
    e128 = e128_ref[...]

    for g in range(G):
        h_lo = g * R
        dtx = _dot(_split3(dt_all[:, h_lo:h_lo + R]), e64)
        eacsx = _dot(_split3(eacs_all[:, h_lo:h_lo + R]), e64)
        wx = _dot(_split3(w_all[:, h_lo:h_lo + R]), e64)
        acsb = _dot(_split3(acs_all[:, h_lo:h_lo + R]), e128)
        xs = xs_scr[:, g * W:(g + 1) * W]
        bm = bm_scr[:, g * N:(g + 1) * N]
        cm = cm_scr[:, g * N:(g + 1) * N]
        xdt = xs * dtx
        cb = _dot_nt(cm, bm)
        y_off = _dot_nt(cm, s_scr[g].astype(BF16))
        ys = []
        for j in range(R // 2):
            ms = []
            for r in (2 * j, 2 * j + 1):
                seg = acsb[:, r * 128:r * 128 + Q] - acs_row[h_lo + r:h_lo + r + 1, :]
                ms.append((cb * jnp.exp2(jnp.where(causal, seg, -jnp.inf))).astype(BF16))
            xp = xdt[:, j * 128:(j + 1) * 128]
            rhs = jnp.concatenate([jnp.where(left, xp, 0.0), jnp.where(left, 0.0, xp)], axis=0).astype(BF16)
            ys.append(_dot(jnp.concatenate(ms, axis=1), rhs))
        y = jnp.concatenate(ys, axis=1) + y_off * eacsx + dskx_ref[:, g * W:(g + 1) * W] * xs

        st = _dot_tn((xs * wx).astype(BF16), bm)
        for r in range(R):
            dec = dec_all[:, h_lo + r:h_lo + r + 1]
            s_scr[g, r * P:(r + 1) * P, :] = dec * s_scr[g, r * P:(r + 1) * P, :] + st[r * P:(r + 1) * P, :]

        v = y * _silu(z_ref[:, g * W:(g + 1) * W])
        y_ref[:, g * W:(g + 1) * W] = _rms(v, nssm_ref[:, g * W:(g + 1) * W]).astype(y_ref.dtype)

    @pl.when(c == n_chunks - 1)
    def _():
        sout_ref[...] = s_scr[...]


def _head_expander(width):
    R = HEADS_PER_GROUP
    row_head = np.arange(3 * R) % R
    col_head = np.arange(R * width) // width
    return jnp.asarray((row_head[:, None] == col_head[None, :]).astype(np.float32)).astype(BF16)


def _ssd(xs3, zg3, conv_state, conv_w, conv_b, dt_bias, a_log, d_skip, n_ssm, state0, *, q_len):
    B, L, _ = xs3.shape
    G, N = N_GROUPS_S, D_STATE
    Q = q_len
    nc = L // Q
    pad = 128 - N_HEADS_S
    const = lambda b, c: (0, 0)
    in_specs = [
        pl.BlockSpec((None, Q, D_INNER), lambda b, c: (b, c, COL_X // D_INNER)),
        pl.BlockSpec((None, Q, D_INNER), lambda b, c: (b, c, COL_Z // D_INNER)),
        pl.BlockSpec((None, Q, BC_W), lambda b, c: (b, c, COL_B // BC_W)),
        pl.BlockSpec((None, Q, BC_W), lambda b, c: (b, c, COL_C // BC_W)),
        pl.BlockSpec((None, Q, 128), lambda b, c: (b, c, (CONV_DIM + COL_DT) // 128)),
        pl.BlockSpec((None, CONV_W - 1, CONV_DIM), lambda b, c: (b, 0, 0)),
        pl.BlockSpec((CONV_W, CONV_DIM), const),
        pl.BlockSpec((1, CONV_DIM), const),
        pl.BlockSpec((1, 128), const),
        pl.BlockSpec((1, 128), const),
        pl.BlockSpec((1, D_INNER), const),
        pl.BlockSpec((1, D_INNER), const),
        pl.BlockSpec((3 * HEADS_PER_GROUP, GROUP_W), const),
        pl.BlockSpec((3 * HEADS_PER_GROUP, HEADS_PER_GROUP * 128), const),
        pl.BlockSpec((None, G, GROUP_W, N), lambda b, c: (b, 0, 0, 0)),
    ]
    args = [xs3, zg3, xs3, xs3, xs3, conv_state, conv_w, conv_b.reshape(1, CONV_DIM),
            jnp.pad(dt_bias, (0, pad)).reshape(1, 128), jnp.pad(a_log, (0, pad)).reshape(1, 128),
            jnp.repeat(d_skip, HEAD_DIM_S).reshape(1, D_INNER), n_ssm.reshape(1, D_INNER),
            _head_expander(HEAD_DIM_S), _head_expander(128), state0.reshape(B, G, GROUP_W, N)]
    y, s_out = pl.pallas_call(
        functools.partial(_ssd_kernel, q_len=Q, n_chunks=nc),
        grid=(B, nc),
        in_specs=in_specs,
        out_specs=[pl.BlockSpec((None, Q, D_INNER), lambda b, c: (b, c, 0)),
                   pl.BlockSpec((None, G, GROUP_W, N), lambda b, c: (b, 0, 0, 0))],
        out_shape=[jax.ShapeDtypeStruct((B, L, D_INNER), BF16),
                   jax.ShapeDtypeStruct((B, G, GROUP_W, N), F32)],
        scratch_shapes=[pltpu.VMEM((HIST, D_INNER), F32), pltpu.VMEM((HIST, BC_W), F32),
                        pltpu.VMEM((HIST, BC_W), F32), pltpu.VMEM((Q, D_INNER), F32),
                        pltpu.VMEM((Q, BC_W), BF16), pltpu.VMEM((Q, BC_W), BF16),
                        pltpu.VMEM((G, GROUP_W, N), F32)],
        compiler_params=_cparams("parallel", "arbitrary"),
        name="ssd",
    )(*args)
    return y, s_out.reshape(B, N_HEADS_S, HEAD_DIM_S, N)


def _merge_kernel(oa_ref, ys_ref, wa_ref, ws_ref, g0_ref, g1_ref, bg_ref, o_ref, *w_out_refs):
    wa, ws = wa_ref[...].astype(BF16), ws_ref[...].astype(BF16)
    if w_out_refs:
        w_out_refs[0][...], w_out_refs[1][...] = wa, ws
    attn = _dot(oa_ref[...], wa)
    ssm = _dot(ys_ref[...], ws)
    bg = bg_ref[...]
    g0 = jax.nn.sigmoid(g0_ref[...] + bg[0:1, :])
    g1 = jax.nn.sigmoid(g1_ref[...] + bg[1:2, :])
    o_ref[...] = (g0 * attn + g1 * ssm).astype(o_ref.dtype)


def _merge(o_attn, y_ssm, w_o_attn, w_o_ssm, proj, b_gate, *, tm=512, tn=512):
    M = o_attn.shape[0]
    tm = min(tm, M)
    emit = w_o_attn.dtype != BF16
    assert not emit or M == tm
    g0blk = COL_GATE // tn
    g1blk = (COL_GATE + D_MODEL) // tn
    col = lambda i, j: (0, j)
    w_specs = [pl.BlockSpec((w_o_attn.shape[0], tn), col), pl.BlockSpec((D_INNER, tn), col)]
    out_specs = [pl.BlockSpec((tm, tn), lambda i, j: (i, j))]
    out_shape = [jax.ShapeDtypeStruct((M, D_MODEL), BF16)]
    if emit:
        out_specs += w_specs
        out_shape += [jax.ShapeDtypeStruct(w.shape, BF16) for w in (w_o_attn, w_o_ssm)]
    out = pl.pallas_call(
        _merge_kernel,
        grid=(M // tm, D_MODEL // tn),
        in_specs=[
            pl.BlockSpec((tm, o_attn.shape[1]), lambda i, j: (i, 0)),
            pl.BlockSpec((tm, D_INNER), lambda i, j: (i, 0)),
            *w_specs,
            pl.BlockSpec((tm, tn), lambda i, j: (i, g0blk + j)),
            pl.BlockSpec((tm, tn), lambda i, j: (i, g1blk + j)),
            pl.BlockSpec((2, tn), col),
        ],
        out_specs=out_specs,
        out_shape=out_shape,
        compiler_params=_cparams("parallel", "arbitrary"),
        name="merge",
    )(o_attn, y_ssm, w_o_attn, w_o_ssm, proj, proj, b_gate)
    return out if emit else out[0]


def _outproj_kernel(m_ref, w_ref, x_ref, o_ref, *w_out_refs):
    w = w_ref[...].astype(BF16)
    if w_out_refs:
        w_out_refs[0][...] = w
    o_ref[...] = x_ref[...] + _dot(m_ref[...], w)


def _outproj(merged, w_out, x, *, tm=512, tn=512):
    M = x.shape[0]
    tm = min(tm, M)
    emit = w_out.dtype != BF16
    assert not emit or M == tm
    w_spec = pl.BlockSpec((D_MODEL, tn), lambda i, j: (0, j))
    out_specs = [pl.BlockSpec((tm, tn), lambda i, j: (i, j))]
    out_shape = [jax.ShapeDtypeStruct((M, D_MODEL), F32)]
    if emit:
        out_specs.append(w_spec)
        out_shape.append(jax.ShapeDtypeStruct(w_out.shape, BF16))
    out = pl.pallas_call(
        _outproj_kernel,
        grid=(M // tm, D_MODEL // tn),
        in_specs=[
            pl.BlockSpec((tm, D_MODEL), lambda i, j: (i, 0)),
            w_spec,
            pl.BlockSpec((tm, tn), lambda i, j: (i, j)),
        ],
        out_specs=out_specs,
        out_shape=out_shape,
        compiler_params=_cparams("parallel", "arbitrary"),
        name="outproj",
    )(merged, w_out, x)
    return out if emit else out[0]


def _swap_halves(w):
    half = QK_ROPE // 2
    return jnp.concatenate([w[..., half:], w[..., :half]], axis=-1)


def _rope_table(pos):
    half = QK_ROPE // 2
    inv = np.power(ROPE_THETA, -np.arange(half, dtype=np.float64) / half)
    ang = np.asarray(pos, np.float64)[:, None] * inv[None, :]
    cos, sin = np.cos(ang), np.sin(ang)
    return np.concatenate([cos, cos, -sin, sin], axis=-1).astype(np.float32)


def kernel(x_prompt, x_sample, cache_kv_latent, cache_k_rope, state_ssm, state_conv, norm_ffn1, w_ffn1_gate, w_ffn1_up, w_ffn1_down, norm_mix, w_in, b_gate, norm_q_a, w_q_b, norm_kv_a, w_kv_b, conv_w, conv_b, dt_bias, a_log, d_skip, norm_ssm, w_o_attn, w_o_ssm, w_out, norm_ffn2, w_ffn2_gate, w_ffn2_up, w_ffn2_down, norm_final):
    Bp, S, D = x_prompt.shape
    Bs, T, _ = x_sample.shape
    past = cache_kv_latent.shape[2]
    H = N_HEADS_A
    l = 0

    w_xs, w_zg = _relayout_w_in(w_in[l])
    wq3 = w_q_b[l].reshape(Q_LORA, H, QK_NOPE + QK_ROPE)
    wq = jnp.concatenate([wq3, _swap_halves(wq3[..., QK_NOPE:])], axis=-1).reshape(Q_LORA, H * QK_HEAD).astype(BF16)
    wkv3 = w_kv_b[l].reshape(KV_LORA, H, QK_NOPE + V_DIM)
    w_uk = wkv3[..., :QK_NOPE]
    w_uv = wkv3[..., QK_NOPE:]
    w_uk_flat = w_uk.reshape(KV_LORA, H * QK_NOPE).astype(BF16)
    w_uv_flat = w_uv.reshape(KV_LORA, H * V_DIM).astype(BF16)
    w_uk_t = w_uk.transpose(1, 2, 0).astype(BF16)
    w_uv_h = w_uv.transpose(1, 0, 2).astype(BF16)
    def mixer_mid(x1, batch):
        xs = _inproj(x1, norm_mix[l], w_xs, tm=1024, tn=SMALL_W)
        zg = _inproj(x1, norm_mix[l], w_zg, tm=1024, tn=1024)
        xs3 = xs.reshape(batch, -1, XS_W)
        return xs, zg, xs3[:, xs3.shape[1] - (CONV_W - 1):, :CONV_DIM]


    def ssd(xs, zg, batch, conv_state, state, q_len):
        y, new_state = _ssd(xs.reshape(batch, -1, XS_W), zg.reshape(batch, -1, ZG_W), conv_state,
                            conv_w[l], conv_b[l], dt_bias[l], a_log[l], d_skip[l], norm_ssm[l], state, q_len=q_len)
        return y.reshape(-1, D_INNER), new_state

    x1s, wg1, wu1, wd1 = _ffn(x_sample.reshape(Bs * T, D), norm_ffn1[l], w_ffn1_gate[l], w_ffn1_up[l],
                              w_ffn1_down[l], tf=512)
    xss, zgs, s_conv = mixer_mid(x1s, Bs)
    cs_s = jnp.asarray(np.tile(_rope_table(past + np.arange(T)), (Bs, 1)))
    s_lat, s_kr, qlat, qrot = _prep(xss, cs_s, norm_q_a[l], norm_kv_a[l], wq, (w_uk_t,), prompt=False)
    o_lat = _attn_sample(qlat, qrot, cache_kv_latent[l], jnp.swapaxes(cache_k_rope[l], 1, 2),
                         s_lat.reshape(Bs, T, KV_LORA), s_kr.reshape(Bs, T, QK_ROPE),
                         streams=2 if Bs % 2 == 0 else 1)
    o_attn_s = _uv_proj(o_lat, w_uv_h)
    y_ssm_s, s_ssm = ssd(xss, zgs, Bs, state_conv[l], state_ssm[l], T)
    merged_s, woa, wos = _merge(o_attn_s, y_ssm_s, w_o_attn[l], w_o_ssm[l], zgs, b_gate[l], tm=1024, tn=256)
    x2s, wo = _outproj(merged_s, w_out[l], x1s, tm=1024, tn=1024)
    y_sample, wg2, wu2, wd2 = _ffn(x2s, norm_ffn2[l], w_ffn2_gate[l], w_ffn2_up[l], w_ffn2_down[l], norm_final,
                                   tf=512)
    y_sample = y_sample.reshape(Bs, T, D)

    x1 = _ffn(x_prompt.reshape(Bp * S, D), norm_ffn1[l], wg1, wu1, wd1)
    xs, zg, p_conv = mixer_mid(x1, Bp)
    cs_p = jnp.asarray(_rope_table(np.arange(S)))
    p_lat, p_kr, q, k, v = _prep(xs, cs_p, norm_q_a[l], norm_kv_a[l], wq, (w_uk_flat, w_uv_flat), prompt=True)
    o_attn = _attn_prompt(q.reshape(Bp, S, -1), k.reshape(Bp, S, -1), v.reshape(Bp, S, -1)).reshape(Bp * S, -1)
    y_ssm, p_ssm = ssd(xs, zg, Bp, jnp.zeros((Bp, CONV_W - 1, CONV_DIM), F32),
                       jnp.zeros((Bp, N_HEADS_S, HEAD_DIM_S, D_STATE), F32), 128)
    merged = _merge(o_attn, y_ssm, woa, wos, zg, b_gate[l], tm=1024, tn=512)
    x2 = _outproj(merged, wo, x1, tm=512, tn=D_MODEL)
    y_prompt = _ffn(x2, norm_ffn2[l], wg2, wu2, wd2, norm_final).reshape(Bp, S, D)

    return (y_prompt, y_sample,
            p_lat.reshape(1, Bp, S, KV_LORA), p_kr.reshape(1, Bp, S, QK_ROPE), p_ssm[None], p_conv[None],
            s_lat.reshape(1, Bs, T, KV_LORA), s_kr.reshape(1, Bs, T, QK_ROPE), s_ssm[None], s_conv[None])
```

```python
import functools
import math

import jax
import jax.numpy as jnp
import numpy as np
from jax import lax
from jax.experimental import pallas as pl
from jax.experimental.pallas import tpu as pltpu

F32 = jnp.float32
BF16 = jnp.bfloat16

D_MODEL = 2048
CHUNK = 64
N_HEADS_A = 16
Q_LORA = 768
KV_LORA = 512
QK_NOPE = 128
QK_ROPE = 64
V_DIM = 128
ROPE_THETA = 10000.0
ATTN_SCALE = (QK_NOPE + QK_ROPE) ** -0.5
D_INNER = 2 * D_MODEL
HEAD_DIM_S = 64
N_HEADS_S = D_INNER // HEAD_DIM_S
N_GROUPS_S = 8
HEADS_PER_GROUP = N_HEADS_S // N_GROUPS_S
GROUP_W = D_INNER // N_GROUPS_S
D_STATE = 128
CONV_W = 4
CONV_DIM = D_INNER + 2 * N_GROUPS_S * D_STATE
D_FF = 5632
EPS = 1e-6

QK_HEAD = 256
COL_QA = 0
COL_LAT = Q_LORA
COL_KR = COL_LAT + KV_LORA
COL_DT = COL_KR + 2 * QK_ROPE
SMALL_W = COL_DT + 128
XS_W = CONV_DIM + SMALL_W
BC_W = N_GROUPS_S * D_STATE
COL_X = 0
COL_B = D_INNER
COL_C = COL_B + BC_W
COL_Z = 0
COL_GATE = D_INNER
ZG_W = COL_GATE + 2 * D_MODEL
LOG2E = math.log2(math.e)

VMEM_LIMIT = 60 * 1024 * 1024


def _cparams(*sem):
    return pltpu.CompilerParams(dimension_semantics=sem, vmem_limit_bytes=VMEM_LIMIT)


def _rms(x, g):
    return x * lax.rsqrt(jnp.mean(x * x, axis=-1, keepdims=True) + EPS) * g


def _silu(x):
    h = 0.5 * x
    return h + h * jnp.tanh(h)


def _dot(a, b):
    return jnp.dot(a, b, preferred_element_type=F32)


def _dot_nt(a, b):
    return lax.dot_general(a, b, (((1,), (1,)), ((), ())), preferred_element_type=F32)


def _dot_tn(a, b):
    return lax.dot_general(a, b, (((0,), (0,)), ((), ())), preferred_element_type=F32)


def _ffn_kernel(*refs, final_norm, emit_bf16):
    refs = list(refs)
    x_ref, g_ref, wg_ref, wu_ref, wd_ref = refs[:5]
    gf_ref = refs[5] if final_norm else None
    o_ref = refs[5 + final_norm]
    h_scr = refs[-1]
    j = pl.program_id(1)

    @pl.when(j == 0)
    def _():
        x = x_ref[...]
        h_scr[...] = _rms(x, g_ref[...]).astype(BF16)
        o_ref[...] = x

    wg, wu, wd = wg_ref[...].astype(BF16), wu_ref[...].astype(BF16), wd_ref[...].astype(BF16)
    if emit_bf16:
        wg_out, wu_out, wd_out = refs[6 + final_norm:9 + final_norm]
        wg_out[...], wu_out[...], wd_out[...] = wg, wu, wd
    h = h_scr[...]
    a = (0.5 * _silu(_dot(h, wg))) * _dot(h, wu)
    o_ref[...] += _dot(a.astype(BF16), wd)

    if final_norm:
        @pl.when(j == pl.num_programs(1) - 1)
        def _():
            o_ref[...] = _rms(o_ref[...], gf_ref[...])


def _ffn(x, g, wg, wu, wd, gf=None, *, tm=1024, tf=512):
    M, D = x.shape
    F = wg.shape[1]
    tm = min(tm, M)
    emit = wg.dtype != BF16
    assert not emit or M == tm
    col = lambda i, j: (0, j)
    row = lambda i, j: (j, 0)
    in_specs = [
        pl.BlockSpec((tm, D), lambda i, j: (i, 0)),
        pl.BlockSpec((1, D), lambda i, j: (0, 0)),
        pl.BlockSpec((D, tf), col),
        pl.BlockSpec((D, tf), col),
        pl.BlockSpec((tf, D), row),
    ]
    args = [x, g.reshape(1, D), wg, wu, wd]
    if gf is not None:
        in_specs.append(pl.BlockSpec((1, D), lambda i, j: (0, 0)))
        args.append(gf.reshape(1, D))
    out_specs = [pl.BlockSpec((tm, D), lambda i, j: (i, 0))]
    out_shape = [jax.ShapeDtypeStruct((M, D), F32)]
    if emit:
        out_specs += [pl.BlockSpec((D, tf), col), pl.BlockSpec((D, tf), col), pl.BlockSpec((tf, D), row)]
        out_shape += [jax.ShapeDtypeStruct(w.shape, BF16) for w in (wg, wu, wd)]
    out = pl.pallas_call(
        functools.partial(_ffn_kernel, final_norm=gf is not None, emit_bf16=emit),
        grid=(M // tm, F // tf),
        in_specs=in_specs,
        out_specs=out_specs,
        out_shape=out_shape,
        scratch_shapes=[pltpu.VMEM((tm, D), BF16)],
        compiler_params=_cparams("parallel", "arbitrary"),
        name="ffn",
    )(*args)
    return out if emit else out[0]


def _relayout_kernel(w_ref, xs_ref, zg_ref):
    w = w_ref[...]
    o_lat = Q_LORA
    o_kr = o_lat + KV_LORA
    o_z = o_kr + QK_ROPE
    o_x = o_z + D_INNER
    o_bc = o_x + D_INNER
    o_dt = o_bc + 2 * BC_W
    o_gate = o_dt + N_HEADS_S
    half = QK_ROPE // 2
    small = jnp.concatenate(
        [w[:, :o_z], w[:, o_kr + half:o_z], w[:, o_kr:o_kr + half], w[:, o_dt:o_gate],
         jnp.zeros((w.shape[0], 128 - N_HEADS_S), F32)], axis=1)
    xs_ref[:, :CONV_DIM] = w[:, o_x:o_dt].astype(BF16)
    xs_ref[:, CONV_DIM:] = small.astype(BF16)
    zg_ref[:, COL_Z:COL_Z + D_INNER] = w[:, o_z:o_x].astype(BF16)
    zg_ref[:, COL_GATE:] = w[:, o_gate:].astype(BF16)


def _relayout_w_in(w, *, tk=128):
    K, N = w.shape
    widths = (XS_W, ZG_W)
    return pl.pallas_call(
        _relayout_kernel,
        grid=(K // tk,),
        in_specs=[pl.BlockSpec((tk, N), lambda i: (i, 0))],
        out_specs=[pl.BlockSpec((tk, n), lambda i: (i, 0)) for n in widths],
        out_shape=[jax.ShapeDtypeStruct((K, n), BF16) for n in widths],
        compiler_params=_cparams("parallel"),
        name="relayout_w_in",
    )(w)


HIST = 8


def _causal_conv_silu(hist, cur, w, bias):
    ext = jnp.concatenate([hist, cur], axis=0)
    acc = bias
    for k in range(CONV_W - 1):
        acc = acc + w[k:k + 1, :] * pltpu.roll(ext, CONV_W - 1 - k, axis=0)[HIST:, :]
    return _silu(acc + w[CONV_W - 1:CONV_W, :] * cur)


def _inproj_kernel(x_ref, g_ref, w_ref, o_ref, h_scr):
    @pl.when(pl.program_id(1) == 0)
    def _():
        h_scr[...] = _rms(x_ref[...], g_ref[...]).astype(BF16)

    o_ref[...] = _dot(h_scr[...], w_ref[...])


def _inproj(x, g, w, *, tm, tn):
    M, D = x.shape
    N = w.shape[1]
    tm = min(tm, M)
    return pl.pallas_call(
        _inproj_kernel,
        grid=(M // tm, N // tn),
        in_specs=[
            pl.BlockSpec((tm, D), lambda i, j: (i, 0)),
            pl.BlockSpec((1, D), lambda i, j: (0, 0)),
            pl.BlockSpec((D, tn), lambda i, j: (0, j)),
        ],
        out_specs=pl.BlockSpec((tm, tn), lambda i, j: (i, j)),
        out_shape=jax.ShapeDtypeStruct((M, N), F32),
        scratch_shapes=[pltpu.VMEM((tm, D), BF16)],
        compiler_params=_cparams("parallel", "arbitrary"),
        name="inproj",
    )(x, g.reshape(1, D), w)


def _inproj2_kernel(x_ref, g_ref, wa_ref, wb_ref, oa_ref, ob_ref, h_scr, *, na):
    j = pl.program_id(1)

    @pl.when(j == 0)
    def _():
        h_scr[...] = _rms(x_ref[...], g_ref[...]).astype(BF16)

    @pl.when(j < na)
    def _():
        oa_ref[...] = _dot(h_scr[...], wa_ref[...])

    @pl.when(j >= na)
    def _():
        ob_ref[...] = _dot(h_scr[...], wb_ref[...])


def _inproj2(x, g, wa, wb, *, tm, tna, tnb):
    M, D = x.shape
    tm = min(tm, M)
    na, nb = wa.shape[1] // tna, wb.shape[1] // tnb
    ja = lambda j: jnp.minimum(j, na - 1)
    jb = lambda j: jnp.maximum(j - na, 0)
    return pl.pallas_call(
        functools.partial(_inproj2_kernel, na=na),
        grid=(M // tm, na + nb),
        in_specs=[
            pl.BlockSpec((tm, D), lambda i, j: (i, 0)),
            pl.BlockSpec((1, D), lambda i, j: (0, 0)),
            pl.BlockSpec((D, tna), lambda i, j: (0, ja(j))),
            pl.BlockSpec((D, tnb), lambda i, j: (0, jb(j))),
        ],
        out_specs=[pl.BlockSpec((tm, tna), lambda i, j: (i, ja(j))),
                   pl.BlockSpec((tm, tnb), lambda i, j: (i, jb(j)))],
        out_shape=[jax.ShapeDtypeStruct((M, wa.shape[1]), F32), jax.ShapeDtypeStruct((M, wb.shape[1]), F32)],
        scratch_shapes=[pltpu.VMEM((tm, D), BF16)],
        compiler_params=_cparams("parallel", "arbitrary"),
        name="inproj2",
    )(x, g.reshape(1, D), wa, wb)


def _rope_pair(blk, cs, keep):
    prod = blk * cs
    return jnp.where(keep, prod + pltpu.roll(prod, 64, axis=1), 0.0)


def _prep_common(small_ref, cs_ref, nq_ref, nkv_ref, wq_ref, ckv_ref, kr_ref):
    sm = small_ref[...]
    cs = cs_ref[...]
    keep = lax.broadcasted_iota(jnp.int32, cs.shape, 1) < QK_ROPE
    cq = _rms(sm[:, COL_QA:COL_QA + Q_LORA], nq_ref[...]).astype(BF16)
    q = _dot(cq, wq_ref[...]) * (ATTN_SCALE * LOG2E)
    ckv = _rms(sm[:, COL_LAT:COL_LAT + KV_LORA], nkv_ref[...])
    ckv_ref[...] = ckv
    krot = _rope_pair(sm[:, COL_KR:COL_KR + 128], cs, keep)
    kr_ref[...] = krot[:, :QK_ROPE]
    return q, ckv, krot, cs, keep


def _prep_prompt_kernel(small_ref, cs_ref, nq_ref, nkv_ref, wq_ref, wuk_ref, wuv_ref,
                        ckv_ref, kr_ref, q_ref, k_ref, v_ref):
    q, ckv, krot, cs, keep = _prep_common(small_ref, cs_ref, nq_ref, nkv_ref, wq_ref, ckv_ref, kr_ref)
    ckv_b = ckv.astype(BF16)
    kn = _dot(ckv_b, wuk_ref[...])
    krot_b = krot.astype(BF16)
    for h in range(N_HEADS_A):
        c0 = h * QK_HEAD
        q_ref[:, c0:c0 + 128] = q[:, c0:c0 + 128].astype(BF16)
        q_ref[:, c0 + 128:c0 + 256] = _rope_pair(q[:, c0 + 128:c0 + 256], cs, keep).astype(BF16)
        k_ref[:, c0:c0 + 128] = kn[:, h * 128:(h + 1) * 128].astype(BF16)
        k_ref[:, c0 + 128:c0 + 256] = krot_b
    v_ref[...] = _dot(ckv_b, wuv_ref[...]).astype(BF16)


def _prep_sample_kernel(small_ref, cs_ref, nq_ref, nkv_ref, wq_ref, wukt_ref,
                        ckv_ref, kr_ref, qlat_ref, qrot_ref):
    q, _, _, cs, keep = _prep_common(small_ref, cs_ref, nq_ref, nkv_ref, wq_ref, ckv_ref, kr_ref)
    for h in range(N_HEADS_A):
        c0 = h * QK_HEAD
        qn = q[:, c0:c0 + 128].astype(BF16)
        qlat_ref[:, h * KV_LORA:(h + 1) * KV_LORA] = _dot(qn, wukt_ref[h]).astype(BF16)
        qrot_ref[:, h * 128:(h + 1) * 128] = _rope_pair(q[:, c0 + 128:c0 + 256], cs, keep).astype(BF16)


def _prep(proj, cs_table, nq, nkv, wq, extra_w, *, prompt, tm=512):
    M = proj.shape[0]
    tm = min(tm, M)
    n_cs = cs_table.shape[0] // tm
    H = N_HEADS_A
    row = lambda i: (i, 0)
    const2 = lambda i: (0, 0)
    in_specs = [
        pl.BlockSpec((tm, SMALL_W), lambda i: (i, CONV_DIM // SMALL_W)),
        pl.BlockSpec((tm, 128), lambda i: (i % n_cs, 0)),
        pl.BlockSpec((1, Q_LORA), const2),
        pl.BlockSpec((1, KV_LORA), const2),
        pl.BlockSpec((Q_LORA, H * QK_HEAD), const2),
    ]
    out_specs = [pl.BlockSpec((tm, KV_LORA), row), pl.BlockSpec((tm, QK_ROPE), row)]
    out_shape = [jax.ShapeDtypeStruct((M, KV_LORA), F32), jax.ShapeDtypeStruct((M, QK_ROPE), F32)]
    if prompt:
        w_uk, w_uv = extra_w
        in_specs += [pl.BlockSpec((KV_LORA, H * QK_NOPE), const2), pl.BlockSpec((KV_LORA, H * V_DIM), const2)]
        out_specs += [pl.BlockSpec((tm, H * QK_HEAD), row), pl.BlockSpec((tm, H * QK_HEAD), row),
                      pl.BlockSpec((tm, H * V_DIM), row)]
        out_shape += [jax.ShapeDtypeStruct((M, H * QK_HEAD), BF16), jax.ShapeDtypeStruct((M, H * QK_HEAD), BF16),
                      jax.ShapeDtypeStruct((M, H * V_DIM), BF16)]
        body, extra = _prep_prompt_kernel, [w_uk, w_uv]
    else:
        (w_ukt,) = extra_w
        in_specs += [pl.BlockSpec((H, QK_NOPE, KV_LORA), lambda i: (0, 0, 0))]
        out_specs += [pl.BlockSpec((tm, H * KV_LORA), row), pl.BlockSpec((tm, H * 128), row)]
        out_shape += [jax.ShapeDtypeStruct((M, H * KV_LORA), BF16), jax.ShapeDtypeStruct((M, H * 128), BF16)]
        body, extra = _prep_sample_kernel, [w_ukt]
    return pl.pallas_call(
        body,
        grid=(M // tm,),
        in_specs=in_specs,
        out_specs=out_specs,
        out_shape=out_shape,
        compiler_params=_cparams("parallel"),
        name="prep_prompt" if prompt else "prep_sample",
    )(proj, cs_table, nq.reshape(1, -1), nkv.reshape(1, -1), wq, *extra)


def _attn_prompt_kernel(q_ref, k_ref, v_ref, bias_ref, o_ref, s_scr, p_scr, *, seq, tq):
    for qi in reversed(range(seq // tq)):
        kv = (qi + 1) * tq
        s_buf = s_scr.at[qi % 2]
        p_buf = p_scr.at[qi % 2]
        q = q_ref[qi * tq:kv, :]
        s_buf[:, :kv] = _dot_nt(q, k_ref[:kv, :])
        s_buf[:, kv - tq:kv] = s_buf[:, kv - tq:kv] + bias_ref[...]
        m = jnp.max(s_buf[:, :kv], axis=-1, keepdims=True)
        p = jnp.exp2(s_buf[:, :kv] - m)
        l = jnp.sum(p, axis=-1, keepdims=True)
        p_buf[:, :kv] = p.astype(BF16)
        acc = _dot(p_buf[:, :kv], v_ref[:kv, :])
        o_ref[qi * tq:kv, :] = (acc / l).astype(o_ref.dtype)


def _attn_prompt(q, k, v, *, tq=512):
    B, S, _ = q.shape
    H = N_HEADS_A
    chunk = np.arange(tq) // CHUNK
    bias = jnp.asarray(np.where(chunk[None, :] <= chunk[:, None], 0.0, -np.inf).astype(np.float32))
    return pl.pallas_call(
        functools.partial(_attn_prompt_kernel, seq=S, tq=tq),
        grid=(B, H),
        in_specs=[
            pl.BlockSpec((None, S, QK_HEAD), lambda b, h: (b, 0, h)),
            pl.BlockSpec((None, S, QK_HEAD), lambda b, h: (b, 0, h)),
            pl.BlockSpec((None, S, V_DIM), lambda b, h: (b, 0, h)),
            pl.BlockSpec((tq, tq), lambda b, h: (0, 0)),
        ],
        out_specs=pl.BlockSpec((None, S, V_DIM), lambda b, h: (b, 0, h)),
        out_shape=jax.ShapeDtypeStruct((B, S, H * V_DIM), BF16),
        scratch_shapes=[pltpu.VMEM((2, tq, S), F32), pltpu.VMEM((2, tq, S), BF16)],
        compiler_params=_cparams("parallel", "parallel"),
        name="attn_prompt",
    )(q, k, v, bias)


def _attn_sample_kernel(qlat_ref, qrot_ref, clat_ref, ckrt_ref, nlat_ref, nkr_ref, o_ref):
    H = N_HEADS_A
    T = nlat_ref.shape[1]
    for sb in range(clat_ref.shape[0]):
        rows = slice(sb * T, (sb + 1) * T)
        ql = jnp.concatenate([qlat_ref[rows, h * KV_LORA:(h + 1) * KV_LORA] for h in range(H)], axis=0)
        qr = jnp.concatenate([qrot_ref[rows, h * 128:h * 128 + QK_ROPE] for h in range(H)], axis=0)
        lat = clat_ref[sb].astype(BF16)
        nlat = nlat_ref[sb].astype(BF16)
        s1 = _dot_nt(ql, lat) + _dot(qr, ckrt_ref[sb].astype(BF16))
        s2 = _dot_nt(ql, nlat) + _dot_nt(qr, nkr_ref[sb].astype(BF16))
        m = jnp.maximum(jnp.max(s1, axis=-1, keepdims=True), jnp.max(s2, axis=-1, keepdims=True))
        p1 = jnp.exp2(s1 - m)
        p2 = jnp.exp2(s2 - m)
        l = jnp.sum(p1, axis=-1, keepdims=True) + jnp.sum(p2, axis=-1, keepdims=True)
        o = _dot(p1.astype(BF16), lat) + _dot(p2.astype(BF16), nlat)
        o = (o / l).astype(o_ref.dtype)
        for h in range(H):
            o_ref[rows, h * KV_LORA:(h + 1) * KV_LORA] = o[h * T:(h + 1) * T, :]


def _attn_sample(qlat, qrot, cache_lat, cache_kr_t, new_lat, new_kr, *, streams):
    Bs, P, _ = cache_lat.shape
    T = new_lat.shape[1]
    H = N_HEADS_A
    ns = streams
    b2 = lambda b: (b, 0)
    b3 = lambda b: (b, 0, 0)
    return pl.pallas_call(
        _attn_sample_kernel,
        grid=(Bs // ns,),
        in_specs=[
            pl.BlockSpec((ns * T, H * KV_LORA), b2),
            pl.BlockSpec((ns * T, H * 128), b2),
            pl.BlockSpec((ns, P, KV_LORA), b3),
            pl.BlockSpec((ns, QK_ROPE, P), b3),
            pl.BlockSpec((ns, T, KV_LORA), b3),
            pl.BlockSpec((ns, T, QK_ROPE), b3),
        ],
        out_specs=pl.BlockSpec((ns * T, H * KV_LORA), b2),
        out_shape=jax.ShapeDtypeStruct((Bs * T, H * KV_LORA), BF16),
        compiler_params=_cparams("parallel"),
        name="attn_sample",
    )(qlat, qrot, cache_lat, cache_kr_t, new_lat, new_kr)


def _uv_kernel(x_ref, w_ref, o_ref):
    o_ref[...] = _dot(x_ref[...], w_ref[...]).astype(o_ref.dtype)


def _uv_proj(olat, w_uv_heads):
    M = olat.shape[0]
    H = N_HEADS_A
    return pl.pallas_call(
        _uv_kernel,
        grid=(H,),
        in_specs=[pl.BlockSpec((M, KV_LORA), lambda h: (0, h)),
                  pl.BlockSpec((None, KV_LORA, V_DIM), lambda h: (h, 0, 0))],
        out_specs=pl.BlockSpec((M, V_DIM), lambda h: (0, h)),
        out_shape=jax.ShapeDtypeStruct((M, H * V_DIM), BF16),
        compiler_params=_cparams("parallel"),
        name="uv_proj",
    )(olat, w_uv_heads)


def _split3(v):
    hi = v.astype(BF16)
    r1 = v - hi.astype(F32)
    mid = r1.astype(BF16)
    lo = (r1 - mid.astype(F32)).astype(BF16)
    return jnp.concatenate([hi, mid, lo], axis=1)


def _ssd_kernel(x_ref, z_ref, b_ref, c_ref, dt_ref, cst_ref, cw_ref, cbias_ref, dtb_ref, alog_ref,
                dskx_ref, nssm_ref, e64_ref, e128_ref, s0_ref,
                y_ref, sout_ref, xbuf, bbuf, cbuf, xs_scr, bm_scr, cm_scr, s_scr, *, q_len, n_chunks):
    c = pl.program_id(1)
    Q = q_len
    P = HEAD_DIM_S
    G, R, N, W = N_GROUPS_S, HEADS_PER_GROUP, D_STATE, GROUP_W
    h0 = HIST - (CONV_W - 1)

    @pl.when(c == 0)
    def _():
        for buf, lo, hi in ((xbuf, 0, D_INNER), (bbuf, D_INNER, D_INNER + BC_W), (cbuf, D_INNER + BC_W, CONV_DIM)):
            buf[0:h0, :] = jnp.zeros((h0, hi - lo), F32)
            buf[h0:HIST, :] = cst_ref[:, lo:hi]
        s_scr[...] = s0_ref[...]

    def conv(hist, cur_ref, lo, hi):
        cur = cur_ref[...]
        out = _causal_conv_silu(hist[...], cur, cw_ref[:, lo:hi], cbias_ref[:, lo:hi])
        hist[...] = cur[Q - HIST:, :]
        return out

    xs_scr[...] = conv(xbuf, x_ref, 0, D_INNER)
    bm_scr[...] = conv(bbuf, b_ref, D_INNER, D_INNER + BC_W).astype(BF16)
    cm_scr[...] = conv(cbuf, c_ref, D_INNER + BC_W, CONV_DIM).astype(BF16)

    dt_all = jax.nn.softplus(dt_ref[...] + dtb_ref[...])
    a_all = dt_all * (-LOG2E * jnp.exp(alog_ref[...]))
    ri = lax.broadcasted_iota(jnp.int32, (Q, Q), 0)
    ci = lax.broadcasted_iota(jnp.int32, (Q, Q), 1)
    causal = ci <= ri
    hp = lax.Precision.HIGHEST
    acs_all = jnp.dot(causal.astype(F32), a_all, precision=hp, preferred_element_type=F32)
    acs_row = lax.dot_general(a_all, (ri <= ci).astype(F32), (((0,), (0,)), ((), ())),
                              precision=hp, preferred_element_type=F32)
    a_end = acs_all[Q - 1:Q, :]
    eacs_all = jnp.exp2(acs_all)
    w_all = dt_all * jnp.exp2(a_end - acs_all)
    dec_all = jnp.exp2(a_end)
    left = lax.broadcasted_iota(jnp.int32, (Q, 128), 1) < P
    e64 = e64_ref[...]
    e128 = e128_ref[...]

    for g in range(G):
        h_lo = g * R
        dtx = _dot(_split3(dt_all[:, h_lo:h_lo + R]), e64)
        eacsx = _dot(_split3(eacs_all[:, h_lo:h_lo + R]), e64)
        wx = _dot(_split3(w_all[:, h_lo:h_lo + R]), e64)
        acsb = _dot(_split3(acs_all[:, h_lo:h_lo + R]), e128)
        xs = xs_scr[:, g * W:(g + 1) * W]
        bm = bm_scr[:, g * N:(g + 1) * N]
        cm = cm_scr[:, g * N:(g + 1) * N]
        xdt = xs * dtx
        cb = _dot_nt(cm, bm)
        y_off = _dot_nt(cm, s_scr[g].astype(BF16))
        ys = []
        for j in range(R // 2):
            ms = []
            for r in (2 * j, 2 * j + 1):
                seg = acsb[:, r * 128:r * 128 + Q] - acs_row[h_lo + r:h_lo + r + 1, :]
                ms.append((cb * jnp.exp2(jnp.where(causal, seg, -jnp.inf))).astype(BF16))
            xp = xdt[:, j * 128:(j + 1) * 128]
            rhs = jnp.concatenate([jnp.where(left, xp, 0.0), jnp.where(left, 0.0, xp)], axis=0).astype(BF16)
            ys.append(_dot(jnp.concatenate(ms, axis=1), rhs))
        y = jnp.concatenate(ys, axis=1) + y_off * eacsx + dskx_ref[:, g * W:(g + 1) * W] * xs

        st = _dot_tn((xs * wx).astype(BF16), bm)
        for r in range(R):
            dec = dec_all[:, h_lo + r:h_lo + r + 1]
            s_scr[g, r * P:(r + 1) * P, :] = dec * s_scr[g, r * P:(r + 1) * P, :] + st[r * P:(r + 1) * P, :]

        v = y * _silu(z_ref[:, g * W:(g + 1) * W])
        y_ref[:, g * W:(g + 1) * W] = _rms(v, nssm_ref[:, g * W:(g + 1) * W]).astype(y_ref.dtype)

    @pl.when(c == n_chunks - 1)
    def _():
        sout_ref[...] = s_scr[...]


def _head_expander(width):
    R = HEADS_PER_GROUP
    row_head = np.arange(3 * R) % R
    col_head = np.arange(R * width) // width
    return jnp.asarray((row_head[:, None] == col_head[None, :]).astype(np.float32)).astype(BF16)


def _ssd(xs3, zg3, conv_state, conv_w, conv_b, dt_bias, a_log, d_skip, n_ssm, state0, *, q_len):
    B, L, _ = xs3.shape
    G, N = N_GROUPS_S, D_STATE
    Q = q_len
    nc = L // Q
    pad = 128 - N_HEADS_S
    const = lambda b, c: (0, 0)
    in_specs = [
        pl.BlockSpec((None, Q, D_INNER), lambda b, c: (b, c, COL_X // D_INNER)),
        pl.BlockSpec((None, Q, D_INNER), lambda b, c: (b, c, COL_Z // D_INNER)),
        pl.BlockSpec((None, Q, BC_W), lambda b, c: (b, c, COL_B // BC_W)),
        pl.BlockSpec((None, Q, BC_W), lambda b, c: (b, c, COL_C // BC_W)),
        pl.BlockSpec((None, Q, 128), lambda b, c: (b, c, (CONV_DIM + COL_DT) // 128)),
        pl.BlockSpec((None, CONV_W - 1, CONV_DIM), lambda b, c: (b, 0, 0)),
        pl.BlockSpec((CONV_W, CONV_DIM), const),
        pl.BlockSpec((1, CONV_DIM), const),
        pl.BlockSpec((1, 128), const),
        pl.BlockSpec((1, 128), const),
        pl.BlockSpec((1, D_INNER), const),
        pl.BlockSpec((1, D_INNER), const),
        pl.BlockSpec((3 * HEADS_PER_GROUP, GROUP_W), const),
        pl.BlockSpec((3 * HEADS_PER_GROUP, HEADS_PER_GROUP * 128), const),
        pl.BlockSpec((None, G, GROUP_W, N), lambda b, c: (b, 0, 0, 0)),
    ]
    args = [xs3, zg3, xs3, xs3, xs3, conv_state, conv_w, conv_b.reshape(1, CONV_DIM),
            jnp.pad(dt_bias, (0, pad)).reshape(1, 128), jnp.pad(a_log, (0, pad)).reshape(1, 128),
            jnp.repeat(d_skip, HEAD_DIM_S).reshape(1, D_INNER), n_ssm.reshape(1, D_INNER),
            _head_expander(HEAD_DIM_S), _head_expander(128), state0.reshape(B, G, GROUP_W, N)]
    y, s_out = pl.pallas_call(
        functools.partial(_ssd_kernel, q_len=Q, n_chunks=nc),
        grid=(B, nc),
        in_specs=in_specs,
        out_specs=[pl.BlockSpec((None, Q, D_INNER), lambda b, c: (b, c, 0)),
                   pl.BlockSpec((None, G, GROUP_W, N), lambda b, c: (b, 0, 0, 0))],
        out_shape=[jax.ShapeDtypeStruct((B, L, D_INNER), BF16),
                   jax.ShapeDtypeStruct((B, G, GROUP_W, N), F32)],
        scratch_shapes=[pltpu.VMEM((HIST, D_INNER), F32), pltpu.VMEM((HIST, BC_W), F32),
                        pltpu.VMEM((HIST, BC_W), F32), pltpu.VMEM((Q, D_INNER), F32),
                        pltpu.VMEM((Q, BC_W), BF16), pltpu.VMEM((Q, BC_W), BF16),
                        pltpu.VMEM((G, GROUP_W, N), F32)],
        compiler_params=_cparams("parallel", "arbitrary"),
        name="ssd",
    )(*args)
    return y, s_out.reshape(B, N_HEADS_S, HEAD_DIM_S, N)


def _merge_kernel(oa_ref, ys_ref, wa_ref, ws_ref, g0_ref, g1_ref, bg_ref, o_ref, *w_out_refs):
    wa, ws = wa_ref[...].astype(BF16), ws_ref[...].astype(BF16)
    if w_out_refs:
        w_out_refs[0][...], w_out_refs[1][...] = wa, ws
    attn = _dot(oa_ref[...], wa)
    ssm = _dot(ys_ref[...], ws)
    bg = bg_ref[...]
    g0 = jax.nn.sigmoid(g0_ref[...] + bg[0:1, :])
    g1 = jax.nn.sigmoid(g1_ref[...] + bg[1:2, :])
    o_ref[...] = (g0 * attn + g1 * ssm).astype(o_ref.dtype)


def _merge(o_attn, y_ssm, w_o_attn, w_o_ssm, proj, b_gate, *, tm=512, tn=512):
    M = o_attn.shape[0]
    tm = min(tm, M)
    emit = w_o_attn.dtype != BF16
    assert not emit or M == tm
    g0blk = COL_GATE // tn
    g1blk = (COL_GATE + D_MODEL) // tn
    col = lambda i, j: (0, j)
    w_specs = [pl.BlockSpec((w_o_attn.shape[0], tn), col), pl.BlockSpec((D_INNER, tn), col)]
    out_specs = [pl.BlockSpec((tm, tn), lambda i, j: (i, j))]
    out_shape = [jax.ShapeDtypeStruct((M, D_MODEL), BF16)]
    if emit:
        out_specs += w_specs
        out_shape += [jax.ShapeDtypeStruct(w.shape, BF16) for w in (w_o_attn, w_o_ssm)]
    out = pl.pallas_call(
        _merge_kernel,
        grid=(M // tm, D_MODEL // tn),
        in_specs=[
            pl.BlockSpec((tm, o_attn.shape[1]), lambda i, j: (i, 0)),
            pl.BlockSpec((tm, D_INNER), lambda i, j: (i, 0)),
            *w_specs,
            pl.BlockSpec((tm, tn), lambda i, j: (i, g0blk + j)),
            pl.BlockSpec((tm, tn), lambda i, j: (i, g1blk + j)),
            pl.BlockSpec((2, tn), col),
        ],
        out_specs=out_specs,
        out_shape=out_shape,
        compiler_params=_cparams("parallel", "arbitrary"),
        name="merge",
    )(o_attn, y_ssm, w_o_attn, w_o_ssm, proj, proj, b_gate)
    return out if emit else out[0]


def _outproj_kernel(m_ref, w_ref, x_ref, o_ref, *w_out_refs):
    w = w_ref[...].astype(BF16)
    if w_out_refs:
        w_out_refs[0][...] = w
    o_ref[...] = x_ref[...] + _dot(m_ref[...], w)


def _outproj(merged, w_out, x, *, tm=512, tn=512):
    M = x.shape[0]
    tm = min(tm, M)
    emit = w_out.dtype != BF16
    assert not emit or M == tm
    w_spec = pl.BlockSpec((D_MODEL, tn), lambda i, j: (0, j))
    out_specs = [pl.BlockSpec((tm, tn), lambda i, j: (i, j))]
    out_shape = [jax.ShapeDtypeStruct((M, D_MODEL), F32)]
    if emit:
        out_specs.append(w_spec)
        out_shape.append(jax.ShapeDtypeStruct(w_out.shape, BF16))
    out = pl.pallas_call(
        _outproj_kernel,
        grid=(M // tm, D_MODEL // tn),
        in_specs=[
            pl.BlockSpec((tm, D_MODEL), lambda i, j: (i, 0)),
            w_spec,
            pl.BlockSpec((tm, tn), lambda i, j: (i, j)),
        ],
        out_specs=out_specs,
        out_shape=out_shape,
        compiler_params=_cparams("parallel", "arbitrary"),
        name="outproj",
    )(merged, w_out, x)
    return out if emit else out[0]


def _swap_halves(w):
    half = QK_ROPE // 2
    return jnp.concatenate([w[..., half:], w[..., :half]], axis=-1)


def _rope_table(pos):
    half = QK_ROPE // 2
    inv = np.power(ROPE_THETA, -np.arange(half, dtype=np.float64) / half)
    ang = np.asarray(pos, np.float64)[:, None] * inv[None, :]
    cos, sin = np.cos(ang), np.sin(ang)
    return np.concatenate([cos, cos, -sin, sin], axis=-1).astype(np.float32)


def kernel(x_prompt, x_sample, cache_kv_latent, cache_k_rope, state_ssm, state_conv, norm_ffn1, w_ffn1_gate, w_ffn1_up, w_ffn1_down, norm_mix, w_in, b_gate, norm_q_a, w_q_b, norm_kv_a, w_kv_b, conv_w, conv_b, dt_bias, a_log, d_skip, norm_ssm, w_o_attn, w_o_ssm, w_out, norm_ffn2, w_ffn2_gate, w_ffn2_up, w_ffn2_down, norm_final):
    Bp, S, D = x_prompt.shape
    Bs, T, _ = x_sample.shape
    past = cache_kv_latent.shape[2]
    H = N_HEADS_A
    l = 0

    w_xs, w_zg = _relayout_w_in(w_in[l])
    wq3 = w_q_b[l].reshape(Q_LORA, H, QK_NOPE + QK_ROPE)
    wq = jnp.concatenate([wq3, _swap_halves(wq3[..., QK_NOPE:])], axis=-1).reshape(Q_LORA, H * QK_HEAD).astype(BF16)
    wkv3 = w_kv_b[l].reshape(KV_LORA, H, QK_NOPE + V_DIM)
    w_uk = wkv3[..., :QK_NOPE]
    w_uv = wkv3[..., QK_NOPE:]
    w_uk_flat = w_uk.reshape(KV_LORA, H * QK_NOPE).astype(BF16)
    w_uv_flat = w_uv.reshape(KV_LORA, H * V_DIM).astype(BF16)
    w_uk_t = w_uk.transpose(1, 2, 0).astype(BF16)
    w_uv_h = w_uv.transpose(1, 0, 2).astype(BF16)
    def mixer_mid(x1, batch):
        xs, zg = _inproj2(x1, norm_mix[l], w_xs, w_zg, tm=1024, tna=SMALL_W // 2, tnb=1024)
        xs3 = xs.reshape(batch, -1, XS_W)
        return xs, zg, xs3[:, xs3.shape[1] - (CONV_W - 1):, :CONV_DIM]


    def ssd(xs, zg, batch, conv_state, state, q_len):
        y, new_state = _ssd(xs.reshape(batch, -1, XS_W), zg.reshape(batch, -1, ZG_W), conv_state,
                            conv_w[l], conv_b[l], dt_bias[l], a_log[l], d_skip[l], norm_ssm[l], state, q_len=q_len)
        return y.reshape(-1, D_INNER), new_state

    x1s, wg1, wu1, wd1 = _ffn(x_sample.reshape(Bs * T, D), norm_ffn1[l], w_ffn1_gate[l], w_ffn1_up[l],
                              w_ffn1_down[l], tf=512)
    xss, zgs, s_conv = mixer_mid(x1s, Bs)
    cs_s = jnp.asarray(np.tile(_rope_table(past + np.arange(T)), (Bs, 1)))
    s_lat, s_kr, qlat, qrot = _prep(xss, cs_s, norm_q_a[l], norm_kv_a[l], wq, (w_uk_t,), prompt=False)
    o_lat = _attn_sample(qlat, qrot, cache_kv_latent[l], jnp.swapaxes(cache_k_rope[l], 1, 2),
                         s_lat.reshape(Bs, T, KV_LORA), s_kr.reshape(Bs, T, QK_ROPE),
                         streams=2 if Bs % 2 == 0 else 1)
    o_attn_s = _uv_proj(o_lat, w_uv_h)
    y_ssm_s, s_ssm = ssd(xss, zgs, Bs, state_conv[l], state_ssm[l], T)
    merged_s, woa, wos = _merge(o_attn_s, y_ssm_s, w_o_attn[l], w_o_ssm[l], zgs, b_gate[l], tm=1024, tn=256)
    x2s, wo = _outproj(merged_s, w_out[l], x1s, tm=1024, tn=1024)
    y_sample, wg2, wu2, wd2 = _ffn(x2s, norm_ffn2[l], w_ffn2_gate[l], w_ffn2_up[l], w_ffn2_down[l], norm_final,
                                   tf=512)
    y_sample = y_sample.reshape(Bs, T, D)

    x1 = _ffn(x_prompt.reshape(Bp * S, D), norm_ffn1[l], wg1, wu1, wd1)
    xs, zg, p_conv = mixer_mid(x1, Bp)
    cs_p = jnp.asarray(_rope_table(np.arange(S)))
    p_lat, p_kr, q, k, v = _prep(xs, cs_p, norm_q_a[l], norm_kv_a[l], wq, (w_uk_flat, w_uv_flat), prompt=True)
    o_attn = _attn_prompt(q.reshape(Bp, S, -1), k.reshape(Bp, S, -1), v.reshape(Bp, S, -1)).reshape(Bp * S, -1)
    y_ssm, p_ssm = ssd(xs, zg, Bp, jnp.zeros((Bp, CONV_W - 1, CONV_DIM), F32),
                       jnp.zeros((Bp, N_HEADS_S, HEAD_DIM_S, D_STATE), F32), 128)
    merged = _merge(o_attn, y_ssm, woa, wos, zg, b_gate[l], tm=1024, tn=512)
    x2 = _outproj(merged, wo, x1, tm=512, tn=D_MODEL)
    y_prompt = _ffn(x2, norm_ffn2[l], wg2, wu2, wd2, norm_final).reshape(Bp, S, D)

    return (y_prompt, y_sample,
            p_lat.reshape(1, Bp, S, KV_LORA), p_kr.reshape(1, Bp, S, QK_ROPE), p_ssm[None], p_conv[None],
            s_lat.reshape(1, Bs, T, KV_LORA), s_kr.reshape(1, Bs, T, QK_ROPE), s_ssm[None], s_conv[None])
```
